```python
import math
import jax, jax.numpy as jnp
from jax import lax
import numpy as np

D_MODEL = 2048
BATCH = 2
SEQ = 4096
DEPTH = 4

GRID_W = 64
CTX_LEN = 256

FOURIER_GROUPS = 4
FOURIER_GROUP_W = D_MODEL // 16
FOURIER_W = FOURIER_GROUPS * FOURIER_GROUP_W
CONV_W = D_MODEL // 4
N_HEADS = 8
HEAD_DIM = D_MODEL // (4 * N_HEADS)
VAL_DIM = 2 * HEAD_DIM
ATTN_QK_W = N_HEADS * 2 * HEAD_DIM
ATTN_V_W = N_HEADS * VAL_DIM
ROPE_THETA = 10000.0
Q_BLOCK = 128
ATTN_SCALE = HEAD_DIM ** -0.5
SUBLN_EPS = 1e-5
N_BRANCH = 3
Q_OFF = FOURIER_W + 3 * CONV_W
K_OFF = Q_OFF + ATTN_QK_W
V_END = K_OFF + ATTN_QK_W + ATTN_V_W
N_IN = V_END + N_BRANCH * D_MODEL
D_FF = ((8 * D_MODEL // 3 + 255) // 256) * 256
EPS = 1e-6

kernel_name = "hybrid_fourier_shortconv_diffattn_prefix_dit"


def rmsnorm(x, g, eps=EPS):
    xf = x.astype(jnp.float32)
    y = xf * lax.rsqrt(jnp.mean(xf * xf, axis=-1, keepdims=True) + eps)
    return (y * g.astype(jnp.float32)).astype(x.dtype)


def modulate(x, shift, scale):
    return x * (1.0 + scale) + shift


def dwconv3(x, w):
    xp = jnp.pad(x, ((0, 0), (1, 1), (0, 0)))
    return xp[:, :-2] * w[0] + xp[:, 1:-1] * w[1] + xp[:, 2:] * w[2]


def split_in(p):
    sizes = (FOURIER_W, CONV_W, CONV_W, CONV_W, ATTN_QK_W, ATTN_QK_W, ATTN_V_W,
             D_MODEL, D_MODEL, D_MODEL)
    return jnp.split(p, [int(s) for s in np.cumsum(sizes)[:-1]], axis=-1)


def fourier_mix(u):
    b, l, _ = u.shape
    ug = u.reshape(b, l, FOURIER_GROUPS, FOURIER_GROUP_W).astype(jnp.float32)
    f = jnp.fft.fft2(ug, axes=(1, 3), norm="ortho")
    return jnp.real(f).reshape(b, l, FOURIER_W).astype(u.dtype)


def axial_rope_tables(length):
    rows = length // GRID_W
    row = jnp.repeat(jnp.arange(rows), GRID_W).astype(jnp.float32)
    col = jnp.tile(jnp.arange(GRID_W), rows).astype(jnp.float32)
    n_freq = HEAD_DIM // 4
    inv = ROPE_THETA ** (-(2.0 * jnp.arange(n_freq, dtype=jnp.float32)) / (HEAD_DIM // 2))
    ang = jnp.stack([row[:, None] * inv, col[:, None] * inv], axis=1)
    return jnp.cos(ang), jnp.sin(ang)


def rope2d(x, cos, sin):
    b, l, h, c, d = x.shape
    xr = x.reshape(b, l, h, c, 2, 2, d // 4).astype(jnp.float32)
    x1, x2 = xr[..., 0, :], xr[..., 1, :]
    cs = cos[None, :, None, None]
    sn = sin[None, :, None, None]
    out = jnp.stack([x1 * cs - x2 * sn, x2 * cs + x1 * sn], axis=-2)
    return out.reshape(b, l, h, c, d).astype(x.dtype)


def diff_lambda(lam, lam_init):
    lf = lam.astype(jnp.float32)
    return jnp.exp(jnp.sum(lf[0] * lf[1])) - jnp.exp(jnp.sum(lf[2] * lf[3])) + lam_init


def diff_attend(q, k, v, lam, subln_g, lam_init):
    s = jnp.einsum('bqhcd,bkhcd->bhcqk', q.astype(jnp.float32), k.astype(jnp.float32)) * ATTN_SCALE
    p = jax.nn.softmax(s, axis=-1)
    a = p[:, :, 0] - lam * p[:, :, 1]
    o = jnp.einsum('bhqk,bkhe->bqhe', a, v.astype(jnp.float32))
    o = o * lax.rsqrt(jnp.mean(o * o, axis=-1, keepdims=True) + SUBLN_EPS)
    o = o * subln_g.astype(jnp.float32) * (1.0 - lam_init)
    return o.reshape(o.shape[0], o.shape[1], N_HEADS * VAL_DIM).astype(v.dtype)


def diff_attend_blocks(q, k, v, lam, subln_g, lam_init):
    b, l = q.shape[0], q.shape[1]
    nblk = l // Q_BLOCK
    qb = q.reshape(b, nblk, Q_BLOCK, N_HEADS, 2, HEAD_DIM).transpose(1, 0, 2, 3, 4, 5)
    ob = lax.map(lambda qq: diff_attend(qq, k, v, lam, subln_g, lam_init), qb)
    return ob.transpose(1, 0, 2, 3).reshape(b, l, N_HEADS * VAL_DIM)


def merge_branches(f_in, cb, cc, cx, attn_o, g_f, g_c, g_a, conv_w, w_bf, w_bc, w_ba, w_o):
    y_f = fourier_mix(f_in) @ w_bf
    y_c = (cb * dwconv3(cc * cx, conv_w)) @ w_bc
    y_a = attn_o @ w_ba
    merged = jax.nn.sigmoid(g_f) * y_f + jax.nn.sigmoid(g_c) * y_c + jax.nn.sigmoid(g_a) * y_a
    return merged @ w_o


def conv_ffn(xm, w_up, conv_w, w_down):
    u = dwconv3(xm @ w_up, conv_w)
    a, v = jnp.split(u, 2, axis=-1)
    return (jax.nn.silu(a) * v) @ w_down


def setup_inputs(seed: int = 0) -> dict:
    key = jax.random.key(seed)
    ks = jax.random.split(key, 20)
    f32 = jnp.float32
    nrm = lambda k, shape, s: jax.random.normal(k, shape, f32) * s
    return {
        "x": nrm(ks[0], (BATCH, SEQ, D_MODEL), 1.0),
        "c": nrm(ks[1], (BATCH, D_MODEL), 1.0),
        "ctx": nrm(ks[2], (BATCH, CTX_LEN, D_MODEL), 1.0),
        "c_ctx": nrm(ks[3], (D_MODEL,), 1.0),
        "w_mod": nrm(ks[4], (DEPTH, D_MODEL, 6 * D_MODEL), 0.5 * D_MODEL ** -0.5),
        "b_mod": nrm(ks[5], (DEPTH, 6 * D_MODEL), 0.01),
        "g_norm1": 1.0 + nrm(ks[6], (DEPTH, D_MODEL), 0.02),
        "g_norm2": 1.0 + nrm(ks[7], (DEPTH, D_MODEL), 0.02),
        "w_in": nrm(ks[8], (DEPTH, D_MODEL, N_IN), D_MODEL ** -0.5),
        "conv_mix_w": nrm(ks[9], (DEPTH, 3, CONV_W), 3 ** -0.5),
        "lambdas": nrm(ks[10], (DEPTH, 4, HEAD_DIM), 0.1),
        "subln_g": 1.0 + nrm(ks[11], (DEPTH, VAL_DIM), 0.02),
        "w_br_fourier": nrm(ks[12], (DEPTH, FOURIER_W, D_MODEL), FOURIER_W ** -0.5),
        "w_br_conv": nrm(ks[13], (DEPTH, CONV_W, D_MODEL), CONV_W ** -0.5),
        "w_br_attn": nrm(ks[14], (DEPTH, ATTN_V_W, D_MODEL), ATTN_V_W ** -0.5),
        "w_out": nrm(ks[15], (DEPTH, D_MODEL, D_MODEL), D_MODEL ** -0.5),
        "w_ffn_up": nrm(ks[16], (DEPTH, D_MODEL, 2 * D_FF), D_MODEL ** -0.5),
        "ffn_conv_w": nrm(ks[17], (DEPTH, 3, 2 * D_FF), 3 ** -0.5),
        "w_ffn_down": nrm(ks[18], (DEPTH, D_FF, D_MODEL), D_FF ** -0.5),
        "g_final": 1.0 + nrm(ks[19], (D_MODEL,), 0.02),
    }


def reference(x, c, ctx, c_ctx, w_mod, b_mod, g_norm1, g_norm2, w_in, conv_mix_w, lambdas,
              subln_g, w_br_fourier, w_br_conv, w_br_attn, w_out, w_ffn_up, ffn_conv_w,
              w_ffn_down, g_final):
    b, l, _ = x.shape
    n_ctx = ctx.shape[1]
    cos, sin = axial_rope_tables(l)
    silu_c = jax.nn.silu(c)
    silu_cc = jax.nn.silu(c_ctx)
    h, hc = x, ctx
    for i in range(DEPTH):
        last = i == DEPTH - 1
        lam_init = 0.8 - 0.6 * math.exp(-0.3 * i)
        lam = diff_lambda(lambdas[i], lam_init)
        sh1, sc1, gt1, sh2, sc2, gt2 = jnp.split((silu_c @ w_mod[i] + b_mod[i])[:, None, :], 6, axis=-1)
        shc1, scc1, gtc1, shc2, scc2, gtc2 = jnp.split(silu_cc @ w_mod[i] + b_mod[i], 6, axis=-1)

        xc = modulate(rmsnorm(hc, g_norm1[i]), shc1, scc1)
        if last:
            kc, vc = jnp.split(xc @ w_in[i][:, K_OFF:V_END], [ATTN_QK_W], axis=-1)
        else:
            fc, cbc, ccc, cxc, qc, kc, vc, gfc, gcc, gac = split_in(xc @ w_in[i])
        kc = kc.reshape(b, n_ctx, N_HEADS, 2, HEAD_DIM)
        vc = vc.reshape(b, n_ctx, N_HEADS, VAL_DIM)

        xl = modulate(rmsnorm(h, g_norm1[i]), sh1, sc1)
        fl, cbl, ccl, cxl, ql, kl, vl, gfl, gcl, gal = split_in(xl @ w_in[i])
        ql = rope2d(ql.reshape(b, l, N_HEADS, 2, HEAD_DIM), cos, sin)
        kl = rope2d(kl.reshape(b, l, N_HEADS, 2, HEAD_DIM), cos, sin)
        k_all = jnp.concatenate([kc, kl], axis=1)
        v_all = jnp.concatenate([vc, vl.reshape(b, l, N_HEADS, VAL_DIM)], axis=1)
        ol = diff_attend_blocks(ql, k_all, v_all, lam, subln_g[i], lam_init)
        h = h + gt1 * merge_branches(fl, cbl, ccl, cxl, ol, gfl, gcl, gal, conv_mix_w[i],
                                     w_br_fourier[i], w_br_conv[i], w_br_attn[i], w_out[i])
        h = h + gt2 * conv_ffn(modulate(rmsnorm(h, g_norm2[i]), sh2, sc2),
                               w_ffn_up[i], ffn_conv_w[i], w_ffn_down[i])

        if not last:
            qc = qc.reshape(b, n_ctx, N_HEADS, 2, HEAD_DIM)
            oc = diff_attend(qc, kc, vc, lam, subln_g[i], lam_init)
            hc = hc + gtc1 * merge_branches(fc, cbc, ccc, cxc, oc, gfc, gcc, gac, conv_mix_w[i],
                                            w_br_fourier[i], w_br_conv[i], w_br_attn[i], w_out[i])
            hc = hc + gtc2 * conv_ffn(modulate(rmsnorm(hc, g_norm2[i]), shc2, scc2),
                                      w_ffn_up[i], ffn_conv_w[i], w_ffn_down[i])
    return rmsnorm(h, g_final)
```

```python
import functools
import math

import numpy as np
import jax
import jax.numpy as jnp
from jax import lax
from jax.experimental import pallas as pl
from jax.experimental.pallas import tpu as pltpu

F32 = jnp.float32
BF16 = jnp.bfloat16

D_MODEL = 2048
GRID_W = 64
FOURIER_GROUPS = 4
FOURIER_GROUP_W = D_MODEL // 16
FOURIER_W = FOURIER_GROUPS * FOURIER_GROUP_W
CONV_W = D_MODEL // 4
N_HEADS = 8
HEAD_DIM = D_MODEL // (4 * N_HEADS)
VAL_DIM = 2 * HEAD_DIM
ATTN_QK_W = N_HEADS * 2 * HEAD_DIM
ATTN_V_W = N_HEADS * VAL_DIM
ROPE_THETA = 10000.0
ATTN_SCALE = HEAD_DIM ** -0.5
SUBLN_EPS = 1e-5
Q_OFF = FOURIER_W + 3 * CONV_W
K_OFF = Q_OFF + ATTN_QK_W
V_OFF = K_OFF + ATTN_QK_W
V_END = V_OFF + ATTN_V_W
N_IN = V_END + 3 * D_MODEL
D_FF = ((8 * D_MODEL // 3 + 255) // 256) * 256
EPS = 1e-6

LANES = 128
BF16_SUBLANES = 16
VMEM_LIMIT = 56 * 1024 * 1024
MOD_ROWS = 8

PROJ_TN = 1024
P_GATE_OFF = Q_OFF
P_Q_OFF = P_GATE_OFF + 3 * D_MODEL
P_K_OFF = P_Q_OFF + ATTN_QK_W
P_V_OFF = P_K_OFF + ATTN_QK_W
PROJ_GATE_TILE = P_GATE_OFF // PROJ_TN
PROJ_Q_TILE = P_Q_OFF // PROJ_TN
PROJ_K_TILE = P_K_OFF // PROJ_TN
PROJ_TILES = N_IN // PROJ_TN
FFN_TF = 512


def _cparams(*sem):
    return pltpu.CompilerParams(dimension_semantics=sem, vmem_limit_bytes=VMEM_LIMIT)


def _mod_kernel(c_ref, w_ref, b_ref, o_ref):
    c = c_ref[...]
    s = c * jax.nn.sigmoid(c)
    s_hi = s.astype(BF16)
    s_lo = (s - s_hi.astype(F32)).astype(BF16)
    w = w_ref[...]
    w_hi = w.astype(BF16)
    w_lo = (w - w_hi.astype(F32)).astype(BF16)
    y = jnp.dot(s_hi, w_hi, preferred_element_type=F32)
    y += jnp.dot(s_lo, w_hi, preferred_element_type=F32)
    y += jnp.dot(s_hi, w_lo, preferred_element_type=F32)
    o_ref[...] = y + b_ref[...]


def _modulation(cvec, w_mod, b_mod):
    depth, d, n = w_mod.shape
    tn = 1024
    return pl.pallas_call(
        _mod_kernel,
        grid=(depth, n // tn),
        in_specs=[
            pl.BlockSpec((MOD_ROWS, d), lambda l, j: (0, 0)),
            pl.BlockSpec((None, d, tn), lambda l, j: (l, 0, j)),
            pl.BlockSpec((None, 1, tn), lambda l, j: (l, 0, j)),
        ],
        out_specs=pl.BlockSpec((None, MOD_ROWS, tn), lambda l, j: (l, 0, j)),
        out_shape=jax.ShapeDtypeStruct((depth, MOD_ROWS, n), F32),
        compiler_params=_cparams("arbitrary", "arbitrary"),
        name="modulation",
    )(cvec, w_mod, b_mod.reshape(depth, 1, n))


def _mod_spec(which, row_fn):
    return pl.BlockSpec((None, None, 1, D_MODEL), lambda *ids: (row_fn(*ids), which, 0, 0))


def _rms(x, g):
    return x * lax.rsqrt(jnp.mean(x * x, axis=-1, keepdims=True) + EPS) * g


def _norm_mod_kernel(h_ref, g_ref, sh_ref, sc_ref, o_ref):
    y = _rms(h_ref[...], g_ref[...])
    o_ref[...] = (y * (1.0 + sc_ref[...]) + sh_ref[...]).astype(o_ref.dtype)


def _norm_kernel(h_ref, g_ref, o_ref):
    o_ref[...] = _rms(h_ref[...], g_ref[...]).astype(o_ref.dtype)


def _norm_modulate(h, g, mods, row_of_tile, tm):
    m, d = h.shape
    return pl.pallas_call(
        _norm_mod_kernel,
        grid=(m // tm,),
        in_specs=[
            pl.BlockSpec((tm, d), lambda i: (i, 0)),
            pl.BlockSpec((1, d), lambda i: (0, 0)),
            _mod_spec(0, row_of_tile),
            _mod_spec(1, row_of_tile),
        ],
        out_specs=pl.BlockSpec((tm, d), lambda i: (i, 0)),
        out_shape=jax.ShapeDtypeStruct((m, d), BF16),
        compiler_params=_cparams("arbitrary"),
        name="norm_modulate",
    )(h, g.reshape(1, d), mods, mods)


def _final_norm(h, g, tm):
    m, d = h.shape
    return pl.pallas_call(
        _norm_kernel,
        grid=(m // tm,),
        in_specs=[pl.BlockSpec((tm, d), lambda i: (i, 0)), pl.BlockSpec((1, d), lambda i: (0, 0))],
        out_specs=pl.BlockSpec((tm, d), lambda i: (i, 0)),
        out_shape=jax.ShapeDtypeStruct((m, d), F32),
        compiler_params=_cparams("arbitrary"),
        name="final_norm",
    )(h, g.reshape(1, d))


def _proj_kernel(*refs, col0, rope):
    if rope:
        x_ref, w_ref, cos_ref, sin_ref, o_ref = refs
    else:
        x_ref, w_ref, o_ref = refs
    j = pl.program_id(0) + col0
    acc = jnp.dot(x_ref[...], w_ref[...], preferred_element_type=F32)
    is_q = j == PROJ_Q_TILE
    is_k = j == PROJ_K_TILE
    is_gate = (j >= PROJ_GATE_TILE) & (j < PROJ_Q_TILE)

    @pl.when(is_q | is_k)
    def _():
        qscale = jnp.where(is_q, ATTN_SCALE, 1.0)
        for hd in range(PROJ_TN // LANES):
            a = acc[:, hd * LANES:(hd + 1) * LANES]
            if rope:
                a = a * cos_ref[...] + pltpu.roll(a, LANES // 2, 1) * sin_ref[...]
            o_ref[:, hd * LANES:(hd + 1) * LANES] = (a * qscale).astype(o_ref.dtype)

    @pl.when(is_gate)
    def _():
        o_ref[...] = jax.nn.sigmoid(acc).astype(o_ref.dtype)

    @pl.when(jnp.logical_not(is_q | is_k | is_gate))
    def _():
        o_ref[...] = acc.astype(o_ref.dtype)


def _in_proj(xm, w_in, rope_tabs, tm, col0, ncols, seq_len):
    m, d = xm.shape
    tn = PROJ_TN
    in_specs = [
        pl.BlockSpec((tm, d), lambda j, i: (i, 0)),
        pl.BlockSpec((d, tn), lambda j, i: (0, j + col0)),
    ]
    args = [xm, w_in]
    rope = rope_tabs is not None
    if rope:
        tps = seq_len // tm
        in_specs += [pl.BlockSpec((tm, LANES), lambda j, i: (i % tps, 0))] * 2
        args += list(rope_tabs)
    return pl.pallas_call(
        functools.partial(_proj_kernel, col0=col0, rope=rope),
        grid=(ncols, m // tm),
        in_specs=in_specs,
        out_specs=pl.BlockSpec((tm, tn), lambda j, i: (i, j)),
        out_shape=jax.ShapeDtypeStruct((m, ncols * tn), BF16),
        compiler_params=_cparams("arbitrary", "arbitrary"),
        name="in_proj",
    )(*args)


def _fourier_kernel(f_ref, cs_ref, cl_ref, sl_ref, o_ref, uc_ref, us_ref, *, chunk, scale):
    i = pl.program_id(1)
    seq = f_ref.shape[0]
    gw = FOURIER_GROUP_W

    @pl.when(i == 0)
    def _():
        for r in range(seq // chunk):
            rows = slice(r * chunk, (r + 1) * chunk)
            for g in range(FOURIER_GROUPS):
                cols = slice(g * gw, (g + 1) * gw)
                y = jnp.dot(f_ref[rows, cols], cs_ref[...], preferred_element_type=F32)
                uc_ref[rows, cols] = y[:, :gw].astype(BF16)
                us_ref[rows, cols] = y[:, gw:].astype(BF16)

    y = jnp.dot(cl_ref[...], uc_ref[...], preferred_element_type=F32)
    y += jnp.dot(sl_ref[...], us_ref[...], preferred_element_type=F32)
    o_ref[...] = (y * scale).astype(o_ref.dtype)


def _fourier(p3, cs, cl, sln, tr):
    b, seq, _ = p3.shape
    chunk = min(seq, 1024)
    scale = 1.0 / math.sqrt(seq * FOURIER_GROUP_W)
    return pl.pallas_call(
        functools.partial(_fourier_kernel, chunk=chunk, scale=scale),
        grid=(b, seq // tr),
        in_specs=[
            pl.BlockSpec((None, seq, FOURIER_W), lambda bi, i: (bi, 0, 0)),
            pl.BlockSpec((FOURIER_GROUP_W, 2 * FOURIER_GROUP_W), lambda bi, i: (0, 0)),
            pl.BlockSpec((tr, seq), lambda bi, i: (i, 0)),
            pl.BlockSpec((tr, seq), lambda bi, i: (i, 0)),
        ],
        out_specs=pl.BlockSpec((None, tr, FOURIER_W), lambda bi, i: (bi, i, 0)),
        out_shape=jax.ShapeDtypeStruct((b, seq, FOURIER_W), BF16),
        scratch_shapes=[pltpu.VMEM((seq, FOURIER_W), BF16), pltpu.VMEM((seq, FOURIER_W), BF16)],
        compiler_params=_cparams("arbitrary", "arbitrary"),
        name="fourier",
    )(p3, cs, cl, sln)


def _attn_kernel(*refs, tq, tk, n_lat, lam_init):
    if n_lat:
        lam_ref, q_ref, kc_ref, vc_ref, kl_ref, vl_ref, g_ref, o_ref = refs
    else:
        lam_ref, q_ref, kc_ref, vc_ref, g_ref, o_ref = refs
    q = q_ref[...]
    lane = lax.broadcasted_iota(jnp.int32, (1, LANES), 1)
    comp0 = (lane % (LANES // 2)) < (LANES // 4)
    zero = jnp.zeros_like(q)
    qq = jnp.concatenate([jnp.where(comp0, q, zero), jnp.where(comp0, zero, q)], axis=0)

    def block(k, v, carry):
        m, l, acc = carry
        s = lax.dot_general(qq, k, (((1,), (1,)), ((), ())), preferred_element_type=F32)
        m_new = jnp.maximum(m, jnp.max(s, axis=-1, keepdims=True))
        alpha = jnp.exp(m - m_new)
        p = jnp.exp(s - m_new)
        l = alpha * l + jnp.sum(p, axis=-1, keepdims=True)
        acc = alpha * acc + jnp.dot(p.astype(BF16), v, preferred_element_type=F32)
        return m_new, l, acc

    carry = (jnp.full((2 * tq, 1), -jnp.inf, F32), jnp.zeros((2 * tq, 1), F32),
             jnp.zeros((2 * tq, VAL_DIM), F32))
    carry = block(kc_ref[...], vc_ref[...], carry)
    if n_lat:
        def body(t, c):
            rows = pl.ds(pl.multiple_of(t * tk, tk), tk)
            return block(kl_ref[rows, :], vl_ref[rows, :], c)
        carry = lax.fori_loop(0, n_lat, body, carry)
    _, l, acc = carry
    o = acc / l
    lf = lam_ref[...]
    lam = (jnp.exp(jnp.sum(lf[0:1] * lf[1:2], axis=-1, keepdims=True))
           - jnp.exp(jnp.sum(lf[2:3] * lf[3:4], axis=-1, keepdims=True)) + lam_init)
    o = o[:tq] - lam * o[tq:]
    o = o * lax.rsqrt(jnp.mean(o * o, axis=-1, keepdims=True) + SUBLN_EPS)
    o_ref[...] = (o * g_ref[...] * (1.0 - lam_init)).astype(o_ref.dtype)


def _attention(q3, qblk, c3, kcblk, vcblk, l3, lam, subln_g, lam_init, tq):
    b, lq, _ = q3.shape
    n_ctx = c3.shape[1]
    tk = 256
    in_specs = [
        pl.BlockSpec((4, HEAD_DIM), lambda bi, h, i: (0, 0)),
        pl.BlockSpec((None, tq, LANES), lambda bi, h, i: (bi, i, qblk + h)),
        pl.BlockSpec((None, n_ctx, LANES), lambda bi, h, i: (bi, 0, kcblk + h)),
        pl.BlockSpec((None, n_ctx, LANES), lambda bi, h, i: (bi, 0, vcblk + h)),
    ]
    args = [lam, q3, c3, c3]
    n_lat = 0
    if l3 is not None:
        seq = l3.shape[1]
        n_lat = seq // tk
        in_specs += [
            pl.BlockSpec((None, seq, LANES), lambda bi, h, i: (bi, 0, P_K_OFF // LANES + h)),
            pl.BlockSpec((None, seq, LANES), lambda bi, h, i: (bi, 0, P_V_OFF // LANES + h)),
        ]
        args += [l3, l3]
    in_specs.append(pl.BlockSpec((1, VAL_DIM), lambda bi, h, i: (0, 0)))
    args.append(subln_g.reshape(1, VAL_DIM))
    return pl.pallas_call(
        functools.partial(_attn_kernel, tq=tq, tk=tk, n_lat=n_lat, lam_init=lam_init),
        grid=(b, N_HEADS, lq // tq),
        in_specs=in_specs,
        out_specs=pl.BlockSpec((None, tq, LANES), lambda bi, h, i: (bi, i, h)),
        out_shape=jax.ShapeDtypeStruct((b, lq, ATTN_V_W), BF16),
        compiler_params=_cparams("arbitrary", "arbitrary", "arbitrary"),
        name="diff_attention",
    )(*args)


def _conv3_rows(buf_ref, w, tm):
    h = BF16_SUBLANES
    return (buf_ref[h - 1:h - 1 + tm, :] * w[0:1] + buf_ref[h:h + tm, :] * w[1:2]
            + buf_ref[h + 1:h + 1 + tm, :] * w[2:3])


def _halo_specs(tm, width, colblk, m):
    per = tm // BF16_SUBLANES
    last = m // BF16_SUBLANES - 1
    prev = pl.BlockSpec((BF16_SUBLANES, width), lambda i, *_: (jnp.maximum(i * per - 1, 0), colblk))
    nxt = pl.BlockSpec((BF16_SUBLANES, width), lambda i, *_: (jnp.minimum((i + 1) * per, last), colblk))
    return prev, nxt


def _merge_kernel(f_ref, cb_ref, cc_ref, cx_ref, ccp_ref, cxp_ref, ccn_ref, cxn_ref, a_ref,
                  gf_ref, gc_ref, ga_ref, cw_ref, wbf_ref, wbc_ref, wba_ref, o_ref, z_ref, *, tps, nchunk):
    i = pl.program_id(0)
    tm = cc_ref.shape[0]
    h = BF16_SUBLANES
    first = (i % tps) == 0
    last = (i % tps) == tps - 1
    zp = ccp_ref[...].astype(F32) * cxp_ref[...].astype(F32)
    zn = ccn_ref[...].astype(F32) * cxn_ref[...].astype(F32)
    z_ref[0:h, :] = jnp.where(first, 0.0, zp)
    z_ref[h:h + tm, :] = cc_ref[...].astype(F32) * cx_ref[...].astype(F32)
    z_ref[h + tm:2 * h + tm, :] = jnp.where(last, 0.0, zn)
    cv = (cb_ref[...].astype(F32) * _conv3_rows(z_ref, cw_ref[...], tm)).astype(BF16)
    f = f_ref[...]
    a = a_ref[...]
    cn = D_MODEL // nchunk
    for n in range(nchunk):
        cols = slice(n * cn, (n + 1) * cn)
        y = gf_ref[:, cols].astype(F32) * jnp.dot(f, wbf_ref[:, cols], preferred_element_type=F32)
        y += gc_ref[:, cols].astype(F32) * jnp.dot(cv, wbc_ref[:, cols], preferred_element_type=F32)
        y += ga_ref[:, cols].astype(F32) * jnp.dot(a, wba_ref[:, cols], preferred_element_type=F32)
        o_ref[:, cols] = y.astype(o_ref.dtype)


def _merge(p, fmix, attn, conv_w, w_bf, w_bc, w_ba, tm, seq_len):
    m = p.shape[0]
    d = D_MODEL
    cw = CONV_W
    tps = seq_len // tm
    blk = lambda width, c: pl.BlockSpec((tm, width), lambda i: (i, c))
    ccp, ccn = _halo_specs(tm, cw, 2, m)
    cxp, cxn = _halo_specs(tm, cw, 3, m)
    gate0 = P_GATE_OFF // d
    full = lambda r, c: pl.BlockSpec((r, c), lambda i: (0, 0))
    return pl.pallas_call(
        functools.partial(_merge_kernel, tps=tps, nchunk=4),
        grid=(m // tm,),
        in_specs=[
            blk(FOURIER_W, 0),
            blk(cw, 1), blk(cw, 2), blk(cw, 3),
            ccp, cxp, ccn, cxn,
            blk(ATTN_V_W, 0),
            pl.BlockSpec((tm, d), lambda i: (i, gate0)),
            pl.BlockSpec((tm, d), lambda i: (i, gate0 + 1)),
            pl.BlockSpec((tm, d), lambda i: (i, gate0 + 2)),
            full(3, cw), full(FOURIER_W, d), full(cw, d), full(ATTN_V_W, d),
        ],
        out_specs=pl.BlockSpec((tm, d), lambda i: (i, 0)),
        out_shape=jax.ShapeDtypeStruct((m, d), BF16),
        scratch_shapes=[pltpu.VMEM((tm + 2 * BF16_SUBLANES, cw), F32)],
        compiler_params=_cparams("arbitrary"),
        name="branch_merge",
    )(fmix, p, p, p, p, p, p, p, attn, p, p, p, conv_w, w_bf, w_bc, w_ba)


def _outproj_kernel(m_ref, w_ref, h_ref, gt_ref, g_ref, sh_ref, sc_ref, ho_ref, xo_ref):
    y = jnp.dot(m_ref[...], w_ref[...], preferred_element_type=F32)
    h = h_ref[...] + gt_ref[...] * y
    ho_ref[...] = h
    xo_ref[...] = (_rms(h, g_ref[...]) * (1.0 + sc_ref[...]) + sh_ref[...]).astype(xo_ref.dtype)


def _out_proj(merged, w_out, h, g2, mods, row_of_tile, tm):
    m, d = h.shape
    return pl.pallas_call(
        _outproj_kernel,
        grid=(m // tm,),
        in_specs=[
            pl.BlockSpec((tm, d), lambda i: (i, 0)),
            pl.BlockSpec((d, d), lambda i: (0, 0)),
            pl.BlockSpec((tm, d), lambda i: (i, 0)),
            _mod_spec(2, row_of_tile),
            pl.BlockSpec((1, d), lambda i: (0, 0)),
            _mod_spec(3, row_of_tile),
            _mod_spec(4, row_of_tile),
        ],
        out_specs=[pl.BlockSpec((tm, d), lambda i: (i, 0)), pl.BlockSpec((tm, d), lambda i: (i, 0))],
        out_shape=[jax.ShapeDtypeStruct((m, d), F32), jax.ShapeDtypeStruct((m, d), BF16)],
        compiler_params=_cparams("arbitrary"),
        name="out_proj",
    )(merged, w_out, h, mods, g2.reshape(1, d), mods, mods)


def _ffn_kernel(x_ref, xp_ref, xn_ref, wa_ref, wv_ref, ca_ref, cv_ref, wd_ref, h_ref, gt_ref,
                o_ref, xh_ref, ua_ref, uv_ref, *, tps):
    i = pl.program_id(0)
    j = pl.program_id(1)
    tm = x_ref.shape[0]
    hl = BF16_SUBLANES

    @pl.when(j == 0)
    def _():
        first = (i % tps) == 0
        last = (i % tps) == tps - 1
        xh_ref[0:hl, :] = jnp.where(first, jnp.zeros_like(xp_ref[...]), xp_ref[...])
        xh_ref[hl:hl + tm, :] = x_ref[...]
        xh_ref[hl + tm:2 * hl + tm, :] = jnp.where(last, jnp.zeros_like(xn_ref[...]), xn_ref[...])
        o_ref[...] = jnp.zeros_like(o_ref)

    xh = xh_ref[...]
    ua_ref[...] = jnp.dot(xh, wa_ref[...], preferred_element_type=F32)
    uv_ref[...] = jnp.dot(xh, wv_ref[...], preferred_element_type=F32)
    a = _conv3_rows(ua_ref, ca_ref[...], tm)
    v = _conv3_rows(uv_ref, cv_ref[...], tm)
    act = (a * jax.nn.sigmoid(a) * v).astype(BF16)
    o_ref[...] += jnp.dot(act, wd_ref[...], preferred_element_type=F32)

    @pl.when(j == pl.num_programs(1) - 1)
    def _():
        o_ref[...] = h_ref[...] + gt_ref[...] * o_ref[...]


def _ffn(xm, w_up, conv_w, w_down, h, mods, row_of_tile, tm, seq_len):
    m, d = xm.shape
    tf = FFN_TF
    nj = D_FF // tf
    tps = seq_len // tm
    xp, xn = _halo_specs(tm, d, 0, m)
    row2 = lambda i, j: row_of_tile(i)
    return pl.pallas_call(
        functools.partial(_ffn_kernel, tps=tps),
        grid=(m // tm, nj),
        in_specs=[
            pl.BlockSpec((tm, d), lambda i, j: (i, 0)),
            xp, xn,
            pl.BlockSpec((d, tf), lambda i, j: (0, j)),
            pl.BlockSpec((d, tf), lambda i, j: (0, nj + j)),
            pl.BlockSpec((3, tf), lambda i, j: (0, j)),
            pl.BlockSpec((3, tf), lambda i, j: (0, nj + j)),
            pl.BlockSpec((tf, d), lambda i, j: (j, 0)),
            pl.BlockSpec((tm, d), lambda i, j: (i, 0)),
            _mod_spec(5, row2),
        ],
        out_specs=pl.BlockSpec((tm, d), lambda i, j: (i, 0)),
        out_shape=jax.ShapeDtypeStruct((m, d), F32),
        scratch_shapes=[
            pltpu.VMEM((tm + 2 * BF16_SUBLANES, d), BF16),
            pltpu.VMEM((tm + 2 * BF16_SUBLANES, tf), F32),
            pltpu.VMEM((tm + 2 * BF16_SUBLANES, tf), F32),
        ],
        compiler_params=_cparams("arbitrary", "arbitrary"),
        name="conv_ffn",
    )(xm, xm, xm, w_up, w_up, conv_w, conv_w, w_down, h, mods)


def _rope_tables(length):
    n_freq = HEAD_DIM // 4
    pos = jnp.arange(length)
    row = (pos // GRID_W).astype(F32)
    col = (pos % GRID_W).astype(F32)
    inv = ROPE_THETA ** (-(2.0 * jnp.arange(n_freq, dtype=F32)) / (HEAD_DIM // 2))
    ang = jnp.concatenate([row[:, None] * inv, col[:, None] * inv], axis=1)
    cos = jnp.tile(jnp.cos(ang), (1, 4))
    sin = jnp.tile(jnp.sin(ang), (1, 4))
    sign = jnp.where(jnp.arange(LANES) < LANES // 2, -1.0, 1.0).astype(F32)
    return cos, sin * sign


def _relayout_w_in(w_in):
    depth, d, _ = w_in.shape
    nf = HEAD_DIM // 4

    def heads(a):
        a = a.reshape(depth, d, N_HEADS, 2, 2, 2, nf).transpose(0, 1, 2, 5, 3, 4, 6)
        return a.reshape(depth, d, ATTN_QK_W)

    parts = [w_in[:, :, :Q_OFF], w_in[:, :, V_END:], heads(w_in[:, :, Q_OFF:K_OFF]),
             heads(w_in[:, :, K_OFF:V_OFF]), w_in[:, :, V_OFF:V_END]]
    return jnp.concatenate([a.astype(BF16) for a in parts], axis=2)


def _dft_mats(n):
    r = math.isqrt(n)
    assert r * r == n
    k = jnp.arange(n, dtype=jnp.int32)
    s = jnp.arange(r, dtype=jnp.int32)
    ang_a = ((k[:, None] * s[None, :]) % r).astype(F32) * (2.0 * math.pi / r)
    ang_b = ((k[:, None] * s[None, :]) % n).astype(F32) * (2.0 * math.pi / n)
    ca, sa = jnp.cos(ang_a)[:, :, None], jnp.sin(ang_a)[:, :, None]
    cb, sb = jnp.cos(ang_b)[:, None, :], jnp.sin(ang_b)[:, None, :]
    cos = (ca * cb - sa * sb).reshape(n, n)
    sin = (sa * cb + ca * sb).reshape(n, n)
    return cos.astype(BF16), (-sin).astype(BF16)


def _channel_dft():
    c = np.arange(FOURIER_GROUP_W)
    ang = 2.0 * np.pi * ((c[:, None] * c[None, :]) % FOURIER_GROUP_W) / FOURIER_GROUP_W
    return jnp.asarray(np.concatenate([np.cos(ang), np.sin(ang)], axis=1), dtype=BF16)


def _mixer(h, xm, seq_len, rope_tabs, dft, ctx_kv, w, mods, row_of_tile, lam_init, tm, tq,
           proj_cols=(0, PROJ_TILES)):
    m = h.shape[0]
    b = m // seq_len
    p = _in_proj(xm, w["w_in"], rope_tabs, tm, proj_cols[0], proj_cols[1], seq_len)
    if proj_cols[1] != PROJ_TILES:
        return None, p
    p3 = p.reshape(b, seq_len, N_IN)
    fmix = _fourier(p3, w["cs"], dft[0], dft[1], min(seq_len, 512)).reshape(m, FOURIER_W)
    if ctx_kv is None:
        attn = _attention(p3, P_Q_OFF // LANES, p3, P_K_OFF // LANES, P_V_OFF // LANES, None,
                          w["lambdas"], w["subln_g"], lam_init, tq)
    else:
        c3, kcblk, vcblk = ctx_kv
        attn = _attention(p3, P_Q_OFF // LANES, c3, kcblk, vcblk, p3,
                          w["lambdas"], w["subln_g"], lam_init, tq)
    merged = _merge(p, fmix, attn.reshape(m, ATTN_V_W), w["conv_mix_w"], w["w_br_fourier"],
                    w["w_br_conv"], w["w_br_attn"], tm, seq_len)
    tm_o = min(tm, 256)
    row_o = lambda i: row_of_tile(i // (tm // tm_o))
    h_mid, xm2 = _out_proj(merged, w["w_out"], h, w["g_norm2"], mods, row_o, tm_o)
    h_new = _ffn(xm2, w["w_ffn_up"], w["ffn_conv_w"], w["w_ffn_down"], h_mid, mods, row_of_tile, tm, seq_len)
    return h_new, p


def kernel(x, c, ctx, c_ctx, w_mod, b_mod, g_norm1, g_norm2, w_in, conv_mix_w, lambdas, subln_g,
           w_br_fourier, w_br_conv, w_br_attn, w_out, w_ffn_up, ffn_conv_w, w_ffn_down, g_final):
    b, seq, d = x.shape
    n_ctx = ctx.shape[1]
    depth = w_mod.shape[0]
    assert d == D_MODEL and b + 1 <= MOD_ROWS

    w_in_b = _relayout_w_in(w_in)
    wb = lambda a: a.astype(BF16)
    w_bf, w_bc, w_ba, w_o = wb(w_br_fourier), wb(w_br_conv), wb(w_br_attn), wb(w_out)
    w_up, w_dn = wb(w_ffn_up), wb(w_ffn_down)

    rope_tabs = _rope_tables(seq)
    dft_l = _dft_mats(seq)
    dft_c = _dft_mats(n_ctx)
    cs = _channel_dft()

    cvec = jnp.zeros((MOD_ROWS, d), F32).at[:b].set(c).at[b].set(c_ctx)
    mods_all = _modulation(cvec, w_mod, b_mod).reshape(depth, MOD_ROWS, 6, 1, d)

    tm_l = min(512, seq)
    tm_c = n_ctx
    tq = min(256, seq)
    tiles_per_batch = seq // tm_l
    lat_row = lambda i: i // tiles_per_batch
    ctx_row = lambda i: b

    h = x.reshape(b * seq, d)
    hc = ctx.reshape(b * n_ctx, d)
    for i in range(depth):
        last = i == depth - 1
        lam_init = 0.8 - 0.6 * math.exp(-0.3 * i)
        mods = mods_all[i]
        w = dict(w_in=w_in_b[i], cs=cs, lambdas=lambdas[i], subln_g=subln_g[i], conv_mix_w=conv_mix_w[i],
                 w_br_fourier=w_bf[i], w_br_conv=w_bc[i], w_br_attn=w_ba[i], w_out=w_o[i],
                 g_norm2=g_norm2[i], w_ffn_up=w_up[i], ffn_conv_w=ffn_conv_w[i], w_ffn_down=w_dn[i])

        xc = _norm_modulate(hc, g_norm1[i], mods, ctx_row, tm_c)
        xl = _norm_modulate(h, g_norm1[i], mods, lat_row, tm_l)
        if last:
            _, pc = _mixer(hc, xc, n_ctx, None, dft_c, None, w, mods, ctx_row, lam_init, tm_c, n_ctx,
                           proj_cols=(PROJ_K_TILE, 2))
            ctx_kv = (pc.reshape(b, n_ctx, 2 * PROJ_TN), 0, PROJ_TN // LANES)
            hc_new = hc
        else:
            hc_new, pc = _mixer(hc, xc, n_ctx, None, dft_c, None, w, mods, ctx_row, lam_init, tm_c, n_ctx)
            ctx_kv = (pc.reshape(b, n_ctx, N_IN), P_K_OFF // LANES, P_V_OFF // LANES)
        h, _ = _mixer(h, xl, seq, rope_tabs, dft_l, ctx_kv, w, mods, lat_row, lam_init, tm_l, tq)
        hc = hc_new
    return _final_norm(h, g_final, tm_l).reshape(b, seq, d)
```

```python
import functools
import math

import numpy as np
import jax
import jax.numpy as jnp
from jax import lax
from jax.experimental import pallas as pl
from jax.experimental.pallas import tpu as pltpu

F32 = jnp.float32
BF16 = jnp.bfloat16

D_MODEL = 2048
GRID_W = 64
FOURIER_GROUPS = 4
FOURIER_GROUP_W = D_MODEL // 16
FOURIER_W = FOURIER_GROUPS * FOURIER_GROUP_W
CONV_W = D_MODEL // 4
N_HEADS = 8
HEAD_DIM = D_MODEL // (4 * N_HEADS)
VAL_DIM = 2 * HEAD_DIM
ATTN_QK_W = N_HEADS * 2 * HEAD_DIM
ATTN_V_W = N_HEADS * VAL_DIM
ROPE_THETA = 10000.0
ATTN_SCALE = HEAD_DIM ** -0.5
SUBLN_EPS = 1e-5
Q_OFF = FOURIER_W + 3 * CONV_W
K_OFF = Q_OFF + ATTN_QK_W
V_OFF = K_OFF + ATTN_QK_W
V_END = V_OFF + ATTN_V_W
N_IN = V_END + 3 * D_MODEL
D_FF = ((8 * D_MODEL // 3 + 255) // 256) * 256
EPS = 1e-6

LANES = 128
BF16_SUBLANES = 16
VMEM_LIMIT = 56 * 1024 * 1024
MOD_ROWS = 8

PROJ_TN = 1024
P_GATE_OFF = Q_OFF
P_Q_OFF = P_GATE_OFF + 3 * D_MODEL
P_K_OFF = P_Q_OFF + ATTN_QK_W
P_V_OFF = P_K_OFF + ATTN_QK_W
PROJ_GATE_TILE = P_GATE_OFF // PROJ_TN
PROJ_Q_TILE = P_Q_OFF // PROJ_TN
PROJ_K_TILE = P_K_OFF // PROJ_TN
PROJ_TILES = N_IN // PROJ_TN
FFN_TF = 512


def _cparams(*sem):
    return pltpu.CompilerParams(dimension_semantics=sem, vmem_limit_bytes=VMEM_LIMIT)


def _mod_kernel(c_ref, w_ref, b_ref, o_ref):
    c = c_ref[...]
    s = c * jax.nn.sigmoid(c)
    s_hi = s.astype(BF16)
    s_lo = (s - s_hi.astype(F32)).astype(BF16)
    w = w_ref[...]
    w_hi = w.astype(BF16)
    w_lo = (w - w_hi.astype(F32)).astype(BF16)
    y = jnp.dot(s_hi, w_hi, preferred_element_type=F32)
    y += jnp.dot(s_lo, w_hi, preferred_element_type=F32)
    y += jnp.dot(s_hi, w_lo, preferred_element_type=F32)
    o_ref[...] = y + b_ref[...]


def _modulation(cvec, w_mod, b_mod):
    depth, d, n = w_mod.shape
    tn = 1024
    return pl.pallas_call(
        _mod_kernel,
        grid=(depth, n // tn),
        in_specs=[
            pl.BlockSpec((MOD_ROWS, d), lambda l, j: (0, 0)),
            pl.BlockSpec((None, d, tn), lambda l, j: (l, 0, j)),
            pl.BlockSpec((None, 1, tn), lambda l, j: (l, 0, j)),
        ],
        out_specs=pl.BlockSpec((None, MOD_ROWS, tn), lambda l, j: (l, 0, j)),
        out_shape=jax.ShapeDtypeStruct((depth, MOD_ROWS, n), F32),
        compiler_params=_cparams("arbitrary", "arbitrary"),
        name="modulation",
    )(cvec, w_mod, b_mod.reshape(depth, 1, n))


def _mod_spec(which, row_fn):
    return pl.BlockSpec((None, None, 1, D_MODEL), lambda *ids: (row_fn(*ids), which, 0, 0))


def _rms(x, g):
    return x * lax.rsqrt(jnp.mean(x * x, axis=-1, keepdims=True) + EPS) * g


def _norm_mod_kernel(h_ref, g_ref, sh_ref, sc_ref, o_ref):
    y = _rms(h_ref[...], g_ref[...])
    o_ref[...] = (y * (1.0 + sc_ref[...]) + sh_ref[...]).astype(o_ref.dtype)


def _norm_kernel(h_ref, g_ref, o_ref):
    o_ref[...] = _rms(h_ref[...], g_ref[...]).astype(o_ref.dtype)


def _norm_modulate(h, g, mods, row_of_tile, tm):
    m, d = h.shape
    return pl.pallas_call(
        _norm_mod_kernel,
        grid=(m // tm,),
        in_specs=[
            pl.BlockSpec((tm, d), lambda i: (i, 0)),
            pl.BlockSpec((1, d), lambda i: (0, 0)),
            _mod_spec(0, row_of_tile),
            _mod_spec(1, row_of_tile),
        ],
        out_specs=pl.BlockSpec((tm, d), lambda i: (i, 0)),
        out_shape=jax.ShapeDtypeStruct((m, d), BF16),
        compiler_params=_cparams("arbitrary"),
        name="norm_modulate",
    )(h, g.reshape(1, d), mods, mods)


def _final_norm(h, g, tm):
    m, d = h.shape
    return pl.pallas_call(
        _norm_kernel,
        grid=(m // tm,),
        in_specs=[pl.BlockSpec((tm, d), lambda i: (i, 0)), pl.BlockSpec((1, d), lambda i: (0, 0))],
        out_specs=pl.BlockSpec((tm, d), lambda i: (i, 0)),
        out_shape=jax.ShapeDtypeStruct((m, d), F32),
        compiler_params=_cparams("arbitrary"),
        name="final_norm",
    )(h, g.reshape(1, d))


def _proj_kernel(*refs, col0, rope):
    if rope:
        x_ref, w_ref, cos_ref, sin_ref, o_ref = refs
    else:
        x_ref, w_ref, o_ref = refs
    j = pl.program_id(0) + col0
    acc = jnp.dot(x_ref[...], w_ref[...], preferred_element_type=F32)
    is_q = j == PROJ_Q_TILE
    is_k = j == PROJ_K_TILE
    is_gate = (j >= PROJ_GATE_TILE) & (j < PROJ_Q_TILE)

    @pl.when(is_q | is_k)
    def _():
        qscale = jnp.where(is_q, ATTN_SCALE, 1.0)
        for hd in range(PROJ_TN // LANES):
            a = acc[:, hd * LANES:(hd + 1) * LANES]
            if rope:
                a = a * cos_ref[...] + pltpu.roll(a, LANES // 2, 1) * sin_ref[...]
            o_ref[:, hd * LANES:(hd + 1) * LANES] = (a * qscale).astype(o_ref.dtype)

    @pl.when(is_gate)
    def _():
        o_ref[...] = jax.nn.sigmoid(acc).astype(o_ref.dtype)

    @pl.when(jnp.logical_not(is_q | is_k | is_gate))
    def _():
        o_ref[...] = acc.astype(o_ref.dtype)


def _in_proj(xm, w_in, rope_tabs, tm, col0, ncols, seq_len):
    m, d = xm.shape
    tn = PROJ_TN
    in_specs = [
        pl.BlockSpec((tm, d), lambda j, i: (i, 0)),
        pl.BlockSpec((d, tn), lambda j, i: (0, j + col0)),
    ]
    args = [xm, w_in]
    rope = rope_tabs is not None
    if rope:
        tps = seq_len // tm
        in_specs += [pl.BlockSpec((tm, LANES), lambda j, i: (i % tps, 0))] * 2
        args += list(rope_tabs)
    return pl.pallas_call(
        functools.partial(_proj_kernel, col0=col0, rope=rope),
        grid=(ncols, m // tm),
        in_specs=in_specs,
        out_specs=pl.BlockSpec((tm, tn), lambda j, i: (i, j)),
        out_shape=jax.ShapeDtypeStruct((m, ncols * tn), BF16),
        compiler_params=_cparams("arbitrary", "arbitrary"),
        name="in_proj",
    )(*args)


def _fourier_kernel(f_ref, cs_ref, cl_ref, sl_ref, o_ref, uc_ref, us_ref, *, chunk, scale):
    i = pl.program_id(1)
    seq = f_ref.shape[0]
    gw = FOURIER_GROUP_W

    @pl.when(i == 0)
    def _():
        for r in range(seq // chunk):
            rows = slice(r * chunk, (r + 1) * chunk)
            for g in range(FOURIER_GROUPS):
                cols = slice(g * gw, (g + 1) * gw)
                y = jnp.dot(f_ref[rows, cols], cs_ref[...], preferred_element_type=F32)
                uc_ref[rows, cols] = y[:, :gw].astype(BF16)
                us_ref[rows, cols] = y[:, gw:].astype(BF16)

    y = jnp.dot(cl_ref[...], uc_ref[...], preferred_element_type=F32)
    y += jnp.dot(sl_ref[...], us_ref[...], preferred_element_type=F32)
    o_ref[...] = (y * scale).astype(o_ref.dtype)


def _fourier(p3, cs, cl, sln, tr):
    b, seq, _ = p3.shape
    chunk = min(seq, 1024)
    scale = 1.0 / math.sqrt(seq * FOURIER_GROUP_W)
    return pl.pallas_call(
        functools.partial(_fourier_kernel, chunk=chunk, scale=scale),
        grid=(b, seq // tr),
        in_specs=[
            pl.BlockSpec((None, seq, FOURIER_W), lambda bi, i: (bi, 0, 0)),
            pl.BlockSpec((FOURIER_GROUP_W, 2 * FOURIER_GROUP_W), lambda bi, i: (0, 0)),
            pl.BlockSpec((tr, seq), lambda bi, i: (i, 0)),
            pl.BlockSpec((tr, seq), lambda bi, i: (i, 0)),
        ],
        out_specs=pl.BlockSpec((None, tr, FOURIER_W), lambda bi, i: (bi, i, 0)),
        out_shape=jax.ShapeDtypeStruct((b, seq, FOURIER_W), BF16),
        scratch_shapes=[pltpu.VMEM((seq, FOURIER_W), BF16), pltpu.VMEM((seq, FOURIER_W), BF16)],
        compiler_params=_cparams("arbitrary", "arbitrary"),
        name="fourier",
    )(p3, cs, cl, sln)


def _attn_kernel(*refs, tq, tk, n_lat, lam_init):
    if n_lat:
        lam_ref, q_ref, kc_ref, kl_ref, vc_ref, vl_ref, g_ref, o_ref, sa_ref, ma_ref, sb_ref, mb_ref = refs
    else:
        lam_ref, q_ref, kc_ref, vc_ref, g_ref, o_ref, sa_ref, ma_ref, sb_ref, mb_ref = refs
        kl_ref = vl_ref = None
    t = pl.program_id(0)

    @pl.when(t == 0)
    def _():
        sb_ref[...] = jnp.zeros(sb_ref.shape, F32)
        mb_ref[...] = jnp.zeros(mb_ref.shape, F32)

    @pl.when(t % 2 == 0)
    def _():
        _attn_step(lam_ref, q_ref, kc_ref, kl_ref, vc_ref, vl_ref, g_ref, o_ref, sa_ref, ma_ref, sb_ref, mb_ref,
                   tq=tq, tk=tk, n_lat=n_lat, lam_init=lam_init)

    @pl.when(t % 2 == 1)
    def _():
        _attn_step(lam_ref, q_ref, kc_ref, kl_ref, vc_ref, vl_ref, g_ref, o_ref, sb_ref, mb_ref, sa_ref, ma_ref,
                   tq=tq, tk=tk, n_lat=n_lat, lam_init=lam_init)


def _attn_step(lam_ref, q_ref, kc_ref, kl_ref, vc_ref, vl_ref, g_ref, o_ref, s_new_ref, m_new_ref,
               s_old_ref, m_old_ref, *, tq, tk, n_lat, lam_init):
    q = q_ref[...]
    lane = lax.broadcasted_iota(jnp.int32, (1, LANES), 1)
    comp0 = (lane % (LANES // 2)) < (LANES // 4)
    zero = jnp.zeros_like(q)
    qq = jnp.concatenate([jnp.where(comp0, q, zero), jnp.where(comp0, zero, q)], axis=0)
    n_ctx = kc_ref.shape[0]
    chunks = [(kc_ref, vc_ref, 0, n_ctx, 0)]
    chunks += [(kl_ref, vl_ref, c * tk, tk, n_ctx + c * tk) for c in range(n_lat)]

    def lane_fold(x, op):
        part = x[:, 0:LANES]
        for c in range(1, x.shape[1] // LANES):
            part = op(part, x[:, c * LANES:(c + 1) * LANES])
        return part

    m_prev = m_old_ref[...]
    mx = None
    lsum = jnp.zeros((2 * tq, LANES), F32)
    acc = jnp.zeros((2 * tq, VAL_DIM), F32)
    for kr, vr, r0, sz, c0 in chunks:
        s = lax.dot_general(qq, kr[r0:r0 + sz, :], (((1,), (1,)), ((), ())), preferred_element_type=F32)
        s_new_ref[:, c0:c0 + sz] = s
        part = lane_fold(s, jnp.maximum)
        mx = part if mx is None else jnp.maximum(mx, part)
        p = jnp.exp(s_old_ref[:, c0:c0 + sz] - m_prev)
        lsum += lane_fold(p, jnp.add)
        acc += jnp.dot(p.astype(BF16), vr[r0:r0 + sz, :], preferred_element_type=F32)
    m_new_ref[...] = jnp.max(mx, axis=-1, keepdims=True)

    o = acc / jnp.sum(lsum, axis=-1, keepdims=True)
    lf = lam_ref[...]
    lam = (jnp.exp(jnp.sum(lf[0:1] * lf[1:2], axis=-1, keepdims=True))
           - jnp.exp(jnp.sum(lf[2:3] * lf[3:4], axis=-1, keepdims=True)) + lam_init)
    o = o[:tq] - lam * o[tq:]
    o = o * lax.rsqrt(jnp.mean(o * o, axis=-1, keepdims=True) + SUBLN_EPS)
    o_ref[...] = (o * g_ref[...] * (1.0 - lam_init)).astype(o_ref.dtype)


def _attention(q3, qblk, c3, kcblk, vcblk, l3, lam, subln_g, lam_init, tq):
    b, lq, _ = q3.shape
    n_ctx = c3.shape[1]
    n_keys = n_ctx + (l3.shape[1] if l3 is not None else 0)
    tk = 512
    nq = lq // tq
    n_tiles = b * N_HEADS * nq

    def tile(t):
        return t // (N_HEADS * nq), (t // nq) % N_HEADS, t % nq

    enter = lambda t: tile(jnp.minimum(t, n_tiles - 1))
    leave = lambda t: tile(jnp.maximum(t - 1, 0))

    def spec(rows, which, colblk, per_tile):
        def index(t):
            bi, h, i = which(t)
            return (bi, i if per_tile else 0, colblk + h)
        return pl.BlockSpec((None, rows, LANES), index)

    in_specs = [pl.BlockSpec((4, HEAD_DIM), lambda t: (0, 0)), spec(tq, enter, qblk, True),
                spec(n_ctx, enter, kcblk, False)]
    args = [lam, q3, c3]
    n_lat = 0
    if l3 is not None:
        seq = l3.shape[1]
        n_lat = seq // tk
        in_specs.append(spec(seq, enter, P_K_OFF // LANES, False))
        args.append(l3)
    in_specs.append(spec(n_ctx, leave, vcblk, False))
    args.append(c3)
    if l3 is not None:
        in_specs.append(spec(seq, leave, P_V_OFF // LANES, False))
        args.append(l3)
    in_specs.append(pl.BlockSpec((1, VAL_DIM), lambda t: (0, 0)))
    args.append(subln_g.reshape(1, VAL_DIM))
    return pl.pallas_call(
        functools.partial(_attn_kernel, tq=tq, tk=tk, n_lat=n_lat, lam_init=lam_init),
        grid=(n_tiles + 1,),
        in_specs=in_specs,
        out_specs=spec(tq, leave, 0, True),
        out_shape=jax.ShapeDtypeStruct((b, lq, ATTN_V_W), BF16),
        scratch_shapes=[pltpu.VMEM((2 * tq, n_keys), F32), pltpu.VMEM((2 * tq, 1), F32)] * 2,
        compiler_params=_cparams("arbitrary"),
        name="diff_attention",
    )(*args)


def _conv3_rows(buf_ref, w, tm):
    h = BF16_SUBLANES
    return (buf_ref[h - 1:h - 1 + tm, :] * w[0:1] + buf_ref[h:h + tm, :] * w[1:2]
            + buf_ref[h + 1:h + 1 + tm, :] * w[2:3])


def _halo_specs(tm, width, colblk, m):
    per = tm // BF16_SUBLANES
    last = m // BF16_SUBLANES - 1
    prev = pl.BlockSpec((BF16_SUBLANES, width), lambda i, *_: (jnp.maximum(i * per - 1, 0), colblk))
    nxt = pl.BlockSpec((BF16_SUBLANES, width), lambda i, *_: (jnp.minimum((i + 1) * per, last), colblk))
    return prev, nxt


def _merge_kernel(f_ref, cb_ref, cc_ref, cx_ref, ccp_ref, cxp_ref, ccn_ref, cxn_ref, a_ref,
                  gf_ref, gc_ref, ga_ref, cw_ref, wbf_ref, wbc_ref, wba_ref, o_ref, z_ref, *, tps, nchunk):
    i = pl.program_id(0)
    tm = cc_ref.shape[0]
    h = BF16_SUBLANES
    first = (i % tps) == 0
    last = (i % tps) == tps - 1
    zp = ccp_ref[...].astype(F32) * cxp_ref[...].astype(F32)
    zn = ccn_ref[...].astype(F32) * cxn_ref[...].astype(F32)
    z_ref[0:h, :] = jnp.where(first, 0.0, zp)
    z_ref[h:h + tm, :] = cc_ref[...].astype(F32) * cx_ref[...].astype(F32)
    z_ref[h + tm:2 * h + tm, :] = jnp.where(last, 0.0, zn)
    cv = (cb_ref[...].astype(F32) * _conv3_rows(z_ref, cw_ref[...], tm)).astype(BF16)
    f = f_ref[...]
    a = a_ref[...]
    cn = D_MODEL // nchunk
    for n in range(nchunk):
        cols = slice(n * cn, (n + 1) * cn)
        y = gf_ref[:, cols].astype(F32) * jnp.dot(f, wbf_ref[:, cols], preferred_element_type=F32)
        y += gc_ref[:, cols].astype(F32) * jnp.dot(cv, wbc_ref[:, cols], preferred_element_type=F32)
        y += ga_ref[:, cols].astype(F32) * jnp.dot(a, wba_ref[:, cols], preferred_element_type=F32)
        o_ref[:, cols] = y.astype(o_ref.dtype)


def _merge(p, fmix, attn, conv_w, w_bf, w_bc, w_ba, tm, seq_len):
    m = p.shape[0]
    d = D_MODEL
    cw = CONV_W
    tps = seq_len // tm
    blk = lambda width, c: pl.BlockSpec((tm, width), lambda i: (i, c))
    ccp, ccn = _halo_specs(tm, cw, 2, m)
    cxp, cxn = _halo_specs(tm, cw, 3, m)
    gate0 = P_GATE_OFF // d
    full = lambda r, c: pl.BlockSpec((r, c), lambda i: (0, 0))
    return pl.pallas_call(
        functools.partial(_merge_kernel, tps=tps, nchunk=4),
        grid=(m // tm,),
        in_specs=[
            blk(FOURIER_W, 0),
            blk(cw, 1), blk(cw, 2), blk(cw, 3),
            ccp, cxp, ccn, cxn,
            blk(ATTN_V_W, 0),
            pl.BlockSpec((tm, d), lambda i: (i, gate0)),
            pl.BlockSpec((tm, d), lambda i: (i, gate0 + 1)),
            pl.BlockSpec((tm, d), lambda i: (i, gate0 + 2)),
            full(3, cw), full(FOURIER_W, d), full(cw, d), full(ATTN_V_W, d),
        ],
        out_specs=pl.BlockSpec((tm, d), lambda i: (i, 0)),
        out_shape=jax.ShapeDtypeStruct((m, d), BF16),
        scratch_shapes=[pltpu.VMEM((tm + 2 * BF16_SUBLANES, cw), F32)],
        compiler_params=_cparams("arbitrary"),
        name="branch_merge",
    )(fmix, p, p, p, p, p, p, p, attn, p, p, p, conv_w, w_bf, w_bc, w_ba)


def _outproj_kernel(m_ref, w_ref, h_ref, gt_ref, g_ref, sh_ref, sc_ref, ho_ref, xo_ref):
    y = jnp.dot(m_ref[...], w_ref[...], preferred_element_type=F32)
    h = h_ref[...] + gt_ref[...] * y
    ho_ref[...] = h
    xo_ref[...] = (_rms(h, g_ref[...]) * (1.0 + sc_ref[...]) + sh_ref[...]).astype(xo_ref.dtype)


def _out_proj(merged, w_out, h, g2, mods, row_of_tile, tm):
    m, d = h.shape
    return pl.pallas_call(
        _outproj_kernel,
        grid=(m // tm,),
        in_specs=[
            pl.BlockSpec((tm, d), lambda i: (i, 0)),
            pl.BlockSpec((d, d), lambda i: (0, 0)),
            pl.BlockSpec((tm, d), lambda i: (i, 0)),
            _mod_spec(2, row_of_tile),
            pl.BlockSpec((1, d), lambda i: (0, 0)),
            _mod_spec(3, row_of_tile),
            _mod_spec(4, row_of_tile),
        ],
        out_specs=[pl.BlockSpec((tm, d), lambda i: (i, 0)), pl.BlockSpec((tm, d), lambda i: (i, 0))],
        out_shape=[jax.ShapeDtypeStruct((m, d), F32), jax.ShapeDtypeStruct((m, d), BF16)],
        compiler_params=_cparams("arbitrary"),
        name="out_proj",
    )(merged, w_out, h, mods, g2.reshape(1, d), mods, mods)


def _ffn_kernel(x_ref, xp_ref, xn_ref, wa_ref, wv_ref, ca_ref, cv_ref, wd_ref, h_ref, gt_ref,
                o_ref, xh_ref, ua_ref, uv_ref, *, tps):
    i = pl.program_id(0)
    j = pl.program_id(1)
    tm = x_ref.shape[0]
    hl = BF16_SUBLANES

    @pl.when(j == 0)
    def _():
        first = (i % tps) == 0
        last = (i % tps) == tps - 1
        xh_ref[0:hl, :] = jnp.where(first, jnp.zeros_like(xp_ref[...]), xp_ref[...])
        xh_ref[hl:hl + tm, :] = x_ref[...]
        xh_ref[hl + tm:2 * hl + tm, :] = jnp.where(last, jnp.zeros_like(xn_ref[...]), xn_ref[...])
        o_ref[...] = jnp.zeros_like(o_ref)

    xh = xh_ref[...]
    ua_ref[...] = jnp.dot(xh, wa_ref[...], preferred_element_type=F32)
    uv_ref[...] = jnp.dot(xh, wv_ref[...], preferred_element_type=F32)
    a = _conv3_rows(ua_ref, ca_ref[...], tm)
    v = _conv3_rows(uv_ref, cv_ref[...], tm)
    act = (a * jax.nn.sigmoid(a) * v).astype(BF16)
    o_ref[...] += jnp.dot(act, wd_ref[...], preferred_element_type=F32)

    @pl.when(j == pl.num_programs(1) - 1)
    def _():
        o_ref[...] = h_ref[...] + gt_ref[...] * o_ref[...]


def _ffn(xm, w_up, conv_w, w_down, h, mods, row_of_tile, tm, seq_len):
    m, d = xm.shape
    tf = FFN_TF
    nj = D_FF // tf
    tps = seq_len // tm
    xp, xn = _halo_specs(tm, d, 0, m)
    row2 = lambda i, j: row_of_tile(i)
    return pl.pallas_call(
        functools.partial(_ffn_kernel, tps=tps),
        grid=(m // tm, nj),
        in_specs=[
            pl.BlockSpec((tm, d), lambda i, j: (i, 0)),
            xp, xn,
            pl.BlockSpec((d, tf), lambda i, j: (0, j)),
            pl.BlockSpec((d, tf), lambda i, j: (0, nj + j)),
            pl.BlockSpec((3, tf), lambda i, j: (0, j)),
            pl.BlockSpec((3, tf), lambda i, j: (0, nj + j)),
            pl.BlockSpec((tf, d), lambda i, j: (j, 0)),
            pl.BlockSpec((tm, d), lambda i, j: (i, 0)),
            _mod_spec(5, row2),
        ],
        out_specs=pl.BlockSpec((tm, d), lambda i, j: (i, 0)),
        out_shape=jax.ShapeDtypeStruct((m, d), F32),
        scratch_shapes=[
            pltpu.VMEM((tm + 2 * BF16_SUBLANES, d), BF16),
            pltpu.VMEM((tm + 2 * BF16_SUBLANES, tf), F32),
            pltpu.VMEM((tm + 2 * BF16_SUBLANES, tf), F32),
        ],
        compiler_params=_cparams("arbitrary", "arbitrary"),
        name="conv_ffn",
    )(xm, xm, xm, w_up, w_up, conv_w, conv_w, w_down, h, mods)


def _rope_tables(length):
    n_freq = HEAD_DIM // 4
    pos = jnp.arange(length)
    row = (pos // GRID_W).astype(F32)
    col = (pos % GRID_W).astype(F32)
    inv = ROPE_THETA ** (-(2.0 * jnp.arange(n_freq, dtype=F32)) / (HEAD_DIM // 2))
    ang = jnp.concatenate([row[:, None] * inv, col[:, None] * inv], axis=1)
    cos = jnp.tile(jnp.cos(ang), (1, 4))
    sin = jnp.tile(jnp.sin(ang), (1, 4))
    sign = jnp.where(jnp.arange(LANES) < LANES // 2, -1.0, 1.0).astype(F32)
    return cos, sin * sign


def _relayout_w_in(w_in):
    depth, d, _ = w_in.shape
    nf = HEAD_DIM // 4

    def heads(a):
        a = a.reshape(depth, d, N_HEADS, 2, 2, 2, nf).transpose(0, 1, 2, 5, 3, 4, 6)
        return a.reshape(depth, d, ATTN_QK_W)

    parts = [w_in[:, :, :Q_OFF], w_in[:, :, V_END:], heads(w_in[:, :, Q_OFF:K_OFF]),
             heads(w_in[:, :, K_OFF:V_OFF]), w_in[:, :, V_OFF:V_END]]
    return jnp.concatenate([a.astype(BF16) for a in parts], axis=2)


def _dft_mats(n):
    r = math.isqrt(n)
    assert r * r == n
    k = jnp.arange(n, dtype=jnp.int32)
    s = jnp.arange(r, dtype=jnp.int32)
    ang_a = ((k[:, None] * s[None, :]) % r).astype(F32) * (2.0 * math.pi / r)
    ang_b = ((k[:, None] * s[None, :]) % n).astype(F32) * (2.0 * math.pi / n)
    ca, sa = jnp.cos(ang_a)[:, :, None], jnp.sin(ang_a)[:, :, None]
    cb, sb = jnp.cos(ang_b)[:, None, :], jnp.sin(ang_b)[:, None, :]
    cos = (ca * cb - sa * sb).reshape(n, n)
    sin = (sa * cb + ca * sb).reshape(n, n)
    return cos.astype(BF16), (-sin).astype(BF16)


def _channel_dft():
    c = np.arange(FOURIER_GROUP_W)
    ang = 2.0 * np.pi * ((c[:, None] * c[None, :]) % FOURIER_GROUP_W) / FOURIER_GROUP_W
    return jnp.asarray(np.concatenate([np.cos(ang), np.sin(ang)], axis=1), dtype=BF16)


def _mixer(h, xm, seq_len, rope_tabs, dft, ctx_kv, w, mods, row_of_tile, lam_init, tm, tq,
           proj_cols=(0, PROJ_TILES)):
    m = h.shape[0]
    b = m // seq_len
    p = _in_proj(xm, w["w_in"], rope_tabs, tm, proj_cols[0], proj_cols[1], seq_len)
    if proj_cols[1] != PROJ_TILES:
        return None, p
    p3 = p.reshape(b, seq_len, N_IN)
    fmix = _fourier(p3, w["cs"], dft[0], dft[1], min(seq_len, 512)).reshape(m, FOURIER_W)
    if ctx_kv is None:
        attn = _attention(p3, P_Q_OFF // LANES, p3, P_K_OFF // LANES, P_V_OFF // LANES, None,
                          w["lambdas"], w["subln_g"], lam_init, tq)
    else:
        c3, kcblk, vcblk = ctx_kv
        attn = _attention(p3, P_Q_OFF // LANES, c3, kcblk, vcblk, p3,
                          w["lambdas"], w["subln_g"], lam_init, tq)
    merged = _merge(p, fmix, attn.reshape(m, ATTN_V_W), w["conv_mix_w"], w["w_br_fourier"],
                    w["w_br_conv"], w["w_br_attn"], tm, seq_len)
    tm_o = min(tm, 256)
    row_o = lambda i: row_of_tile(i // (tm // tm_o))
    h_mid, xm2 = _out_proj(merged, w["w_out"], h, w["g_norm2"], mods, row_o, tm_o)
    h_new = _ffn(xm2, w["w_ffn_up"], w["ffn_conv_w"], w["w_ffn_down"], h_mid, mods, row_of_tile, tm, seq_len)
    return h_new, p


def kernel(x, c, ctx, c_ctx, w_mod, b_mod, g_norm1, g_norm2, w_in, conv_mix_w, lambdas, subln_g,
           w_br_fourier, w_br_conv, w_br_attn, w_out, w_ffn_up, ffn_conv_w, w_ffn_down, g_final):
    b, seq, d = x.shape
    n_ctx = ctx.shape[1]
    depth = w_mod.shape[0]
    assert d == D_MODEL and b + 1 <= MOD_ROWS

    w_in_b = _relayout_w_in(w_in)
    wb = lambda a: a.astype(BF16)
    w_bf, w_bc, w_ba, w_o = wb(w_br_fourier), wb(w_br_conv), wb(w_br_attn), wb(w_out)
    w_up, w_dn = wb(w_ffn_up), wb(w_ffn_down)

    rope_tabs = _rope_tables(seq)
    dft_l = _dft_mats(seq)
    dft_c = _dft_mats(n_ctx)
    cs = _channel_dft()

    cvec = jnp.zeros((MOD_ROWS, d), F32).at[:b].set(c).at[b].set(c_ctx)
    mods_all = _modulation(cvec, w_mod, b_mod).reshape(depth, MOD_ROWS, 6, 1, d)

    tm_l = min(512, seq)
    tm_c = n_ctx
    tq = min(256, seq)
    tiles_per_batch = seq // tm_l
    lat_row = lambda i: i // tiles_per_batch
    ctx_row = lambda i: b

    h = x.reshape(b * seq, d)
    hc = ctx.reshape(b * n_ctx, d)
    for i in range(depth):
        last = i == depth - 1
        lam_init = 0.8 - 0.6 * math.exp(-0.3 * i)
        mods = mods_all[i]
        w = dict(w_in=w_in_b[i], cs=cs, lambdas=lambdas[i], subln_g=subln_g[i], conv_mix_w=conv_mix_w[i],
                 w_br_fourier=w_bf[i], w_br_conv=w_bc[i], w_br_attn=w_ba[i], w_out=w_o[i],
                 g_norm2=g_norm2[i], w_ffn_up=w_up[i], ffn_conv_w=ffn_conv_w[i], w_ffn_down=w_dn[i])

        xc = _norm_modulate(hc, g_norm1[i], mods, ctx_row, tm_c)
        xl = _norm_modulate(h, g_norm1[i], mods, lat_row, tm_l)
        if last:
            _, pc = _mixer(hc, xc, n_ctx, None, dft_c, None, w, mods, ctx_row, lam_init, tm_c, n_ctx,
                           proj_cols=(PROJ_K_TILE, 2))
            ctx_kv = (pc.reshape(b, n_ctx, 2 * PROJ_TN), 0, PROJ_TN // LANES)
            hc_new = hc
        else:
            hc_new, pc = _mixer(hc, xc, n_ctx, None, dft_c, None, w, mods, ctx_row, lam_init, tm_c, n_ctx)
            ctx_kv = (pc.reshape(b, n_ctx, N_IN), P_K_OFF // LANES, P_V_OFF // LANES)
        h, _ = _mixer(h, xl, seq, rope_tabs, dft_l, ctx_kv, w, mods, lat_row, lam_init, tm_l, tq)
        hc = hc_new
    return _final_norm(h, g_final, tm_l).reshape(b, seq, d)
```

```python
import functools
import math

import numpy as np
import jax
import jax.numpy as jnp
from jax import lax
from jax.experimental import pallas as pl
from jax.experimental.pallas import tpu as pltpu

F32 = jnp.float32
BF16 = jnp.bfloat16

D_MODEL = 2048
GRID_W = 64
FOURIER_GROUPS = 4
FOURIER_GROUP_W = D_MODEL // 16
FOURIER_W = FOURIER_GROUPS * FOURIER_GROUP_W
CONV_W = D_MODEL // 4
N_HEADS = 8
HEAD_DIM = D_MODEL // (4 * N_HEADS)
VAL_DIM = 2 * HEAD_DIM
ATTN_QK_W = N_HEADS * 2 * HEAD_DIM
ATTN_V_W = N_HEADS * VAL_DIM
ROPE_THETA = 10000.0
ATTN_SCALE = HEAD_DIM ** -0.5
SUBLN_EPS = 1e-5
Q_OFF = FOURIER_W + 3 * CONV_W
K_OFF = Q_OFF + ATTN_QK_W
V_OFF = K_OFF + ATTN_QK_W
V_END = V_OFF + ATTN_V_W
N_IN = V_END + 3 * D_MODEL
D_FF = ((8 * D_MODEL // 3 + 255) // 256) * 256
EPS = 1e-6

LANES = 128
MXU_COLS = 256
BF16_SUBLANES = 16
VMEM_LIMIT = 56 * 1024 * 1024
MOD_ROWS = 8

PROJ_TN = 1024
P_GATE_OFF = Q_OFF
P_Q_OFF = P_GATE_OFF + 3 * D_MODEL
P_K_OFF = P_Q_OFF + ATTN_QK_W
P_V_OFF = P_K_OFF + ATTN_QK_W
PROJ_GATE_TILE = P_GATE_OFF // PROJ_TN
PROJ_Q_TILE = P_Q_OFF // PROJ_TN
PROJ_K_TILE = P_K_OFF // PROJ_TN
PROJ_TILES = N_IN // PROJ_TN
FFN_TF = 512
FFN_UP_TM = 1024
FFN_DOWN_TN = 512
FFN_DOWN_TM = 512


def _cparams(*sem):
    return pltpu.CompilerParams(dimension_semantics=sem, vmem_limit_bytes=VMEM_LIMIT)


def _mod_kernel(c_ref, w_ref, b_ref, o_ref):
    c = c_ref[...]
    s = c * jax.nn.sigmoid(c)
    s_hi = s.astype(BF16)
    s_lo = (s - s_hi.astype(F32)).astype(BF16)
    w = w_ref[...]
    w_hi = w.astype(BF16)
    w_lo = (w - w_hi.astype(F32)).astype(BF16)
    y = jnp.dot(s_hi, w_hi, preferred_element_type=F32)
    y += jnp.dot(s_lo, w_hi, preferred_element_type=F32)
    y += jnp.dot(s_hi, w_lo, preferred_element_type=F32)
    o_ref[...] = y + b_ref[...]


def _modulation(cvec, w_mod, b_mod):
    depth, d, n = w_mod.shape
    tn = 1024
    return pl.pallas_call(
        _mod_kernel,
        grid=(depth, n // tn),
        in_specs=[
            pl.BlockSpec((MOD_ROWS, d), lambda l, j: (0, 0)),
            pl.BlockSpec((None, d, tn), lambda l, j: (l, 0, j)),
            pl.BlockSpec((None, 1, tn), lambda l, j: (l, 0, j)),
        ],
        out_specs=pl.BlockSpec((None, MOD_ROWS, tn), lambda l, j: (l, 0, j)),
        out_shape=jax.ShapeDtypeStruct((depth, MOD_ROWS, n), F32),
        compiler_params=_cparams("arbitrary", "arbitrary"),
        name="modulation",
    )(cvec, w_mod, b_mod.reshape(depth, 1, n))


def _mod_spec(which, mod_row, tm):
    return pl.BlockSpec((None, None, 1, D_MODEL), lambda i, *_: (mod_row(i * tm), which, 0, 0))


def _rms(x, g):
    return x * lax.rsqrt(jnp.mean(x * x, axis=-1, keepdims=True) + EPS) * g


def _norm_mod_kernel(h_ref, g_ref, sh_ref, sc_ref, o_ref):
    y = _rms(h_ref[...], g_ref[...])
    o_ref[...] = (y * (1.0 + sc_ref[...]) + sh_ref[...]).astype(o_ref.dtype)


def _norm_kernel(h_ref, g_ref, o_ref):
    o_ref[...] = _rms(h_ref[...], g_ref[...]).astype(o_ref.dtype)


def _norm_modulate(h, g, mods, mod_row, tm):
    m, d = h.shape
    return pl.pallas_call(
        _norm_mod_kernel,
        grid=(m // tm,),
        in_specs=[
            pl.BlockSpec((tm, d), lambda i: (i, 0)),
            pl.BlockSpec((1, d), lambda i: (0, 0)),
            _mod_spec(0, mod_row, tm),
            _mod_spec(1, mod_row, tm),
        ],
        out_specs=pl.BlockSpec((tm, d), lambda i: (i, 0)),
        out_shape=jax.ShapeDtypeStruct((m, d), BF16),
        compiler_params=_cparams("arbitrary"),
        name="norm_modulate",
    )(h, g.reshape(1, d), mods, mods)


def _final_norm(h, g, tm):
    m, d = h.shape
    return pl.pallas_call(
        _norm_kernel,
        grid=(m // tm,),
        in_specs=[pl.BlockSpec((tm, d), lambda i: (i, 0)), pl.BlockSpec((1, d), lambda i: (0, 0))],
        out_specs=pl.BlockSpec((tm, d), lambda i: (i, 0)),
        out_shape=jax.ShapeDtypeStruct((m, d), F32),
        compiler_params=_cparams("arbitrary"),
        name="final_norm",
    )(h, g.reshape(1, d))


def _proj_kernel(*refs, col0, rope):
    if rope:
        x_ref, w_ref, cos_ref, sin_up_ref, sin_dn_ref, o_ref, wb_ref = refs
    else:
        x_ref, w_ref, o_ref, wb_ref = refs
    j = pl.program_id(0) + col0

    @pl.when(pl.program_id(1) == 0)
    def _():
        wb_ref[...] = w_ref[...].astype(BF16)

    is_q = j == PROJ_Q_TILE
    is_k = j == PROJ_K_TILE
    is_gate = (j >= PROJ_GATE_TILE) & (j < PROJ_Q_TILE)
    x = x_ref[...]

    def column_chunks(epilogue):
        for c in range(PROJ_TN // MXU_COLS):
            cols = slice(c * MXU_COLS, (c + 1) * MXU_COLS)
            acc = jnp.dot(x, wb_ref[:, cols], preferred_element_type=F32)
            o_ref[:, cols] = epilogue(acc).astype(o_ref.dtype)

    def rotary(acc):
        qscale = jnp.where(is_q, ATTN_SCALE, 1.0)
        half = HEAD_DIM // 4
        out = []
        for hd in range(MXU_COLS // LANES):
            a = acc[:, hd * LANES:(hd + 1) * LANES]
            if rope:
                a = (a * cos_ref[...] + pltpu.roll(a, LANES - half, 1) * sin_up_ref[...]
                     + pltpu.roll(a, half, 1) * sin_dn_ref[...])
            out.append(a * qscale)
        return jnp.concatenate(out, axis=1)

    pl.when(is_q | is_k)(lambda: column_chunks(rotary))
    pl.when(is_gate)(lambda: column_chunks(jax.nn.sigmoid))
    pl.when(jnp.logical_not(is_q | is_k | is_gate))(lambda: column_chunks(lambda acc: acc))


def _in_proj(xm, w_in, layer, rope_tabs, tm, col0, ncols, seq_len):
    m, d = xm.shape
    tn = PROJ_TN
    n_plain = PROJ_GATE_TILE
    n_gate = PROJ_Q_TILE - PROJ_GATE_TILE
    n_qkv = PROJ_TILES - PROJ_Q_TILE

    def w_tile(j):
        j = j + col0
        return jnp.where(j < n_plain, j, jnp.where(j < n_plain + n_gate, j + n_qkv, j - n_gate))

    in_specs = [
        pl.BlockSpec((tm, d), lambda j, i: (i, 0)),
        pl.BlockSpec((None, d, tn), lambda j, i: (layer, 0, w_tile(j))),
    ]
    args = [xm, w_in]
    rope = rope_tabs is not None
    if rope:
        tps = seq_len // tm
        in_specs += [pl.BlockSpec((tm, LANES), lambda j, i: (i % tps, 0))] * 3
        args += list(rope_tabs)
    return pl.pallas_call(
        functools.partial(_proj_kernel, col0=col0, rope=rope),
        grid=(ncols, m // tm),
        in_specs=in_specs,
        out_specs=pl.BlockSpec((tm, tn), lambda j, i: (i, j)),
        out_shape=jax.ShapeDtypeStruct((m, ncols * tn), BF16),
        scratch_shapes=[pltpu.VMEM((d, tn), BF16)],
        compiler_params=_cparams("arbitrary", "arbitrary"),
        name="in_proj",
    )(*args)


def _fourier_kernel(f_ref, cs_ref, cl_ref, sl_ref, o_ref, uc_ref, us_ref, *, chunk, scale):
    i = pl.program_id(1)
    seq = f_ref.shape[0]
    gw = FOURIER_GROUP_W

    @pl.when(i == 0)
    def _():
        for r in range(seq // chunk):
            rows = slice(r * chunk, (r + 1) * chunk)
            for g in range(FOURIER_GROUPS):
                cols = slice(g * gw, (g + 1) * gw)
                y = jnp.dot(f_ref[rows, cols], cs_ref[...], preferred_element_type=F32)
                uc_ref[rows, cols] = y[:, :gw].astype(BF16)
                us_ref[rows, cols] = y[:, gw:].astype(BF16)

    y = jnp.dot(cl_ref[...], uc_ref[...], preferred_element_type=F32)
    y += jnp.dot(sl_ref[...], us_ref[...], preferred_element_type=F32)
    o_ref[...] = (y * scale).astype(o_ref.dtype)


def _fourier(p3, cs, cl, sln, tr):
    b, seq, _ = p3.shape
    chunk = min(seq, 1024)
    scale = 1.0 / math.sqrt(seq * FOURIER_GROUP_W)
    return pl.pallas_call(
        functools.partial(_fourier_kernel, chunk=chunk, scale=scale),
        grid=(b, seq // tr),
        in_specs=[
            pl.BlockSpec((None, seq, FOURIER_W), lambda bi, i: (bi, 0, 0)),
            pl.BlockSpec((FOURIER_GROUP_W, 2 * FOURIER_GROUP_W), lambda bi, i: (0, 0)),
            pl.BlockSpec((tr, seq), lambda bi, i: (i, 0)),
            pl.BlockSpec((tr, seq), lambda bi, i: (i, 0)),
        ],
        out_specs=pl.BlockSpec((None, tr, FOURIER_W), lambda bi, i: (bi, i, 0)),
        out_shape=jax.ShapeDtypeStruct((b, seq, FOURIER_W), BF16),
        scratch_shapes=[pltpu.VMEM((seq, FOURIER_W), BF16), pltpu.VMEM((seq, FOURIER_W), BF16)],
        compiler_params=_cparams("arbitrary", "arbitrary"),
        name="fourier",
    )(p3, cs, cl, sln)


def _attn_kernel(*refs, tq, tk, n_lat, lam_init):
    if n_lat:
        lam_ref, q_ref, kc_ref, kl_ref, vc_ref, vl_ref, g_ref, o_ref, sa_ref, ma_ref, sb_ref, mb_ref = refs
    else:
        lam_ref, q_ref, kc_ref, vc_ref, g_ref, o_ref, sa_ref, ma_ref, sb_ref, mb_ref = refs
        kl_ref = vl_ref = None
    t = pl.program_id(0)

    @pl.when(t == 0)
    def _():
        sb_ref[...] = jnp.zeros(sb_ref.shape, F32)
        mb_ref[...] = jnp.zeros(mb_ref.shape, F32)

    @pl.when(t % 2 == 0)
    def _():
        _attn_step(lam_ref, q_ref, kc_ref, kl_ref, vc_ref, vl_ref, g_ref, o_ref, sa_ref, ma_ref, sb_ref, mb_ref,
                   tq=tq, tk=tk, n_lat=n_lat, lam_init=lam_init)

    @pl.when(t % 2 == 1)
    def _():
        _attn_step(lam_ref, q_ref, kc_ref, kl_ref, vc_ref, vl_ref, g_ref, o_ref, sb_ref, mb_ref, sa_ref, ma_ref,
                   tq=tq, tk=tk, n_lat=n_lat, lam_init=lam_init)


def _attn_step(lam_ref, q_ref, kc_ref, kl_ref, vc_ref, vl_ref, g_ref, o_ref, s_new_ref, m_new_ref,
               s_old_ref, m_old_ref, *, tq, tk, n_lat, lam_init):
    q = q_ref[...]
    lane = lax.broadcasted_iota(jnp.int32, (1, LANES), 1)
    comp0 = lane < HEAD_DIM
    zero = jnp.zeros_like(q)
    qq = jnp.concatenate([jnp.where(comp0, q, zero), jnp.where(comp0, zero, q)], axis=0)
    n_ctx = kc_ref.shape[0]
    chunks = [(kc_ref, vc_ref, 0, n_ctx, 0)]
    chunks += [(kl_ref, vl_ref, c * tk, tk, n_ctx + c * tk) for c in range(n_lat)]

    def lane_fold(x, op):
        part = x[:, 0:LANES]
        for c in range(1, x.shape[1] // LANES):
            part = op(part, x[:, c * LANES:(c + 1) * LANES])
        return part

    m_prev = m_old_ref[...]
    mx = None
    lsum = jnp.zeros((2 * tq, LANES), F32)
    acc = jnp.zeros((2 * tq, VAL_DIM), F32)
    for kr, vr, r0, sz, c0 in chunks:
        s = lax.dot_general(qq, kr[r0:r0 + sz, :], (((1,), (1,)), ((), ())), preferred_element_type=F32)
        s_new_ref[:, c0:c0 + sz] = s
        part = lane_fold(s, jnp.maximum)
        mx = part if mx is None else jnp.maximum(mx, part)
        p = jnp.exp(s_old_ref[:, c0:c0 + sz] - m_prev)
        lsum += lane_fold(p, jnp.add)
        acc += jnp.dot(p.astype(BF16), vr[r0:r0 + sz, :], preferred_element_type=F32)
    m_new_ref[...] = jnp.max(mx, axis=-1, keepdims=True)

    o = acc / jnp.sum(lsum, axis=-1, keepdims=True)
    lf = lam_ref[...]
    lam = (jnp.exp(jnp.sum(lf[0:1] * lf[1:2], axis=-1, keepdims=True))
           - jnp.exp(jnp.sum(lf[2:3] * lf[3:4], axis=-1, keepdims=True)) + lam_init)
    o = o[:tq] - lam * o[tq:]
    o = o * lax.rsqrt(jnp.mean(o * o, axis=-1, keepdims=True) + SUBLN_EPS)
    o_ref[...] = (o * g_ref[...] * (1.0 - lam_init)).astype(o_ref.dtype)


def _attention(q3, qblk, c3, kcblk, vcblk, l3, lam, subln_g, lam_init, tq):
    b, lq, _ = q3.shape
    n_ctx = c3.shape[1]
    n_keys = n_ctx + (l3.shape[1] if l3 is not None else 0)
    tk = 512
    nq = lq // tq
    n_tiles = b * N_HEADS * nq

    def tile(t):
        return t // (N_HEADS * nq), (t // nq) % N_HEADS, t % nq

    enter = lambda t: tile(jnp.minimum(t, n_tiles - 1))
    leave = lambda t: tile(jnp.maximum(t - 1, 0))

    def spec(rows, which, colblk, per_tile):
        def index(t):
            bi, h, i = which(t)
            return (bi, i if per_tile else 0, colblk + h)
        return pl.BlockSpec((None, rows, LANES), index)

    in_specs = [pl.BlockSpec((4, HEAD_DIM), lambda t: (0, 0)), spec(tq, enter, qblk, True),
                spec(n_ctx, enter, kcblk, False)]
    args = [lam, q3, c3]
    n_lat = 0
    if l3 is not None:
        seq = l3.shape[1]
        n_lat = seq // tk
        in_specs.append(spec(seq, enter, P_K_OFF // LANES, False))
        args.append(l3)
    in_specs.append(spec(n_ctx, leave, vcblk, False))
    args.append(c3)
    if l3 is not None:
        in_specs.append(spec(seq, leave, P_V_OFF // LANES, False))
        args.append(l3)
    in_specs.append(pl.BlockSpec((1, VAL_DIM), lambda t: (0, 0)))
    args.append(subln_g.reshape(1, VAL_DIM))
    return pl.pallas_call(
        functools.partial(_attn_kernel, tq=tq, tk=tk, n_lat=n_lat, lam_init=lam_init),
        grid=(n_tiles + 1,),
        in_specs=in_specs,
        out_specs=spec(tq, leave, 0, True),
        out_shape=jax.ShapeDtypeStruct((b, lq, ATTN_V_W), BF16),
        scratch_shapes=[pltpu.VMEM((2 * tq, n_keys), F32), pltpu.VMEM((2 * tq, 1), F32)] * 2,
        compiler_params=_cparams("arbitrary"),
        name="diff_attention",
    )(*args)


def _conv3_rows(buf_ref, w, tm):
    h = BF16_SUBLANES
    return (buf_ref[h - 1:h - 1 + tm, :] * w[0:1] + buf_ref[h:h + tm, :] * w[1:2]
            + buf_ref[h + 1:h + 1 + tm, :] * w[2:3])


def _halo_specs(tm, width, colblk, m, row_axis=0):
    per = tm // BF16_SUBLANES
    last = m // BF16_SUBLANES - 1
    prev = pl.BlockSpec((BF16_SUBLANES, width),
                        lambda *ids: (jnp.maximum(ids[row_axis] * per - 1, 0), colblk))
    nxt = pl.BlockSpec((BF16_SUBLANES, width),
                       lambda *ids: (jnp.minimum((ids[row_axis] + 1) * per, last), colblk))
    return prev, nxt


def _merge_kernel(f_ref, cb_ref, cc_ref, cx_ref, ccp_ref, cxp_ref, ccn_ref, cxn_ref, a_ref,
                  gf_ref, gc_ref, ga_ref, cw_ref, wbf_ref, wbc_ref, wba_ref, o_ref, z_ref, *, tps, nchunk):
    i = pl.program_id(0)
    tm = cc_ref.shape[0]
    h = BF16_SUBLANES
    first = (i % tps) == 0
    last = (i % tps) == tps - 1
    zp = ccp_ref[...].astype(F32) * cxp_ref[...].astype(F32)
    zn = ccn_ref[...].astype(F32) * cxn_ref[...].astype(F32)
    z_ref[0:h, :] = jnp.where(first, 0.0, zp)
    z_ref[h:h + tm, :] = cc_ref[...].astype(F32) * cx_ref[...].astype(F32)
    z_ref[h + tm:2 * h + tm, :] = jnp.where(last, 0.0, zn)
    cv = (cb_ref[...].astype(F32) * _conv3_rows(z_ref, cw_ref[...], tm)).astype(BF16)
    f = f_ref[...]
    a = a_ref[...]
    cn = D_MODEL // nchunk
    for n in range(nchunk):
        cols = slice(n * cn, (n + 1) * cn)
        y = gf_ref[:, cols].astype(F32) * jnp.dot(f, wbf_ref[:, cols], preferred_element_type=F32)
        y += gc_ref[:, cols].astype(F32) * jnp.dot(cv, wbc_ref[:, cols], preferred_element_type=F32)
        y += ga_ref[:, cols].astype(F32) * jnp.dot(a, wba_ref[:, cols], preferred_element_type=F32)
        o_ref[:, cols] = y.astype(o_ref.dtype)


def _merge(p, fmix, attn, conv_w, w_bf, w_bc, w_ba, tm, seq_len):
    m = p.shape[0]
    d = D_MODEL
    cw = CONV_W
    tps = seq_len // tm
    blk = lambda width, c: pl.BlockSpec((tm, width), lambda i: (i, c))
    ccp, ccn = _halo_specs(tm, cw, 2, m)
    cxp, cxn = _halo_specs(tm, cw, 3, m)
    gate0 = P_GATE_OFF // d
    full = lambda r, c: pl.BlockSpec((r, c), lambda i: (0, 0))
    return pl.pallas_call(
        functools.partial(_merge_kernel, tps=tps, nchunk=4),
        grid=(m // tm,),
        in_specs=[
            blk(FOURIER_W, 0),
            blk(cw, 1), blk(cw, 2), blk(cw, 3),
            ccp, cxp, ccn, cxn,
            blk(ATTN_V_W, 0),
            pl.BlockSpec((tm, d), lambda i: (i, gate0)),
            pl.BlockSpec((tm, d), lambda i: (i, gate0 + 1)),
            pl.BlockSpec((tm, d), lambda i: (i, gate0 + 2)),
            full(3, cw), full(FOURIER_W, d), full(cw, d), full(ATTN_V_W, d),
        ],
        out_specs=pl.BlockSpec((tm, d), lambda i: (i, 0)),
        out_shape=jax.ShapeDtypeStruct((m, d), BF16),
        scratch_shapes=[pltpu.VMEM((tm + 2 * BF16_SUBLANES, cw), F32)],
        compiler_params=_cparams("arbitrary"),
        name="branch_merge",
    )(fmix, p, p, p, p, p, p, p, attn, p, p, p, conv_w, w_bf, w_bc, w_ba)


def _outproj_kernel(m_ref, w_ref, h_ref, gt_ref, g_ref, sh_ref, sc_ref, ho_ref, xo_ref):
    y = jnp.dot(m_ref[...], w_ref[...], preferred_element_type=F32)
    h = h_ref[...] + gt_ref[...] * y
    ho_ref[...] = h
    xo_ref[...] = (_rms(h, g_ref[...]) * (1.0 + sc_ref[...]) + sh_ref[...]).astype(xo_ref.dtype)


def _out_proj(merged, w_out, h, g2, mods, mod_row, tm):
    m, d = h.shape
    return pl.pallas_call(
        _outproj_kernel,
        grid=(m // tm,),
        in_specs=[
            pl.BlockSpec((tm, d), lambda i: (i, 0)),
            pl.BlockSpec((d, d), lambda i: (0, 0)),
            pl.BlockSpec((tm, d), lambda i: (i, 0)),
            _mod_spec(2, mod_row, tm),
            pl.BlockSpec((1, d), lambda i: (0, 0)),
            _mod_spec(3, mod_row, tm),
            _mod_spec(4, mod_row, tm),
        ],
        out_specs=[pl.BlockSpec((tm, d), lambda i: (i, 0)), pl.BlockSpec((tm, d), lambda i: (i, 0))],
        out_shape=[jax.ShapeDtypeStruct((m, d), F32), jax.ShapeDtypeStruct((m, d), BF16)],
        compiler_params=_cparams("arbitrary"),
        name="out_proj",
    )(merged, w_out, h, mods, g2.reshape(1, d), mods, mods)


def _ffn_up_kernel(x_ref, xp_ref, xn_ref, wa_ref, wv_ref, ca_ref, cv_ref, o_ref,
                   wab_ref, wvb_ref, xh_ref, ua_ref, uv_ref, *, tps):
    i = pl.program_id(1)
    tm = x_ref.shape[0]
    hl = BF16_SUBLANES

    @pl.when(i == 0)
    def _():
        wab_ref[...] = wa_ref[...].astype(BF16)
        wvb_ref[...] = wv_ref[...].astype(BF16)

    first = (i % tps) == 0
    last = (i % tps) == tps - 1
    xh_ref[0:hl, :] = jnp.where(first, jnp.zeros_like(xp_ref[...]), xp_ref[...])
    xh_ref[hl:hl + tm, :] = x_ref[...]
    xh_ref[hl + tm:2 * hl + tm, :] = jnp.where(last, jnp.zeros_like(xn_ref[...]), xn_ref[...])
    xh = xh_ref[...]
    ua_ref[...] = jnp.dot(xh, wab_ref[...], preferred_element_type=F32)
    uv_ref[...] = jnp.dot(xh, wvb_ref[...], preferred_element_type=F32)
    a = _conv3_rows(ua_ref, ca_ref[...], tm)
    v = _conv3_rows(uv_ref, cv_ref[...], tm)
    o_ref[...] = (a * jax.nn.sigmoid(a) * v).astype(o_ref.dtype)


def _ffn_down_kernel(a_ref, w_ref, h_ref, gt_ref, o_ref, wb_ref):
    @pl.when(pl.program_id(1) == 0)
    def _():
        wb_ref[...] = w_ref[...].astype(BF16)

    y = jnp.dot(a_ref[...], wb_ref[...], preferred_element_type=F32)
    o_ref[...] = h_ref[...] + gt_ref[...] * y


def _ffn(xm, w_up, conv_w, w_down, layer, h, mods, mod_row, tm, seq_len):
    m, d = xm.shape
    tf = FFN_TF
    nj = D_FF // tf
    tps = seq_len // tm
    xp, xn = _halo_specs(tm, d, 0, m, row_axis=1)
    act = pl.pallas_call(
        functools.partial(_ffn_up_kernel, tps=tps),
        grid=(nj, m // tm),
        in_specs=[
            pl.BlockSpec((tm, d), lambda j, i: (i, 0)),
            xp, xn,
            pl.BlockSpec((None, d, tf), lambda j, i: (layer, 0, j)),
            pl.BlockSpec((None, d, tf), lambda j, i: (layer, 0, nj + j)),
            pl.BlockSpec((None, 3, tf), lambda j, i: (layer, 0, j)),
            pl.BlockSpec((None, 3, tf), lambda j, i: (layer, 0, nj + j)),
        ],
        out_specs=pl.BlockSpec((tm, tf), lambda j, i: (i, j)),
        out_shape=jax.ShapeDtypeStruct((m, D_FF), BF16),
        scratch_shapes=[
            pltpu.VMEM((d, tf), BF16),
            pltpu.VMEM((d, tf), BF16),
            pltpu.VMEM((tm + 2 * BF16_SUBLANES, d), BF16),
            pltpu.VMEM((tm + 2 * BF16_SUBLANES, tf), F32),
            pltpu.VMEM((tm + 2 * BF16_SUBLANES, tf), F32),
        ],
        compiler_params=_cparams("arbitrary", "arbitrary"),
        name="ffn_up",
    )(xm, xm, xm, w_up, w_up, conv_w, conv_w)

    tn = FFN_DOWN_TN
    tmd = min(m, FFN_DOWN_TM)
    gate_row = lambda n, i: mod_row(i * tmd)
    return pl.pallas_call(
        _ffn_down_kernel,
        grid=(d // tn, m // tmd),
        in_specs=[
            pl.BlockSpec((tmd, D_FF), lambda n, i: (i, 0)),
            pl.BlockSpec((None, D_FF, tn), lambda n, i: (layer, 0, n)),
            pl.BlockSpec((tmd, tn), lambda n, i: (i, n)),
            pl.BlockSpec((None, None, 1, tn), lambda n, i: (gate_row(n, i), 5, 0, n)),
        ],
        out_specs=pl.BlockSpec((tmd, tn), lambda n, i: (i, n)),
        out_shape=jax.ShapeDtypeStruct((m, d), F32),
        scratch_shapes=[pltpu.VMEM((D_FF, tn), BF16)],
        compiler_params=_cparams("arbitrary", "arbitrary"),
        name="ffn_down",
    )(act, w_down, h, mods)


def _rope_tables(length):
    n_freq = HEAD_DIM // 4
    pos = jnp.arange(length)
    row = (pos // GRID_W).astype(F32)
    col = (pos % GRID_W).astype(F32)
    inv = ROPE_THETA ** (-(2.0 * jnp.arange(n_freq, dtype=F32)) / (HEAD_DIM // 2))
    ang_r, ang_c = row[:, None] * inv, col[:, None] * inv
    ang = jnp.concatenate([ang_r, ang_r, ang_c, ang_c], axis=1)
    cos = jnp.tile(jnp.cos(ang), (1, 2))
    sin = jnp.tile(jnp.sin(ang), (1, 2))
    is_x1 = (jnp.arange(LANES) % (2 * n_freq)) < n_freq
    return cos, jnp.where(is_x1, -sin, 0.0), jnp.where(is_x1, 0.0, sin)


def _dft_mats(n):
    r = math.isqrt(n)
    assert r * r == n
    k = jnp.arange(n, dtype=jnp.int32)
    s = jnp.arange(r, dtype=jnp.int32)
    ang_a = ((k[:, None] * s[None, :]) % r).astype(F32) * (2.0 * math.pi / r)
    ang_b = ((k[:, None] * s[None, :]) % n).astype(F32) * (2.0 * math.pi / n)
    ca, sa = jnp.cos(ang_a)[:, :, None], jnp.sin(ang_a)[:, :, None]
    cb, sb = jnp.cos(ang_b)[:, None, :], jnp.sin(ang_b)[:, None, :]
    cos = (ca * cb - sa * sb).reshape(n, n)
    sin = (sa * cb + ca * sb).reshape(n, n)
    return cos.astype(BF16), (-sin).astype(BF16)


def _channel_dft():
    c = np.arange(FOURIER_GROUP_W)
    ang = 2.0 * np.pi * ((c[:, None] * c[None, :]) % FOURIER_GROUP_W) / FOURIER_GROUP_W
    return jnp.asarray(np.concatenate([np.cos(ang), np.sin(ang)], axis=1), dtype=BF16)


def _mixer(h, xm, seq_len, rope_tabs, dft, ctx_kv, w, mods, mod_row, lam_init, tm, tq,
           proj_cols=(0, PROJ_TILES)):
    m = h.shape[0]
    b = m // seq_len
    p = _in_proj(xm, w["w_in"], w["layer"], rope_tabs, min(m, 1024), proj_cols[0], proj_cols[1], seq_len)
    if proj_cols[1] != PROJ_TILES:
        return None, p
    p3 = p.reshape(b, seq_len, N_IN)
    fmix = _fourier(p3, w["cs"], dft[0], dft[1], min(seq_len, 512)).reshape(m, FOURIER_W)
    if ctx_kv is None:
        attn = _attention(p3, P_Q_OFF // LANES, p3, P_K_OFF // LANES, P_V_OFF // LANES, None,
                          w["lambdas"], w["subln_g"], lam_init, tq)
    else:
        c3, kcblk, vcblk = ctx_kv
        attn = _attention(p3, P_Q_OFF // LANES, c3, kcblk, vcblk, p3,
                          w["lambdas"], w["subln_g"], lam_init, tq)
    merged = _merge(p, fmix, attn.reshape(m, ATTN_V_W), w["conv_mix_w"], w["w_br_fourier"],
                    w["w_br_conv"], w["w_br_attn"], tm, seq_len)
    h_mid, xm2 = _out_proj(merged, w["w_out"], h, w["g_norm2"], mods, mod_row, min(tm, 256))
    h_new = _ffn(xm2, w["w_ffn_up"], w["ffn_conv_w"], w["w_ffn_down"], w["layer"], h_mid, mods, mod_row,
                 min(seq_len, FFN_UP_TM), seq_len)
    return h_new, p


def kernel(x, c, ctx, c_ctx, w_mod, b_mod, g_norm1, g_norm2, w_in, conv_mix_w, lambdas, subln_g,
           w_br_fourier, w_br_conv, w_br_attn, w_out, w_ffn_up, ffn_conv_w, w_ffn_down, g_final):
    b, seq, d = x.shape
    n_ctx = ctx.shape[1]
    depth = w_mod.shape[0]
    assert d == D_MODEL and b + 1 <= MOD_ROWS

    wb = lambda a: a.astype(BF16)
    w_bf, w_bc, w_ba, w_o = wb(w_br_fourier), wb(w_br_conv), wb(w_br_attn), wb(w_out)

    rope_tabs = _rope_tables(seq)
    dft_l = _dft_mats(seq)
    dft_c = _dft_mats(n_ctx)
    cs = _channel_dft()

    cvec = jnp.zeros((MOD_ROWS, d), F32).at[:b].set(c).at[b].set(c_ctx)
    mods_all = _modulation(cvec, w_mod, b_mod).reshape(depth, MOD_ROWS, 6, 1, d)

    tm_l = min(512, seq)
    tm_c = n_ctx
    tq = min(256, seq)
    lat_row = lambda r: r // seq
    ctx_row = lambda r: b

    h = x.reshape(b * seq, d)
    hc = ctx.reshape(b * n_ctx, d)
    for i in range(depth):
        last = i == depth - 1
        lam_init = 0.8 - 0.6 * math.exp(-0.3 * i)
        mods = mods_all[i]
        w = dict(w_in=w_in, layer=i, cs=cs, lambdas=lambdas[i], subln_g=subln_g[i], conv_mix_w=conv_mix_w[i],
                 w_br_fourier=w_bf[i], w_br_conv=w_bc[i], w_br_attn=w_ba[i], w_out=w_o[i],
                 g_norm2=g_norm2[i], w_ffn_up=w_ffn_up, ffn_conv_w=ffn_conv_w, w_ffn_down=w_ffn_down)

        xc = _norm_modulate(hc, g_norm1[i], mods, ctx_row, tm_c)
        xl = _norm_modulate(h, g_norm1[i], mods, lat_row, tm_l)
        if last:
            _, pc = _mixer(hc, xc, n_ctx, None, dft_c, None, w, mods, ctx_row, lam_init, tm_c, n_ctx,
                           proj_cols=(PROJ_K_TILE, 2))
            ctx_kv = (pc.reshape(b, n_ctx, 2 * PROJ_TN), 0, PROJ_TN // LANES)
            hc_new = hc
        else:
            hc_new, pc = _mixer(hc, xc, n_ctx, None, dft_c, None, w, mods, ctx_row, lam_init, tm_c, n_ctx)
            ctx_kv = (pc.reshape(b, n_ctx, N_IN), P_K_OFF // LANES, P_V_OFF // LANES)
        h, _ = _mixer(h, xl, seq, rope_tabs, dft_l, ctx_kv, w, mods, lat_row, lam_init, tm_l, tq)
        hc = hc_new
    return _final_norm(h, g_final, tm_l).reshape(b, seq, d)
```

```python
import functools
import math

import numpy as np
import jax
import jax.numpy as jnp
from jax import lax
from jax.experimental import pallas as pl
from jax.experimental.pallas import tpu as pltpu

F32 = jnp.float32
BF16 = jnp.bfloat16

D_MODEL = 2048
GRID_W = 64
FOURIER_GROUPS = 4
FOURIER_GROUP_W = D_MODEL // 16
FOURIER_W = FOURIER_GROUPS * FOURIER_GROUP_W
CONV_W = D_MODEL // 4
N_HEADS = 8
HEAD_DIM = D_MODEL // (4 * N_HEADS)
VAL_DIM = 2 * HEAD_DIM
ATTN_QK_W = N_HEADS * 2 * HEAD_DIM
ATTN_V_W = N_HEADS * VAL_DIM
ROPE_THETA = 10000.0
ATTN_SCALE = HEAD_DIM ** -0.5
SUBLN_EPS = 1e-5
Q_OFF = FOURIER_W + 3 * CONV_W
K_OFF = Q_OFF + ATTN_QK_W
V_OFF = K_OFF + ATTN_QK_W
V_END = V_OFF + ATTN_V_W
N_IN = V_END + 3 * D_MODEL
D_FF = ((8 * D_MODEL // 3 + 255) // 256) * 256
EPS = 1e-6

LANES = 128
MXU_COLS = 256
BF16_SUBLANES = 16
VMEM_LIMIT = 56 * 1024 * 1024
MOD_ROWS = 8

PROJ_TN = 1024
P_GATE_OFF = Q_OFF
P_Q_OFF = P_GATE_OFF + 3 * D_MODEL
P_K_OFF = P_Q_OFF + ATTN_QK_W
P_V_OFF = P_K_OFF + ATTN_QK_W
PROJ_GATE_TILE = P_GATE_OFF // PROJ_TN
PROJ_Q_TILE = P_Q_OFF // PROJ_TN
PROJ_K_TILE = P_K_OFF // PROJ_TN
PROJ_TILES = N_IN // PROJ_TN
ATTN_ROW_BLOCK = 32
ATTN_PV_ROWS = 128
ATTN_FLASH_TK = 512
FFN_TF = 512
FFN_UP_TM = 1024
FFN_DOWN_TN = 512
FFN_DOWN_TM = 512


def _cparams(*sem):
    return pltpu.CompilerParams(dimension_semantics=sem, vmem_limit_bytes=VMEM_LIMIT)


def _mod_kernel(c_ref, w_ref, b_ref, o_ref):
    c = c_ref[...]
    s = c * jax.nn.sigmoid(c)
    s_hi = s.astype(BF16)
    s_lo = (s - s_hi.astype(F32)).astype(BF16)
    w = w_ref[...]
    w_hi = w.astype(BF16)
    w_lo = (w - w_hi.astype(F32)).astype(BF16)
    y = jnp.dot(s_hi, w_hi, preferred_element_type=F32)
    y += jnp.dot(s_lo, w_hi, preferred_element_type=F32)
    y += jnp.dot(s_hi, w_lo, preferred_element_type=F32)
    o_ref[...] = y + b_ref[...]


def _modulation(cvec, w_mod, b_mod):
    depth, d, n = w_mod.shape
    tn = 1024
    return pl.pallas_call(
        _mod_kernel,
        grid=(depth, n // tn),
        in_specs=[
            pl.BlockSpec((MOD_ROWS, d), lambda l, j: (0, 0)),
            pl.BlockSpec((None, d, tn), lambda l, j: (l, 0, j)),
            pl.BlockSpec((None, 1, tn), lambda l, j: (l, 0, j)),
        ],
        out_specs=pl.BlockSpec((None, MOD_ROWS, tn), lambda l, j: (l, 0, j)),
        out_shape=jax.ShapeDtypeStruct((depth, MOD_ROWS, n), F32),
        compiler_params=_cparams("arbitrary", "arbitrary"),
        name="modulation",
    )(cvec, w_mod, b_mod.reshape(depth, 1, n))


def _mod_spec(which, mod_row, tm):
    return pl.BlockSpec((None, None, 1, D_MODEL), lambda i, *_: (mod_row(i * tm), which, 0, 0))


def _rms(x, g):
    return x * lax.rsqrt(jnp.mean(x * x, axis=-1, keepdims=True) + EPS) * g


def _norm_mod_kernel(h_ref, g_ref, sh_ref, sc_ref, o_ref):
    y = _rms(h_ref[...], g_ref[...])
    o_ref[...] = (y * (1.0 + sc_ref[...]) + sh_ref[...]).astype(o_ref.dtype)


def _norm_kernel(h_ref, g_ref, o_ref):
    o_ref[...] = _rms(h_ref[...], g_ref[...]).astype(o_ref.dtype)


def _norm_modulate(h, g, mods, mod_row, tm):
    m, d = h.shape
    return pl.pallas_call(
        _norm_mod_kernel,
        grid=(m // tm,),
        in_specs=[
            pl.BlockSpec((tm, d), lambda i: (i, 0)),
            pl.BlockSpec((1, d), lambda i: (0, 0)),
            _mod_spec(0, mod_row, tm),
            _mod_spec(1, mod_row, tm),
        ],
        out_specs=pl.BlockSpec((tm, d), lambda i: (i, 0)),
        out_shape=jax.ShapeDtypeStruct((m, d), BF16),
        compiler_params=_cparams("arbitrary"),
        name="norm_modulate",
    )(h, g.reshape(1, d), mods, mods)


def _final_norm(h, g, tm):
    m, d = h.shape
    return pl.pallas_call(
        _norm_kernel,
        grid=(m // tm,),
        in_specs=[pl.BlockSpec((tm, d), lambda i: (i, 0)), pl.BlockSpec((1, d), lambda i: (0, 0))],
        out_specs=pl.BlockSpec((tm, d), lambda i: (i, 0)),
        out_shape=jax.ShapeDtypeStruct((m, d), F32),
        compiler_params=_cparams("arbitrary"),
        name="final_norm",
    )(h, g.reshape(1, d))


def _proj_kernel(*refs, col0, rope):
    if rope:
        x_ref, w_ref, cos_ref, sin_up_ref, sin_dn_ref, o_ref, wb_ref = refs
    else:
        x_ref, w_ref, o_ref, wb_ref = refs
    j = pl.program_id(0) + col0

    @pl.when(pl.program_id(1) == 0)
    def _():
        wb_ref[...] = w_ref[...].astype(BF16)

    is_q = j == PROJ_Q_TILE
    is_k = j == PROJ_K_TILE
    is_gate = (j >= PROJ_GATE_TILE) & (j < PROJ_Q_TILE)
    x = x_ref[...]

    def column_chunks(epilogue):
        for c in range(PROJ_TN // MXU_COLS):
            cols = slice(c * MXU_COLS, (c + 1) * MXU_COLS)
            acc = jnp.dot(x, wb_ref[:, cols], preferred_element_type=F32)
            o_ref[:, cols] = epilogue(acc).astype(o_ref.dtype)

    def rotary(acc):
        qscale = jnp.where(is_q, ATTN_SCALE * math.log2(math.e), 1.0)
        half = HEAD_DIM // 4
        out = []
        for hd in range(MXU_COLS // LANES):
            a = acc[:, hd * LANES:(hd + 1) * LANES]
            if rope:
                a = (a * cos_ref[...] + pltpu.roll(a, LANES - half, 1) * sin_up_ref[...]
                     + pltpu.roll(a, half, 1) * sin_dn_ref[...])
            out.append(a * qscale)
        return jnp.concatenate(out, axis=1)

    pl.when(is_q | is_k)(lambda: column_chunks(rotary))
    pl.when(is_gate)(lambda: column_chunks(jax.nn.sigmoid))
    pl.when(jnp.logical_not(is_q | is_k | is_gate))(lambda: column_chunks(lambda acc: acc))


def _in_proj(xm, w_in, layer, rope_tabs, tm, col0, ncols, seq_len):
    m, d = xm.shape
    tn = PROJ_TN
    n_plain = PROJ_GATE_TILE
    n_gate = PROJ_Q_TILE - PROJ_GATE_TILE
    n_qkv = PROJ_TILES - PROJ_Q_TILE

    def w_tile(j):
        j = j + col0
        return jnp.where(j < n_plain, j, jnp.where(j < n_plain + n_gate, j + n_qkv, j - n_gate))

    in_specs = [
        pl.BlockSpec((tm, d), lambda j, i: (i, 0)),
        pl.BlockSpec((None, d, tn), lambda j, i: (layer, 0, w_tile(j))),
    ]
    args = [xm, w_in]
    rope = rope_tabs is not None
    if rope:
        tps = seq_len // tm
        in_specs += [pl.BlockSpec((tm, LANES), lambda j, i: (i % tps, 0))] * 3
        args += list(rope_tabs)
    return pl.pallas_call(
        functools.partial(_proj_kernel, col0=col0, rope=rope),
        grid=(ncols, m // tm),
        in_specs=in_specs,
        out_specs=pl.BlockSpec((tm, tn), lambda j, i: (i, j)),
        out_shape=jax.ShapeDtypeStruct((m, ncols * tn), BF16),
        scratch_shapes=[pltpu.VMEM((d, tn), BF16)],
        compiler_params=_cparams("arbitrary", "arbitrary"),
        name="in_proj",
    )(*args)


def _fourier_kernel(f_ref, cs_ref, cl_ref, sl_ref, o_ref, uc_ref, us_ref, *, chunk, scale):
    i = pl.program_id(1)
    seq = f_ref.shape[0]
    gw = FOURIER_GROUP_W

    @pl.when(i == 0)
    def _():
        for r in range(seq // chunk):
            rows = slice(r * chunk, (r + 1) * chunk)
            for g in range(FOURIER_GROUPS):
                cols = slice(g * gw, (g + 1) * gw)
                y = jnp.dot(f_ref[rows, cols], cs_ref[...], preferred_element_type=F32)
                uc_ref[rows, cols] = y[:, :gw].astype(BF16)
                us_ref[rows, cols] = y[:, gw:].astype(BF16)

    y = jnp.dot(cl_ref[...], uc_ref[...], preferred_element_type=F32)
    y += jnp.dot(sl_ref[...], us_ref[...], preferred_element_type=F32)
    o_ref[...] = (y * scale).astype(o_ref.dtype)


def _fourier(p3, cs, cl, sln, tr):
    b, seq, _ = p3.shape
    chunk = min(seq, 1024)
    scale = 1.0 / math.sqrt(seq * FOURIER_GROUP_W)
    return pl.pallas_call(
        functools.partial(_fourier_kernel, chunk=chunk, scale=scale),
        grid=(b, seq // tr),
        in_specs=[
            pl.BlockSpec((None, seq, FOURIER_W), lambda bi, i: (bi, 0, 0)),
            pl.BlockSpec((FOURIER_GROUP_W, 2 * FOURIER_GROUP_W), lambda bi, i: (0, 0)),
            pl.BlockSpec((tr, seq), lambda bi, i: (i, 0)),
            pl.BlockSpec((tr, seq), lambda bi, i: (i, 0)),
        ],
        out_specs=pl.BlockSpec((None, tr, FOURIER_W), lambda bi, i: (bi, i, 0)),
        out_shape=jax.ShapeDtypeStruct((b, seq, FOURIER_W), BF16),
        scratch_shapes=[pltpu.VMEM((seq, FOURIER_W), BF16), pltpu.VMEM((seq, FOURIER_W), BF16)],
        compiler_params=_cparams("arbitrary", "arbitrary"),
        name="fourier",
    )(p3, cs, cl, sln)


def _attn_flash_kernel(*refs, tq, tk, n_lat, lam_init):
    if n_lat:
        lam_ref, q_ref, kc_ref, vc_ref, kl_ref, vl_ref, g_ref, o_ref = refs
    else:
        lam_ref, q_ref, kc_ref, vc_ref, g_ref, o_ref = refs
        kl_ref = vl_ref = None
    q = q_ref[...]
    lane = lax.broadcasted_iota(jnp.int32, (1, LANES), 1)
    comp0 = lane < HEAD_DIM
    zero = jnp.zeros_like(q)
    qq = jnp.concatenate([jnp.where(comp0, q, zero), jnp.where(comp0, zero, q)], axis=0)
    n_ctx = kc_ref.shape[0]
    chunks = [(kc_ref, vc_ref, 0, n_ctx)] + [(kl_ref, vl_ref, c * tk, tk) for c in range(n_lat)]

    def lane_fold(x, op):
        part = x[:, 0:LANES]
        for c in range(1, x.shape[1] // LANES):
            part = op(part, x[:, c * LANES:(c + 1) * LANES])
        return part

    m = l = acc = None
    for kr, vr, r0, sz in chunks:
        s = lax.dot_general(qq, kr[r0:r0 + sz, :], (((1,), (1,)), ((), ())), preferred_element_type=F32)
        mc = jnp.max(lane_fold(s, jnp.maximum), axis=-1, keepdims=True)
        m_new = mc if m is None else jnp.maximum(m, mc)
        p = jnp.exp2(s - m_new)
        lc = lane_fold(p, jnp.add)
        y = jnp.dot(p.astype(BF16), vr[r0:r0 + sz, :], preferred_element_type=F32)
        if m is None:
            l, acc = lc, y
        else:
            alpha = jnp.exp2(m - m_new)
            l = alpha * l + lc
            acc = alpha * acc + y
        m = m_new
    o = acc / jnp.sum(l, axis=-1, keepdims=True)
    lf = lam_ref[...]
    lam = (jnp.exp(jnp.sum(lf[0:1] * lf[1:2], axis=-1, keepdims=True))
           - jnp.exp(jnp.sum(lf[2:3] * lf[3:4], axis=-1, keepdims=True)) + lam_init)
    o = o[:tq] - lam * o[tq:]
    o = o * lax.rsqrt(jnp.mean(o * o, axis=-1, keepdims=True) + SUBLN_EPS)
    o_ref[...] = (o * g_ref[...] * (1.0 - lam_init)).astype(o_ref.dtype)


def _attention_flash(q3, qblk, c3, kcblk, vcblk, l3, lam, subln_g, lam_init, tq):
    b, lq, _ = q3.shape
    n_ctx = c3.shape[1]
    tk = ATTN_FLASH_TK
    in_specs = [
        pl.BlockSpec((4, HEAD_DIM), lambda bi, h, i: (0, 0)),
        pl.BlockSpec((None, tq, LANES), lambda bi, h, i: (bi, i, qblk + h)),
        pl.BlockSpec((None, n_ctx, LANES), lambda bi, h, i: (bi, 0, kcblk + h)),
        pl.BlockSpec((None, n_ctx, LANES), lambda bi, h, i: (bi, 0, vcblk + h)),
    ]
    args = [lam, q3, c3, c3]
    n_lat = 0
    if l3 is not None:
        seq = l3.shape[1]
        n_lat = seq // min(tk, seq)
        tk = seq // n_lat
        in_specs += [
            pl.BlockSpec((None, seq, LANES), lambda bi, h, i: (bi, 0, P_K_OFF // LANES + h)),
            pl.BlockSpec((None, seq, LANES), lambda bi, h, i: (bi, 0, P_V_OFF // LANES + h)),
        ]
        args += [l3, l3]
    in_specs.append(pl.BlockSpec((1, VAL_DIM), lambda bi, h, i: (0, 0)))
    args.append(subln_g.reshape(1, VAL_DIM))
    return pl.pallas_call(
        functools.partial(_attn_flash_kernel, tq=tq, tk=tk, n_lat=n_lat, lam_init=lam_init),
        grid=(b, N_HEADS, lq // tq),
        in_specs=in_specs,
        out_specs=pl.BlockSpec((None, tq, LANES), lambda bi, h, i: (bi, i, h)),
        out_shape=jax.ShapeDtypeStruct((b, lq, ATTN_V_W), BF16),
        compiler_params=_cparams("arbitrary", "arbitrary", "arbitrary"),
        name="diff_attention",
    )(*args)


def _attn_kernel(*refs, tq, tk, n_lat, lam_init):
    if n_lat:
        lam_ref, q_ref, kc_ref, kl_ref, vc_ref, vl_ref, g_ref, o_ref, sa_ref, sb_ref, p_ref, l_ref = refs
    else:
        lam_ref, q_ref, kc_ref, vc_ref, g_ref, o_ref, sa_ref, sb_ref, p_ref, l_ref = refs
        kl_ref = vl_ref = None
    t = pl.program_id(0)

    @pl.when(t == 0)
    def _():
        sb_ref[...] = jnp.zeros(sb_ref.shape, F32)

    step = functools.partial(_attn_step, lam_ref, q_ref, kc_ref, kl_ref, vc_ref, vl_ref, g_ref, o_ref,
                             p_new_ref=p_ref, l_new_ref=l_ref, tq=tq, tk=tk, n_lat=n_lat, lam_init=lam_init)
    pl.when(t % 2 == 0)(lambda: step(s_new_ref=sa_ref, s_old_ref=sb_ref))
    pl.when(t % 2 == 1)(lambda: step(s_new_ref=sb_ref, s_old_ref=sa_ref))


def _attn_step(lam_ref, q_ref, kc_ref, kl_ref, vc_ref, vl_ref, g_ref, o_ref, *, s_new_ref, s_old_ref,
               p_new_ref, l_new_ref, tq, tk, n_lat, lam_init):
    q = q_ref[...]
    lane = lax.broadcasted_iota(jnp.int32, (1, LANES), 1)
    comp0 = lane < HEAD_DIM
    zero = jnp.zeros_like(q)
    qq = jnp.concatenate([jnp.where(comp0, q, zero), jnp.where(comp0, zero, q)], axis=0)
    n_ctx = kc_ref.shape[0]
    n_keys = s_new_ref.shape[1]

    def scores(kr, r0, sz, c0):
        s_new_ref[:, c0:c0 + sz] = lax.dot_general(qq, kr[r0:r0 + sz, :], (((1,), (1,)), ((), ())),
                                                   preferred_element_type=F32)

    def finish_rows(g):
        rows = slice(g * ATTN_PV_ROWS, (g + 1) * ATTN_PV_ROWS)
        mx = s_old_ref[rows, 0:LANES]
        for c in range(1, n_keys // LANES):
            mx = jnp.maximum(mx, s_old_ref[rows, c * LANES:(c + 1) * LANES])
        m = jnp.max(mx, axis=-1, keepdims=True)
        lsum = jnp.zeros((ATTN_PV_ROWS, LANES), F32)
        acc = jnp.zeros((ATTN_PV_ROWS, VAL_DIM), F32)
        for vr, r0, sz, c0 in [(vc_ref, 0, n_ctx, 0)] + [(vl_ref, c * tk, tk, n_ctx + c * tk)
                                                         for c in range(n_lat)]:
            p = jnp.exp2(s_old_ref[rows, c0:c0 + sz] - m)
            for c in range(sz // LANES):
                lsum += p[:, c * LANES:(c + 1) * LANES]
            acc += jnp.dot(p.astype(BF16), vr[r0:r0 + sz, :], preferred_element_type=F32)
        return acc / jnp.sum(lsum, axis=-1, keepdims=True)

    scores(kc_ref, 0, n_ctx, 0)
    for c in range(n_lat):
        scores(kl_ref, c * tk, tk, n_ctx + c * tk)
    o = jnp.concatenate([finish_rows(g) for g in range(2 * tq // ATTN_PV_ROWS)], axis=0)
    lf = lam_ref[...]
    lam = (jnp.exp(jnp.sum(lf[0:1] * lf[1:2], axis=-1, keepdims=True))
           - jnp.exp(jnp.sum(lf[2:3] * lf[3:4], axis=-1, keepdims=True)) + lam_init)
    o = o[:tq] - lam * o[tq:]
    o = o * lax.rsqrt(jnp.mean(o * o, axis=-1, keepdims=True) + SUBLN_EPS)
    o_ref[...] = (o * g_ref[...] * (1.0 - lam_init)).astype(o_ref.dtype)


def _attention(q3, qblk, c3, kcblk, vcblk, l3, lam, subln_g, lam_init, tq):
    b, lq, _ = q3.shape
    n_ctx = c3.shape[1]
    n_keys = n_ctx + (l3.shape[1] if l3 is not None else 0)
    tk = 512
    nq = lq // tq
    n_tiles = b * N_HEADS * nq

    def tile(t):
        return t // (N_HEADS * nq), (t // nq) % N_HEADS, t % nq

    enter = lambda t: tile(jnp.minimum(t, n_tiles - 1))
    leave = lambda t: tile(jnp.maximum(t - 1, 0))

    def spec(rows, which, colblk, per_tile):
        def index(t):
            bi, h, i = which(t)
            return (bi, i if per_tile else 0, colblk + h)
        return pl.BlockSpec((None, rows, LANES), index)

    in_specs = [pl.BlockSpec((4, HEAD_DIM), lambda t: (0, 0)), spec(tq, enter, qblk, True),
                spec(n_ctx, enter, kcblk, False)]
    args = [lam, q3, c3]
    n_lat = 0
    if l3 is not None:
        seq = l3.shape[1]
        n_lat = seq // tk
        in_specs.append(spec(seq, enter, P_K_OFF // LANES, False))
        args.append(l3)
    in_specs.append(spec(n_ctx, leave, vcblk, False))
    args.append(c3)
    if l3 is not None:
        in_specs.append(spec(seq, leave, P_V_OFF // LANES, False))
        args.append(l3)
    in_specs.append(pl.BlockSpec((1, VAL_DIM), lambda t: (0, 0)))
    args.append(subln_g.reshape(1, VAL_DIM))
    return pl.pallas_call(
        functools.partial(_attn_kernel, tq=tq, tk=tk, n_lat=n_lat, lam_init=lam_init),
        grid=(n_tiles + 1,),
        in_specs=in_specs,
        out_specs=spec(tq, leave, 0, True),
        out_shape=jax.ShapeDtypeStruct((b, lq, ATTN_V_W), BF16),
        scratch_shapes=[pltpu.VMEM((2 * tq, n_keys), F32), pltpu.VMEM((2 * tq, n_keys), F32),
                        pltpu.VMEM((2 * tq, n_keys), BF16), pltpu.VMEM((2 * tq, 1), F32)],
        compiler_params=_cparams("arbitrary"),
        name="diff_attention",
    )(*args)


def _conv3_rows(buf_ref, w, tm):
    h = BF16_SUBLANES
    return (buf_ref[h - 1:h - 1 + tm, :] * w[0:1] + buf_ref[h:h + tm, :] * w[1:2]
            + buf_ref[h + 1:h + 1 + tm, :] * w[2:3])


def _halo_specs(tm, width, colblk, m, row_axis=0):
    per = tm // BF16_SUBLANES
    last = m // BF16_SUBLANES - 1
    prev = pl.BlockSpec((BF16_SUBLANES, width),
                        lambda *ids: (jnp.maximum(ids[row_axis] * per - 1, 0), colblk))
    nxt = pl.BlockSpec((BF16_SUBLANES, width),
                       lambda *ids: (jnp.minimum((ids[row_axis] + 1) * per, last), colblk))
    return prev, nxt


def _merge_kernel(f_ref, cb_ref, cc_ref, cx_ref, ccp_ref, cxp_ref, ccn_ref, cxn_ref, a_ref,
                  gf_ref, gc_ref, ga_ref, cw_ref, wbf_ref, wbc_ref, wba_ref, o_ref, z_ref, *, tps, nchunk):
    i = pl.program_id(0)
    tm = cc_ref.shape[0]
    h = BF16_SUBLANES
    first = (i % tps) == 0
    last = (i % tps) == tps - 1
    zp = ccp_ref[...].astype(F32) * cxp_ref[...].astype(F32)
    zn = ccn_ref[...].astype(F32) * cxn_ref[...].astype(F32)
    z_ref[0:h, :] = jnp.where(first, 0.0, zp)
    z_ref[h:h + tm, :] = cc_ref[...].astype(F32) * cx_ref[...].astype(F32)
    z_ref[h + tm:2 * h + tm, :] = jnp.where(last, 0.0, zn)
    cv = (cb_ref[...].astype(F32) * _conv3_rows(z_ref, cw_ref[...], tm)).astype(BF16)
    f = f_ref[...]
    a = a_ref[...]
    cn = D_MODEL // nchunk
    for n in range(nchunk):
        cols = slice(n * cn, (n + 1) * cn)
        y = gf_ref[:, cols].astype(F32) * jnp.dot(f, wbf_ref[:, cols], preferred_element_type=F32)
        y += gc_ref[:, cols].astype(F32) * jnp.dot(cv, wbc_ref[:, cols], preferred_element_type=F32)
        y += ga_ref[:, cols].astype(F32) * jnp.dot(a, wba_ref[:, cols], preferred_element_type=F32)
        o_ref[:, cols] = y.astype(o_ref.dtype)


def _merge(p, fmix, attn, conv_w, w_bf, w_bc, w_ba, tm, seq_len):
    m = p.shape[0]
    d = D_MODEL
    cw = CONV_W
    tps = seq_len // tm
    blk = lambda width, c: pl.BlockSpec((tm, width), lambda i: (i, c))
    ccp, ccn = _halo_specs(tm, cw, 2, m)
    cxp, cxn = _halo_specs(tm, cw, 3, m)
    gate0 = P_GATE_OFF // d
    full = lambda r, c: pl.BlockSpec((r, c), lambda i: (0, 0))
    return pl.pallas_call(
        functools.partial(_merge_kernel, tps=tps, nchunk=4),
        grid=(m // tm,),
        in_specs=[
            blk(FOURIER_W, 0),
            blk(cw, 1), blk(cw, 2), blk(cw, 3),
            ccp, cxp, ccn, cxn,
            blk(ATTN_V_W, 0),
            pl.BlockSpec((tm, d), lambda i: (i, gate0)),
            pl.BlockSpec((tm, d), lambda i: (i, gate0 + 1)),
            pl.BlockSpec((tm, d), lambda i: (i, gate0 + 2)),
            full(3, cw), full(FOURIER_W, d), full(cw, d), full(ATTN_V_W, d),
        ],
        out_specs=pl.BlockSpec((tm, d), lambda i: (i, 0)),
        out_shape=jax.ShapeDtypeStruct((m, d), BF16),
        scratch_shapes=[pltpu.VMEM((tm + 2 * BF16_SUBLANES, cw), F32)],
        compiler_params=_cparams("arbitrary"),
        name="branch_merge",
    )(fmix, p, p, p, p, p, p, p, attn, p, p, p, conv_w, w_bf, w_bc, w_ba)


def _outproj_kernel(m_ref, w_ref, h_ref, gt_ref, g_ref, sh_ref, sc_ref, ho_ref, xo_ref):
    y = jnp.dot(m_ref[...], w_ref[...], preferred_element_type=F32)
    h = h_ref[...] + gt_ref[...] * y
    ho_ref[...] = h
    xo_ref[...] = (_rms(h, g_ref[...]) * (1.0 + sc_ref[...]) + sh_ref[...]).astype(xo_ref.dtype)


def _out_proj(merged, w_out, h, g2, mods, mod_row, tm):
    m, d = h.shape
    return pl.pallas_call(
        _outproj_kernel,
        grid=(m // tm,),
        in_specs=[
            pl.BlockSpec((tm, d), lambda i: (i, 0)),
            pl.BlockSpec((d, d), lambda i: (0, 0)),
            pl.BlockSpec((tm, d), lambda i: (i, 0)),
            _mod_spec(2, mod_row, tm),
            pl.BlockSpec((1, d), lambda i: (0, 0)),
            _mod_spec(3, mod_row, tm),
            _mod_spec(4, mod_row, tm),
        ],
        out_specs=[pl.BlockSpec((tm, d), lambda i: (i, 0)), pl.BlockSpec((tm, d), lambda i: (i, 0))],
        out_shape=[jax.ShapeDtypeStruct((m, d), F32), jax.ShapeDtypeStruct((m, d), BF16)],
        compiler_params=_cparams("arbitrary"),
        name="out_proj",
    )(merged, w_out, h, mods, g2.reshape(1, d), mods, mods)


def _ffn_up_kernel(x_ref, xp_ref, xn_ref, wa_ref, wv_ref, ca_ref, cv_ref, o_ref,
                   wab_ref, wvb_ref, xh_ref, ua_ref, uv_ref, *, tps):
    i = pl.program_id(1)
    tm = x_ref.shape[0]
    hl = BF16_SUBLANES

    @pl.when(i == 0)
    def _():
        wab_ref[...] = wa_ref[...].astype(BF16)
        wvb_ref[...] = wv_ref[...].astype(BF16)

    first = (i % tps) == 0
    last = (i % tps) == tps - 1
    xh_ref[0:hl, :] = jnp.where(first, jnp.zeros_like(xp_ref[...]), xp_ref[...])
    xh_ref[hl:hl + tm, :] = x_ref[...]
    xh_ref[hl + tm:2 * hl + tm, :] = jnp.where(last, jnp.zeros_like(xn_ref[...]), xn_ref[...])
    xh = xh_ref[...]
    ua_ref[...] = jnp.dot(xh, wab_ref[...], preferred_element_type=F32)
    uv_ref[...] = jnp.dot(xh, wvb_ref[...], preferred_element_type=F32)
    a = _conv3_rows(ua_ref, ca_ref[...], tm)
    v = _conv3_rows(uv_ref, cv_ref[...], tm)
    o_ref[...] = (a * jax.nn.sigmoid(a) * v).astype(o_ref.dtype)


def _ffn_down_kernel(a_ref, w_ref, h_ref, gt_ref, o_ref, wb_ref):
    @pl.when(pl.program_id(1) == 0)
    def _():
        wb_ref[...] = w_ref[...].astype(BF16)

    y = jnp.dot(a_ref[...], wb_ref[...], preferred_element_type=F32)
    o_ref[...] = h_ref[...] + gt_ref[...] * y


def _ffn(xm, w_up, conv_w, w_down, layer, h, mods, mod_row, tm, seq_len):
    m, d = xm.shape
    tf = FFN_TF
    nj = D_FF // tf
    tps = seq_len // tm
    xp, xn = _halo_specs(tm, d, 0, m, row_axis=1)
    act = pl.pallas_call(
        functools.partial(_ffn_up_kernel, tps=tps),
        grid=(nj, m // tm),
        in_specs=[
            pl.BlockSpec((tm, d), lambda j, i: (i, 0)),
            xp, xn,
            pl.BlockSpec((None, d, tf), lambda j, i: (layer, 0, j)),
            pl.BlockSpec((None, d, tf), lambda j, i: (layer, 0, nj + j)),
            pl.BlockSpec((None, 3, tf), lambda j, i: (layer, 0, j)),
            pl.BlockSpec((None, 3, tf), lambda j, i: (layer, 0, nj + j)),
        ],
        out_specs=pl.BlockSpec((tm, tf), lambda j, i: (i, j)),
        out_shape=jax.ShapeDtypeStruct((m, D_FF), BF16),
        scratch_shapes=[
            pltpu.VMEM((d, tf), BF16),
            pltpu.VMEM((d, tf), BF16),
            pltpu.VMEM((tm + 2 * BF16_SUBLANES, d), BF16),
            pltpu.VMEM((tm + 2 * BF16_SUBLANES, tf), F32),
            pltpu.VMEM((tm + 2 * BF16_SUBLANES, tf), F32),
        ],
        compiler_params=_cparams("arbitrary", "arbitrary"),
        name="ffn_up",
    )(xm, xm, xm, w_up, w_up, conv_w, conv_w)

    tn = FFN_DOWN_TN
    tmd = min(m, FFN_DOWN_TM)
    gate_row = lambda n, i: mod_row(i * tmd)
    return pl.pallas_call(
        _ffn_down_kernel,
        grid=(d // tn, m // tmd),
        in_specs=[
            pl.BlockSpec((tmd, D_FF), lambda n, i: (i, 0)),
            pl.BlockSpec((None, D_FF, tn), lambda n, i: (layer, 0, n)),
            pl.BlockSpec((tmd, tn), lambda n, i: (i, n)),
            pl.BlockSpec((None, None, 1, tn), lambda n, i: (gate_row(n, i), 5, 0, n)),
        ],
        out_specs=pl.BlockSpec((tmd, tn), lambda n, i: (i, n)),
        out_shape=jax.ShapeDtypeStruct((m, d), F32),
        scratch_shapes=[pltpu.VMEM((D_FF, tn), BF16)],
        compiler_params=_cparams("arbitrary", "arbitrary"),
        name="ffn_down",
    )(act, w_down, h, mods)


def _rope_tables(length):
    n_freq = HEAD_DIM // 4
    pos = jnp.arange(length)
    row = (pos // GRID_W).astype(F32)
    col = (pos % GRID_W).astype(F32)
    inv = ROPE_THETA ** (-(2.0 * jnp.arange(n_freq, dtype=F32)) / (HEAD_DIM // 2))
    ang_r, ang_c = row[:, None] * inv, col[:, None] * inv
    ang = jnp.concatenate([ang_r, ang_r, ang_c, ang_c], axis=1)
    cos = jnp.tile(jnp.cos(ang), (1, 2))
    sin = jnp.tile(jnp.sin(ang), (1, 2))
    is_x1 = (jnp.arange(LANES) % (2 * n_freq)) < n_freq
    return cos, jnp.where(is_x1, -sin, 0.0), jnp.where(is_x1, 0.0, sin)


def _dft_mats(n):
    r = math.isqrt(n)
    assert r * r == n
    k = jnp.arange(n, dtype=jnp.int32)
    s = jnp.arange(r, dtype=jnp.int32)
    ang_a = ((k[:, None] * s[None, :]) % r).astype(F32) * (2.0 * math.pi / r)
    ang_b = ((k[:, None] * s[None, :]) % n).astype(F32) * (2.0 * math.pi / n)
    ca, sa = jnp.cos(ang_a)[:, :, None], jnp.sin(ang_a)[:, :, None]
    cb, sb = jnp.cos(ang_b)[:, None, :], jnp.sin(ang_b)[:, None, :]
    cos = (ca * cb - sa * sb).reshape(n, n)
    sin = (sa * cb + ca * sb).reshape(n, n)
    return cos.astype(BF16), (-sin).astype(BF16)


def _channel_dft():
    c = np.arange(FOURIER_GROUP_W)
    ang = 2.0 * np.pi * ((c[:, None] * c[None, :]) % FOURIER_GROUP_W) / FOURIER_GROUP_W
    return jnp.asarray(np.concatenate([np.cos(ang), np.sin(ang)], axis=1), dtype=BF16)


def _mixer(h, xm, seq_len, rope_tabs, dft, ctx_kv, w, mods, mod_row, lam_init, tm, tq,
           proj_cols=(0, PROJ_TILES)):
    m = h.shape[0]
    b = m // seq_len
    p = _in_proj(xm, w["w_in"], w["layer"], rope_tabs, min(m, 1024), proj_cols[0], proj_cols[1], seq_len)
    if proj_cols[1] != PROJ_TILES:
        return None, p
    p3 = p.reshape(b, seq_len, N_IN)
    fmix = _fourier(p3, w["cs"], dft[0], dft[1], min(seq_len, 512)).reshape(m, FOURIER_W)
    if ctx_kv is None:
        attn = _attention_flash(p3, P_Q_OFF // LANES, p3, P_K_OFF // LANES, P_V_OFF // LANES, None,
                          w["lambdas"], w["subln_g"], lam_init, tq)
    else:
        c3, kcblk, vcblk = ctx_kv
        attn = _attention_flash(p3, P_Q_OFF // LANES, c3, kcblk, vcblk, p3,
                          w["lambdas"], w["subln_g"], lam_init, tq)
    merged = _merge(p, fmix, attn.reshape(m, ATTN_V_W), w["conv_mix_w"], w["w_br_fourier"],
                    w["w_br_conv"], w["w_br_attn"], tm, seq_len)
    h_mid, xm2 = _out_proj(merged, w["w_out"], h, w["g_norm2"], mods, mod_row, min(tm, 256))
    h_new = _ffn(xm2, w["w_ffn_up"], w["ffn_conv_w"], w["w_ffn_down"], w["layer"], h_mid, mods, mod_row,
                 min(seq_len, FFN_UP_TM), seq_len)
    return h_new, p


def kernel(x, c, ctx, c_ctx, w_mod, b_mod, g_norm1, g_norm2, w_in, conv_mix_w, lambdas, subln_g,
           w_br_fourier, w_br_conv, w_br_attn, w_out, w_ffn_up, ffn_conv_w, w_ffn_down, g_final):
    b, seq, d = x.shape
    n_ctx = ctx.shape[1]
    depth = w_mod.shape[0]
    assert d == D_MODEL and b + 1 <= MOD_ROWS

    wb = lambda a: a.astype(BF16)
    w_bf, w_bc, w_ba, w_o = wb(w_br_fourier), wb(w_br_conv), wb(w_br_attn), wb(w_out)

    rope_tabs = _rope_tables(seq)
    dft_l = _dft_mats(seq)
    dft_c = _dft_mats(n_ctx)
    cs = _channel_dft()

    cvec = jnp.zeros((MOD_ROWS, d), F32).at[:b].set(c).at[b].set(c_ctx)
    mods_all = _modulation(cvec, w_mod, b_mod).reshape(depth, MOD_ROWS, 6, 1, d)

    tm_l = min(512, seq)
    tm_c = n_ctx
    tq = min(512, seq)
    lat_row = lambda r: r // seq
    ctx_row = lambda r: b

    h = x.reshape(b * seq, d)
    hc = ctx.reshape(b * n_ctx, d)
    for i in range(depth):
        last = i == depth - 1
        lam_init = 0.8 - 0.6 * math.exp(-0.3 * i)
        mods = mods_all[i]
        w = dict(w_in=w_in, layer=i, cs=cs, lambdas=lambdas[i], subln_g=subln_g[i], conv_mix_w=conv_mix_w[i],
                 w_br_fourier=w_bf[i], w_br_conv=w_bc[i], w_br_attn=w_ba[i], w_out=w_o[i],
                 g_norm2=g_norm2[i], w_ffn_up=w_ffn_up, ffn_conv_w=ffn_conv_w, w_ffn_down=w_ffn_down)

        xc = _norm_modulate(hc, g_norm1[i], mods, ctx_row, tm_c)
        xl = _norm_modulate(h, g_norm1[i], mods, lat_row, tm_l)
        if last:
            _, pc = _mixer(hc, xc, n_ctx, None, dft_c, None, w, mods, ctx_row, lam_init, tm_c, n_ctx,
                           proj_cols=(PROJ_K_TILE, 2))
            ctx_kv = (pc.reshape(b, n_ctx, 2 * PROJ_TN), 0, PROJ_TN // LANES)
            hc_new = hc
        else:
            hc_new, pc = _mixer(hc, xc, n_ctx, None, dft_c, None, w, mods, ctx_row, lam_init, tm_c, n_ctx)
            ctx_kv = (pc.reshape(b, n_ctx, N_IN), P_K_OFF // LANES, P_V_OFF // LANES)
        h, _ = _mixer(h, xl, seq, rope_tabs, dft_l, ctx_kv, w, mods, lat_row, lam_init, tm_l, tq)
        hc = hc_new
    return _final_norm(h, g_final, tm_l).reshape(b, seq, d)
```

```python
import functools
import math

import numpy as np
import jax
import jax.numpy as jnp
from jax import lax
from jax.experimental import pallas as pl
from jax.experimental.pallas import tpu as pltpu

F32 = jnp.float32
BF16 = jnp.bfloat16

D_MODEL = 2048
GRID_W = 64
FOURIER_GROUPS = 4
FOURIER_GROUP_W = D_MODEL // 16
FOURIER_W = FOURIER_GROUPS * FOURIER_GROUP_W
CONV_W = D_MODEL // 4
N_HEADS = 8
HEAD_DIM = D_MODEL // (4 * N_HEADS)
VAL_DIM = 2 * HEAD_DIM
ATTN_QK_W = N_HEADS * 2 * HEAD_DIM
ATTN_V_W = N_HEADS * VAL_DIM
ROPE_THETA = 10000.0
ATTN_SCALE = HEAD_DIM ** -0.5
SUBLN_EPS = 1e-5
Q_OFF = FOURIER_W + 3 * CONV_W
K_OFF = Q_OFF + ATTN_QK_W
V_OFF = K_OFF + ATTN_QK_W
V_END = V_OFF + ATTN_V_W
N_IN = V_END + 3 * D_MODEL
D_FF = ((8 * D_MODEL // 3 + 255) // 256) * 256
EPS = 1e-6

LANES = 128
MXU_COLS = 256
BF16_SUBLANES = 16
VMEM_LIMIT = 56 * 1024 * 1024
MOD_ROWS = 8

PROJ_TN = 1024
P_GATE_OFF = Q_OFF
P_Q_OFF = P_GATE_OFF + 3 * D_MODEL
P_K_OFF = P_Q_OFF + ATTN_QK_W
P_V_OFF = P_K_OFF + ATTN_QK_W
PROJ_GATE_TILE = P_GATE_OFF // PROJ_TN
PROJ_Q_TILE = P_Q_OFF // PROJ_TN
PROJ_K_TILE = P_K_OFF // PROJ_TN
PROJ_TILES = N_IN // PROJ_TN
ATTN_TK = 256
FFN_TF = 512
FFN_UP_TM = 1024
FFN_DOWN_TN = 512
FFN_DOWN_TM = 512


def _cparams(*sem):
    return pltpu.CompilerParams(dimension_semantics=sem, vmem_limit_bytes=VMEM_LIMIT)


def _mod_kernel(c_ref, w_ref, b_ref, o_ref):
    c = c_ref[...]
    s = c * jax.nn.sigmoid(c)
    s_hi = s.astype(BF16)
    s_lo = (s - s_hi.astype(F32)).astype(BF16)
    w = w_ref[...]
    w_hi = w.astype(BF16)
    w_lo = (w - w_hi.astype(F32)).astype(BF16)
    y = jnp.dot(s_hi, w_hi, preferred_element_type=F32)
    y += jnp.dot(s_lo, w_hi, preferred_element_type=F32)
    y += jnp.dot(s_hi, w_lo, preferred_element_type=F32)
    o_ref[...] = y + b_ref[...]


def _modulation(cvec, w_mod, b_mod):
    depth, d, n = w_mod.shape
    tn = 1024
    return pl.pallas_call(
        _mod_kernel,
        grid=(depth, n // tn),
        in_specs=[
            pl.BlockSpec((MOD_ROWS, d), lambda l, j: (0, 0)),
            pl.BlockSpec((None, d, tn), lambda l, j: (l, 0, j)),
            pl.BlockSpec((None, 1, tn), lambda l, j: (l, 0, j)),
        ],
        out_specs=pl.BlockSpec((None, MOD_ROWS, tn), lambda l, j: (l, 0, j)),
        out_shape=jax.ShapeDtypeStruct((depth, MOD_ROWS, n), F32),
        compiler_params=_cparams("arbitrary", "arbitrary"),
        name="modulation",
    )(cvec, w_mod, b_mod.reshape(depth, 1, n))


def _mod_spec(which, mod_row, tm):
    return pl.BlockSpec((None, None, 1, D_MODEL), lambda i, *_: (mod_row(i * tm), which, 0, 0))


def _rms(x, g):
    return x * lax.rsqrt(jnp.mean(x * x, axis=-1, keepdims=True) + EPS) * g


def _norm_mod_kernel(h_ref, g_ref, sh_ref, sc_ref, o_ref):
    y = _rms(h_ref[...], g_ref[...])
    o_ref[...] = (y * (1.0 + sc_ref[...]) + sh_ref[...]).astype(o_ref.dtype)


def _norm_kernel(h_ref, g_ref, o_ref):
    o_ref[...] = _rms(h_ref[...], g_ref[...]).astype(o_ref.dtype)


def _norm_modulate(h, g, mods, mod_row, tm):
    m, d = h.shape
    return pl.pallas_call(
        _norm_mod_kernel,
        grid=(m // tm,),
        in_specs=[
            pl.BlockSpec((tm, d), lambda i: (i, 0)),
            pl.BlockSpec((1, d), lambda i: (0, 0)),
            _mod_spec(0, mod_row, tm),
            _mod_spec(1, mod_row, tm),
        ],
        out_specs=pl.BlockSpec((tm, d), lambda i: (i, 0)),
        out_shape=jax.ShapeDtypeStruct((m, d), BF16),
        compiler_params=_cparams("arbitrary"),
        name="norm_modulate",
    )(h, g.reshape(1, d), mods, mods)


def _final_norm(h, g, tm):
    m, d = h.shape
    return pl.pallas_call(
        _norm_kernel,
        grid=(m // tm,),
        in_specs=[pl.BlockSpec((tm, d), lambda i: (i, 0)), pl.BlockSpec((1, d), lambda i: (0, 0))],
        out_specs=pl.BlockSpec((tm, d), lambda i: (i, 0)),
        out_shape=jax.ShapeDtypeStruct((m, d), F32),
        compiler_params=_cparams("arbitrary"),
        name="final_norm",
    )(h, g.reshape(1, d))


def _proj_kernel(*refs, col0, rope):
    if rope:
        x_ref, w_ref, cos_ref, sin_up_ref, sin_dn_ref, o_ref, wb_ref = refs
    else:
        x_ref, w_ref, o_ref, wb_ref = refs
    j = pl.program_id(0) + col0

    @pl.when(pl.program_id(1) == 0)
    def _():
        wb_ref[...] = w_ref[...].astype(BF16)

    is_q = j == PROJ_Q_TILE
    is_k = j == PROJ_K_TILE
    is_gate = (j >= PROJ_GATE_TILE) & (j < PROJ_Q_TILE)
    x = x_ref[...]

    def column_chunks(epilogue):
        for c in range(PROJ_TN // MXU_COLS):
            cols = slice(c * MXU_COLS, (c + 1) * MXU_COLS)
            acc = jnp.dot(x, wb_ref[:, cols], preferred_element_type=F32)
            o_ref[:, cols] = epilogue(acc).astype(o_ref.dtype)

    def rotary(acc):
        qscale = jnp.where(is_q, ATTN_SCALE * math.log2(math.e), 1.0)
        half = HEAD_DIM // 4
        out = []
        for hd in range(MXU_COLS // LANES):
            a = acc[:, hd * LANES:(hd + 1) * LANES]
            if rope:
                a = (a * cos_ref[...] + pltpu.roll(a, LANES - half, 1) * sin_up_ref[...]
                     + pltpu.roll(a, half, 1) * sin_dn_ref[...])
            out.append(a * qscale)
        return jnp.concatenate(out, axis=1)

    pl.when(is_q | is_k)(lambda: column_chunks(rotary))
    pl.when(is_gate)(lambda: column_chunks(jax.nn.sigmoid))
    pl.when(jnp.logical_not(is_q | is_k | is_gate))(lambda: column_chunks(lambda acc: acc))


def _in_proj(xm, w_in, layer, rope_tabs, tm, col0, ncols, seq_len):
    m, d = xm.shape
    tn = PROJ_TN
    n_plain = PROJ_GATE_TILE
    n_gate = PROJ_Q_TILE - PROJ_GATE_TILE
    n_qkv = PROJ_TILES - PROJ_Q_TILE

    def w_tile(j):
        j = j + col0
        return jnp.where(j < n_plain, j, jnp.where(j < n_plain + n_gate, j + n_qkv, j - n_gate))

    in_specs = [
        pl.BlockSpec((tm, d), lambda j, i: (i, 0)),
        pl.BlockSpec((None, d, tn), lambda j, i: (layer, 0, w_tile(j))),
    ]
    args = [xm, w_in]
    rope = rope_tabs is not None
    if rope:
        tps = seq_len // tm
        in_specs += [pl.BlockSpec((tm, LANES), lambda j, i: (i % tps, 0))] * 3
        args += list(rope_tabs)
    return pl.pallas_call(
        functools.partial(_proj_kernel, col0=col0, rope=rope),
        grid=(ncols, m // tm),
        in_specs=in_specs,
        out_specs=pl.BlockSpec((tm, tn), lambda j, i: (i, j)),
        out_shape=jax.ShapeDtypeStruct((m, ncols * tn), BF16),
        scratch_shapes=[pltpu.VMEM((d, tn), BF16)],
        compiler_params=_cparams("arbitrary", "arbitrary"),
        name="in_proj",
    )(*args)


def _fourier_kernel(f_ref, cs_ref, cl_ref, sl_ref, o_ref, uc_ref, us_ref, *, chunk, scale):
    i = pl.program_id(1)
    seq = f_ref.shape[0]
    gw = FOURIER_GROUP_W

    @pl.when(i == 0)
    def _():
        for r in range(seq // chunk):
            rows = slice(r * chunk, (r + 1) * chunk)
            for g in range(FOURIER_GROUPS):
                cols = slice(g * gw, (g + 1) * gw)
                y = jnp.dot(f_ref[rows, cols], cs_ref[...], preferred_element_type=F32)
                uc_ref[rows, cols] = y[:, :gw].astype(BF16)
                us_ref[rows, cols] = y[:, gw:].astype(BF16)

    y = jnp.dot(cl_ref[...], uc_ref[...], preferred_element_type=F32)
    y += jnp.dot(sl_ref[...], us_ref[...], preferred_element_type=F32)
    o_ref[...] = (y * scale).astype(o_ref.dtype)


def _fourier(p3, cs, cl, sln, tr):
    b, seq, _ = p3.shape
    chunk = min(seq, 1024)
    scale = 1.0 / math.sqrt(seq * FOURIER_GROUP_W)
    return pl.pallas_call(
        functools.partial(_fourier_kernel, chunk=chunk, scale=scale),
        grid=(b, seq // tr),
        in_specs=[
            pl.BlockSpec((None, seq, FOURIER_W), lambda bi, i: (bi, 0, 0)),
            pl.BlockSpec((FOURIER_GROUP_W, 2 * FOURIER_GROUP_W), lambda bi, i: (0, 0)),
            pl.BlockSpec((tr, seq), lambda bi, i: (i, 0)),
            pl.BlockSpec((tr, seq), lambda bi, i: (i, 0)),
        ],
        out_specs=pl.BlockSpec((None, tr, FOURIER_W), lambda bi, i: (bi, i, 0)),
        out_shape=jax.ShapeDtypeStruct((b, seq, FOURIER_W), BF16),
        scratch_shapes=[pltpu.VMEM((seq, FOURIER_W), BF16), pltpu.VMEM((seq, FOURIER_W), BF16)],
        compiler_params=_cparams("arbitrary", "arbitrary"),
        name="fourier",
    )(p3, cs, cl, sln)


def _attn_kernel(*refs, tq, tk, n_lat, lam_init):
    if n_lat:
        lam_ref, q_ref, kc_ref, vc_ref, kl_ref, vl_ref, g_ref, o_ref = refs
    else:
        lam_ref, q_ref, kc_ref, vc_ref, g_ref, o_ref = refs
        kl_ref = vl_ref = None
    q = q_ref[...]
    lane = lax.broadcasted_iota(jnp.int32, (1, LANES), 1)
    comp0 = lane < HEAD_DIM
    zero = jnp.zeros_like(q)
    qq_t = jnp.concatenate([jnp.where(comp0, q, zero), jnp.where(comp0, zero, q)], axis=0).T
    n_ctx = kc_ref.shape[0]
    chunks = [(kc_ref, vc_ref, 0, n_ctx)] + [(kl_ref, vl_ref, c * tk, tk) for c in range(n_lat)]

    lane_tiles = 2 * tq // MXU_COLS
    items = [(c, n) for c in range(len(chunks)) for n in range(lane_tiles)]

    def scores(c, n):
        kr, _, r0, sz = chunks[c]
        return jnp.dot(kr[r0:r0 + sz, :], qq_t[:, n * MXU_COLS:(n + 1) * MXU_COLS],
                       preferred_element_type=F32)

    m = [None] * lane_tiles
    l = [None] * lane_tiles
    acc = [None] * lane_tiles
    s_next = scores(*items[0])
    for idx, (c, n) in enumerate(items):
        _, vr, r0, sz = chunks[c]
        s_t = s_next
        if idx + 1 < len(items):
            s_next = scores(*items[idx + 1])
        mc = jnp.max(s_t, axis=0, keepdims=True)
        m_new = mc if m[n] is None else jnp.maximum(m[n], mc)
        p_t = jnp.exp2(s_t - m_new).astype(BF16)
        lc = jnp.dot(jnp.ones((BF16_SUBLANES, sz), BF16), p_t, preferred_element_type=F32)[0:1]
        y = lax.dot_general(vr[r0:r0 + sz, :], p_t, (((0,), (0,)), ((), ())),
                            preferred_element_type=F32)
        if m[n] is None:
            l[n], acc[n] = lc, y
        else:
            alpha = jnp.exp2(m[n] - m_new)
            l[n] = alpha * l[n] + lc
            acc[n] = alpha * acc[n] + y
        m[n] = m_new
    o = jnp.concatenate([acc[n] / l[n] for n in range(lane_tiles)], axis=1)
    lf = lam_ref[...]
    lam = (jnp.exp(jnp.sum(lf[0:1] * lf[1:2], axis=-1, keepdims=True))
           - jnp.exp(jnp.sum(lf[2:3] * lf[3:4], axis=-1, keepdims=True)) + lam_init)
    o = o[:, :tq] - lam * o[:, tq:]
    o = o * lax.rsqrt(jnp.mean(o * o, axis=0, keepdims=True) + SUBLN_EPS)
    o_ref[...] = (o * g_ref[...] * (1.0 - lam_init)).T.astype(o_ref.dtype)


def _attention(q3, qblk, c3, kcblk, vcblk, l3, lam, subln_g, lam_init, tq):
    b, lq, _ = q3.shape
    n_ctx = c3.shape[1]
    tk = ATTN_TK
    in_specs = [
        pl.BlockSpec((4, HEAD_DIM), lambda bi, h, i: (0, 0)),
        pl.BlockSpec((None, tq, LANES), lambda bi, h, i: (bi, i, qblk + h)),
        pl.BlockSpec((None, n_ctx, LANES), lambda bi, h, i: (bi, 0, kcblk + h)),
        pl.BlockSpec((None, n_ctx, LANES), lambda bi, h, i: (bi, 0, vcblk + h)),
    ]
    args = [lam, q3, c3, c3]
    n_lat = 0
    if l3 is not None:
        seq = l3.shape[1]
        n_lat = seq // min(tk, seq)
        tk = seq // n_lat
        in_specs += [
            pl.BlockSpec((None, seq, LANES), lambda bi, h, i: (bi, 0, P_K_OFF // LANES + h)),
            pl.BlockSpec((None, seq, LANES), lambda bi, h, i: (bi, 0, P_V_OFF // LANES + h)),
        ]
        args += [l3, l3]
    in_specs.append(pl.BlockSpec((VAL_DIM, 1), lambda bi, h, i: (0, 0)))
    args.append(subln_g.reshape(VAL_DIM, 1))
    return pl.pallas_call(
        functools.partial(_attn_kernel, tq=tq, tk=tk, n_lat=n_lat, lam_init=lam_init),
        grid=(b, N_HEADS, lq // tq),
        in_specs=in_specs,
        out_specs=pl.BlockSpec((None, tq, LANES), lambda bi, h, i: (bi, i, h)),
        out_shape=jax.ShapeDtypeStruct((b, lq, ATTN_V_W), BF16),
        compiler_params=_cparams("arbitrary", "arbitrary", "arbitrary"),
        name="diff_attention",
    )(*args)


def _conv3_rows(buf_ref, w, tm):
    h = BF16_SUBLANES
    return (buf_ref[h - 1:h - 1 + tm, :] * w[0:1] + buf_ref[h:h + tm, :] * w[1:2]
            + buf_ref[h + 1:h + 1 + tm, :] * w[2:3])


def _halo_specs(tm, width, colblk, m, row_axis=0):
    per = tm // BF16_SUBLANES
    last = m // BF16_SUBLANES - 1
    prev = pl.BlockSpec((BF16_SUBLANES, width),
                        lambda *ids: (jnp.maximum(ids[row_axis] * per - 1, 0), colblk))
    nxt = pl.BlockSpec((BF16_SUBLANES, width),
                       lambda *ids: (jnp.minimum((ids[row_axis] + 1) * per, last), colblk))
    return prev, nxt


def _merge_kernel(f_ref, cb_ref, cc_ref, cx_ref, ccp_ref, cxp_ref, ccn_ref, cxn_ref, a_ref,
                  gf_ref, gc_ref, ga_ref, cw_ref, wbf_ref, wbc_ref, wba_ref, o_ref, z_ref, *, tps, nchunk):
    i = pl.program_id(0)
    tm = cc_ref.shape[0]
    h = BF16_SUBLANES
    first = (i % tps) == 0
    last = (i % tps) == tps - 1
    zp = ccp_ref[...].astype(F32) * cxp_ref[...].astype(F32)
    zn = ccn_ref[...].astype(F32) * cxn_ref[...].astype(F32)
    z_ref[0:h, :] = jnp.where(first, 0.0, zp)
    z_ref[h:h + tm, :] = cc_ref[...].astype(F32) * cx_ref[...].astype(F32)
    z_ref[h + tm:2 * h + tm, :] = jnp.where(last, 0.0, zn)
    cv = (cb_ref[...].astype(F32) * _conv3_rows(z_ref, cw_ref[...], tm)).astype(BF16)
    f = f_ref[...]
    a = a_ref[...]
    cn = D_MODEL // nchunk
    for n in range(nchunk):
        cols = slice(n * cn, (n + 1) * cn)
        y = gf_ref[:, cols].astype(F32) * jnp.dot(f, wbf_ref[:, cols], preferred_element_type=F32)
        y += gc_ref[:, cols].astype(F32) * jnp.dot(cv, wbc_ref[:, cols], preferred_element_type=F32)
        y += ga_ref[:, cols].astype(F32) * jnp.dot(a, wba_ref[:, cols], preferred_element_type=F32)
        o_ref[:, cols] = y.astype(o_ref.dtype)


def _merge(p, fmix, attn, conv_w, w_bf, w_bc, w_ba, tm, seq_len):
    m = p.shape[0]
    d = D_MODEL
    cw = CONV_W
    tps = seq_len // tm
    blk = lambda width, c: pl.BlockSpec((tm, width), lambda i: (i, c))
    ccp, ccn = _halo_specs(tm, cw, 2, m)
    cxp, cxn = _halo_specs(tm, cw, 3, m)
    gate0 = P_GATE_OFF // d
    full = lambda r, c: pl.BlockSpec((r, c), lambda i: (0, 0))
    return pl.pallas_call(
        functools.partial(_merge_kernel, tps=tps, nchunk=4),
        grid=(m // tm,),
        in_specs=[
            blk(FOURIER_W, 0),
            blk(cw, 1), blk(cw, 2), blk(cw, 3),
            ccp, cxp, ccn, cxn,
            blk(ATTN_V_W, 0),
            pl.BlockSpec((tm, d), lambda i: (i, gate0)),
            pl.BlockSpec((tm, d), lambda i: (i, gate0 + 1)),
            pl.BlockSpec((tm, d), lambda i: (i, gate0 + 2)),
            full(3, cw), full(FOURIER_W, d), full(cw, d), full(ATTN_V_W, d),
        ],
        out_specs=pl.BlockSpec((tm, d), lambda i: (i, 0)),
        out_shape=jax.ShapeDtypeStruct((m, d), BF16),
        scratch_shapes=[pltpu.VMEM((tm + 2 * BF16_SUBLANES, cw), F32)],
        compiler_params=_cparams("arbitrary"),
        name="branch_merge",
    )(fmix, p, p, p, p, p, p, p, attn, p, p, p, conv_w, w_bf, w_bc, w_ba)


def _outproj_kernel(m_ref, w_ref, h_ref, gt_ref, g_ref, sh_ref, sc_ref, ho_ref, xo_ref):
    y = jnp.dot(m_ref[...], w_ref[...], preferred_element_type=F32)
    h = h_ref[...] + gt_ref[...] * y
    ho_ref[...] = h
    xo_ref[...] = (_rms(h, g_ref[...]) * (1.0 + sc_ref[...]) + sh_ref[...]).astype(xo_ref.dtype)


def _out_proj(merged, w_out, h, g2, mods, mod_row, tm):
    m, d = h.shape
    return pl.pallas_call(
        _outproj_kernel,
        grid=(m // tm,),
        in_specs=[
            pl.BlockSpec((tm, d), lambda i: (i, 0)),
            pl.BlockSpec((d, d), lambda i: (0, 0)),
            pl.BlockSpec((tm, d), lambda i: (i, 0)),
            _mod_spec(2, mod_row, tm),
            pl.BlockSpec((1, d), lambda i: (0, 0)),
            _mod_spec(3, mod_row, tm),
            _mod_spec(4, mod_row, tm),
        ],
        out_specs=[pl.BlockSpec((tm, d), lambda i: (i, 0)), pl.BlockSpec((tm, d), lambda i: (i, 0))],
        out_shape=[jax.ShapeDtypeStruct((m, d), F32), jax.ShapeDtypeStruct((m, d), BF16)],
        compiler_params=_cparams("arbitrary"),
        name="out_proj",
    )(merged, w_out, h, mods, g2.reshape(1, d), mods, mods)


def _ffn_up_kernel(x_ref, xp_ref, xn_ref, wa_ref, wv_ref, ca_ref, cv_ref, o_ref,
                   wab_ref, wvb_ref, xh_ref, ua_ref, uv_ref, *, tps):
    i = pl.program_id(1)
    tm = x_ref.shape[0]
    hl = BF16_SUBLANES

    @pl.when(i == 0)
    def _():
        wab_ref[...] = wa_ref[...].astype(BF16)
        wvb_ref[...] = wv_ref[...].astype(BF16)

    first = (i % tps) == 0
    last = (i % tps) == tps - 1
    xh_ref[0:hl, :] = jnp.where(first, jnp.zeros_like(xp_ref[...]), xp_ref[...])
    xh_ref[hl:hl + tm, :] = x_ref[...]
    xh_ref[hl + tm:2 * hl + tm, :] = jnp.where(last, jnp.zeros_like(xn_ref[...]), xn_ref[...])
    xh = xh_ref[...]
    ua_ref[...] = jnp.dot(xh, wab_ref[...], preferred_element_type=F32)
    uv_ref[...] = jnp.dot(xh, wvb_ref[...], preferred_element_type=F32)
    a = _conv3_rows(ua_ref, ca_ref[...], tm)
    v = _conv3_rows(uv_ref, cv_ref[...], tm)
    o_ref[...] = (a * jax.nn.sigmoid(a) * v).astype(o_ref.dtype)


def _ffn_down_kernel(a_ref, w_ref, h_ref, gt_ref, o_ref, wb_ref):
    @pl.when(pl.program_id(1) == 0)
    def _():
        wb_ref[...] = w_ref[...].astype(BF16)

    y = jnp.dot(a_ref[...], wb_ref[...], preferred_element_type=F32)
    o_ref[...] = h_ref[...] + gt_ref[...] * y


def _ffn(xm, w_up, conv_w, w_down, layer, h, mods, mod_row, tm, seq_len):
    m, d = xm.shape
    tf = FFN_TF
    nj = D_FF // tf
    tps = seq_len // tm
    xp, xn = _halo_specs(tm, d, 0, m, row_axis=1)
    act = pl.pallas_call(
        functools.partial(_ffn_up_kernel, tps=tps),
        grid=(nj, m // tm),
        in_specs=[
            pl.BlockSpec((tm, d), lambda j, i: (i, 0)),
            xp, xn,
            pl.BlockSpec((None, d, tf), lambda j, i: (layer, 0, j)),
            pl.BlockSpec((None, d, tf), lambda j, i: (layer, 0, nj + j)),
            pl.BlockSpec((None, 3, tf), lambda j, i: (layer, 0, j)),
            pl.BlockSpec((None, 3, tf), lambda j, i: (layer, 0, nj + j)),
        ],
        out_specs=pl.BlockSpec((tm, tf), lambda j, i: (i, j)),
        out_shape=jax.ShapeDtypeStruct((m, D_FF), BF16),
        scratch_shapes=[
            pltpu.VMEM((d, tf), BF16),
            pltpu.VMEM((d, tf), BF16),
            pltpu.VMEM((tm + 2 * BF16_SUBLANES, d), BF16),
            pltpu.VMEM((tm + 2 * BF16_SUBLANES, tf), F32),
            pltpu.VMEM((tm + 2 * BF16_SUBLANES, tf), F32),
        ],
        compiler_params=_cparams("arbitrary", "arbitrary"),
        name="ffn_up",
    )(xm, xm, xm, w_up, w_up, conv_w, conv_w)

    tn = FFN_DOWN_TN
    tmd = min(m, FFN_DOWN_TM)
    gate_row = lambda n, i: mod_row(i * tmd)
    return pl.pallas_call(
        _ffn_down_kernel,
        grid=(d // tn, m // tmd),
        in_specs=[
            pl.BlockSpec((tmd, D_FF), lambda n, i: (i, 0)),
            pl.BlockSpec((None, D_FF, tn), lambda n, i: (layer, 0, n)),
            pl.BlockSpec((tmd, tn), lambda n, i: (i, n)),
            pl.BlockSpec((None, None, 1, tn), lambda n, i: (gate_row(n, i), 5, 0, n)),
        ],
        out_specs=pl.BlockSpec((tmd, tn), lambda n, i: (i, n)),
        out_shape=jax.ShapeDtypeStruct((m, d), F32),
        scratch_shapes=[pltpu.VMEM((D_FF, tn), BF16)],
        compiler_params=_cparams("arbitrary", "arbitrary"),
        name="ffn_down",
    )(act, w_down, h, mods)


def _rope_tables(length):
    n_freq = HEAD_DIM // 4
    pos = jnp.arange(length)
    row = (pos // GRID_W).astype(F32)
    col = (pos % GRID_W).astype(F32)
    inv = ROPE_THETA ** (-(2.0 * jnp.arange(n_freq, dtype=F32)) / (HEAD_DIM // 2))
    ang_r, ang_c = row[:, None] * inv, col[:, None] * inv
    ang = jnp.concatenate([ang_r, ang_r, ang_c, ang_c], axis=1)
    cos = jnp.tile(jnp.cos(ang), (1, 2))
    sin = jnp.tile(jnp.sin(ang), (1, 2))
    is_x1 = (jnp.arange(LANES) % (2 * n_freq)) < n_freq
    return cos, jnp.where(is_x1, -sin, 0.0), jnp.where(is_x1, 0.0, sin)


def _dft_mats(n):
    r = math.isqrt(n)
    assert r * r == n
    k = jnp.arange(n, dtype=jnp.int32)
    s = jnp.arange(r, dtype=jnp.int32)
    ang_a = ((k[:, None] * s[None, :]) % r).astype(F32) * (2.0 * math.pi / r)
    ang_b = ((k[:, None] * s[None, :]) % n).astype(F32) * (2.0 * math.pi / n)
    ca, sa = jnp.cos(ang_a)[:, :, None], jnp.sin(ang_a)[:, :, None]
    cb, sb = jnp.cos(ang_b)[:, None, :], jnp.sin(ang_b)[:, None, :]
    cos = (ca * cb - sa * sb).reshape(n, n)
    sin = (sa * cb + ca * sb).reshape(n, n)
    return cos.astype(BF16), (-sin).astype(BF16)


def _channel_dft():
    c = np.arange(FOURIER_GROUP_W)
    ang = 2.0 * np.pi * ((c[:, None] * c[None, :]) % FOURIER_GROUP_W) / FOURIER_GROUP_W
    return jnp.asarray(np.concatenate([np.cos(ang), np.sin(ang)], axis=1), dtype=BF16)


def _mixer(h, xm, seq_len, rope_tabs, dft, ctx_kv, w, mods, mod_row, lam_init, tm, tq,
           proj_cols=(0, PROJ_TILES)):
    m = h.shape[0]
    b = m // seq_len
    p = _in_proj(xm, w["w_in"], w["layer"], rope_tabs, min(m, 1024), proj_cols[0], proj_cols[1], seq_len)
    if proj_cols[1] != PROJ_TILES:
        return None, p
    p3 = p.reshape(b, seq_len, N_IN)
    fmix = _fourier(p3, w["cs"], dft[0], dft[1], min(seq_len, 512)).reshape(m, FOURIER_W)
    if ctx_kv is None:
        attn = _attention(p3, P_Q_OFF // LANES, p3, P_K_OFF // LANES, P_V_OFF // LANES, None,
                          w["lambdas"], w["subln_g"], lam_init, tq)
    else:
        c3, kcblk, vcblk = ctx_kv
        attn = _attention(p3, P_Q_OFF // LANES, c3, kcblk, vcblk, p3,
                          w["lambdas"], w["subln_g"], lam_init, tq)
    merged = _merge(p, fmix, attn.reshape(m, ATTN_V_W), w["conv_mix_w"], w["w_br_fourier"],
                    w["w_br_conv"], w["w_br_attn"], tm, seq_len)
    h_mid, xm2 = _out_proj(merged, w["w_out"], h, w["g_norm2"], mods, mod_row, min(tm, 256))
    h_new = _ffn(xm2, w["w_ffn_up"], w["ffn_conv_w"], w["w_ffn_down"], w["layer"], h_mid, mods, mod_row,
                 min(seq_len, FFN_UP_TM), seq_len)
    return h_new, p


def kernel(x, c, ctx, c_ctx, w_mod, b_mod, g_norm1, g_norm2, w_in, conv_mix_w, lambdas, subln_g,
           w_br_fourier, w_br_conv, w_br_attn, w_out, w_ffn_up, ffn_conv_w, w_ffn_down, g_final):
    b, seq, d = x.shape
    n_ctx = ctx.shape[1]
    depth = w_mod.shape[0]
    assert d == D_MODEL and b + 1 <= MOD_ROWS

    wb = lambda a: a.astype(BF16)
    w_bf, w_bc, w_ba, w_o = wb(w_br_fourier), wb(w_br_conv), wb(w_br_attn), wb(w_out)

    rope_tabs = _rope_tables(seq)
    dft_l = _dft_mats(seq)
    dft_c = _dft_mats(n_ctx)
    cs = _channel_dft()

    cvec = jnp.zeros((MOD_ROWS, d), F32).at[:b].set(c).at[b].set(c_ctx)
    mods_all = _modulation(cvec, w_mod, b_mod).reshape(depth, MOD_ROWS, 6, 1, d)

    tm_l = min(512, seq)
    tm_c = n_ctx
    tq = min(512, seq)
    lat_row = lambda r: r // seq
    ctx_row = lambda r: b

    h = x.reshape(b * seq, d)
    hc = ctx.reshape(b * n_ctx, d)
    for i in range(depth):
        last = i == depth - 1
        lam_init = 0.8 - 0.6 * math.exp(-0.3 * i)
        mods = mods_all[i]
        w = dict(w_in=w_in, layer=i, cs=cs, lambdas=lambdas[i], subln_g=subln_g[i], conv_mix_w=conv_mix_w[i],
                 w_br_fourier=w_bf[i], w_br_conv=w_bc[i], w_br_attn=w_ba[i], w_out=w_o[i],
                 g_norm2=g_norm2[i], w_ffn_up=w_ffn_up, ffn_conv_w=ffn_conv_w, w_ffn_down=w_ffn_down)

        xc = _norm_modulate(hc, g_norm1[i], mods, ctx_row, tm_c)
        xl = _norm_modulate(h, g_norm1[i], mods, lat_row, tm_l)
        if last:
            _, pc = _mixer(hc, xc, n_ctx, None, dft_c, None, w, mods, ctx_row, lam_init, tm_c, n_ctx,
                           proj_cols=(PROJ_K_TILE, 2))
            ctx_kv = (pc.reshape(b, n_ctx, 2 * PROJ_TN), 0, PROJ_TN // LANES)
            hc_new = hc
        else:
            hc_new, pc = _mixer(hc, xc, n_ctx, None, dft_c, None, w, mods, ctx_row, lam_init, tm_c, n_ctx)
            ctx_kv = (pc.reshape(b, n_ctx, N_IN), P_K_OFF // LANES, P_V_OFF // LANES)
        h, _ = _mixer(h, xl, seq, rope_tabs, dft_l, ctx_kv, w, mods, lat_row, lam_init, tm_l, tq)
        hc = hc_new
    return _final_norm(h, g_final, tm_l).reshape(b, seq, d)
```

```python
import functools
import math

import numpy as np
import jax
import jax.numpy as jnp
from jax import lax
from jax.experimental import pallas as pl
from jax.experimental.pallas import tpu as pltpu

F32 = jnp.float32
BF16 = jnp.bfloat16

D_MODEL = 2048
GRID_W = 64
FOURIER_GROUPS = 4
FOURIER_GROUP_W = D_MODEL // 16
FOURIER_W = FOURIER_GROUPS * FOURIER_GROUP_W
CONV_W = D_MODEL // 4
N_HEADS = 8
HEAD_DIM = D_MODEL // (4 * N_HEADS)
VAL_DIM = 2 * HEAD_DIM
ATTN_QK_W = N_HEADS * 2 * HEAD_DIM
ATTN_V_W = N_HEADS * VAL_DIM
ROPE_THETA = 10000.0
ATTN_SCALE = HEAD_DIM ** -0.5
SUBLN_EPS = 1e-5
Q_OFF = FOURIER_W + 3 * CONV_W
K_OFF = Q_OFF + ATTN_QK_W
V_OFF = K_OFF + ATTN_QK_W
V_END = V_OFF + ATTN_V_W
N_IN = V_END + 3 * D_MODEL
D_FF = ((8 * D_MODEL // 3 + 255) // 256) * 256
EPS = 1e-6

LANES = 128
MXU_COLS = 256
BF16_SUBLANES = 16
VMEM_LIMIT = 56 * 1024 * 1024
MOD_ROWS = 8

PROJ_TN = 1024
PROJ_TM = 1024
P_GATE_OFF = Q_OFF
P_Q_OFF = P_GATE_OFF + 3 * D_MODEL
P_K_OFF = P_Q_OFF + ATTN_QK_W
P_V_OFF = P_K_OFF + ATTN_QK_W
PROJ_GATE_TILE = P_GATE_OFF // PROJ_TN
PROJ_Q_TILE = P_Q_OFF // PROJ_TN
PROJ_K_TILE = P_K_OFF // PROJ_TN
PROJ_TILES = N_IN // PROJ_TN
ATTN_TK = 512
FOURIER_TR = 512
FFN_TF = 512
FFN_UP_TM = 1024
FFN_DOWN_TN = 512
FFN_DOWN_TM = 512


def _cparams(*sem):
    return pltpu.CompilerParams(dimension_semantics=sem, vmem_limit_bytes=VMEM_LIMIT)


def _mod_kernel(c_ref, w_ref, b_ref, o_ref):
    c = c_ref[...]
    s = c * jax.nn.sigmoid(c)
    s_hi = s.astype(BF16)
    s_lo = (s - s_hi.astype(F32)).astype(BF16)
    w = w_ref[...]
    w_hi = w.astype(BF16)
    w_lo = (w - w_hi.astype(F32)).astype(BF16)
    y = jnp.dot(s_hi, w_hi, preferred_element_type=F32)
    y += jnp.dot(s_lo, w_hi, preferred_element_type=F32)
    y += jnp.dot(s_hi, w_lo, preferred_element_type=F32)
    o_ref[...] = y + b_ref[...]


def _modulation(cvec, w_mod, b_mod):
    depth, d, n = w_mod.shape
    tn = 1024
    return pl.pallas_call(
        _mod_kernel,
        grid=(depth, n // tn),
        in_specs=[
            pl.BlockSpec((MOD_ROWS, d), lambda l, j: (0, 0)),
            pl.BlockSpec((None, d, tn), lambda l, j: (l, 0, j)),
            pl.BlockSpec((None, 1, tn), lambda l, j: (l, 0, j)),
        ],
        out_specs=pl.BlockSpec((None, MOD_ROWS, tn), lambda l, j: (l, 0, j)),
        out_shape=jax.ShapeDtypeStruct((depth, MOD_ROWS, n), F32),
        compiler_params=_cparams("arbitrary", "arbitrary"),
        name="modulation",
    )(cvec, w_mod, b_mod.reshape(depth, 1, n))


def _mod_spec(which, mod_row, tm):
    return pl.BlockSpec((None, None, 1, D_MODEL), lambda i, *_: (mod_row(i * tm), which, 0, 0))


def _rms(x, g):
    return x * lax.rsqrt(jnp.mean(x * x, axis=-1, keepdims=True) + EPS) * g


def _norm_mod_kernel(h_ref, g_ref, sh_ref, sc_ref, o_ref):
    y = _rms(h_ref[...], g_ref[...])
    o_ref[...] = (y * (1.0 + sc_ref[...]) + sh_ref[...]).astype(o_ref.dtype)


def _norm_kernel(h_ref, g_ref, o_ref):
    o_ref[...] = _rms(h_ref[...], g_ref[...]).astype(o_ref.dtype)


def _norm_modulate(h, g, mods, mod_row, tm):
    m, d = h.shape
    return pl.pallas_call(
        _norm_mod_kernel,
        grid=(m // tm,),
        in_specs=[
            pl.BlockSpec((tm, d), lambda i: (i, 0)),
            pl.BlockSpec((1, d), lambda i: (0, 0)),
            _mod_spec(0, mod_row, tm),
            _mod_spec(1, mod_row, tm),
        ],
        out_specs=pl.BlockSpec((tm, d), lambda i: (i, 0)),
        out_shape=jax.ShapeDtypeStruct((m, d), BF16),
        compiler_params=_cparams("arbitrary"),
        name="norm_modulate",
    )(h, g.reshape(1, d), mods, mods)


def _final_norm(h, g, tm):
    m, d = h.shape
    return pl.pallas_call(
        _norm_kernel,
        grid=(m // tm,),
        in_specs=[pl.BlockSpec((tm, d), lambda i: (i, 0)), pl.BlockSpec((1, d), lambda i: (0, 0))],
        out_specs=pl.BlockSpec((tm, d), lambda i: (i, 0)),
        out_shape=jax.ShapeDtypeStruct((m, d), F32),
        compiler_params=_cparams("arbitrary"),
        name="final_norm",
    )(h, g.reshape(1, d))


def _proj_kernel(*refs, col0, rope):
    if rope:
        x_ref, w_ref, cos_ref, sin_up_ref, sin_dn_ref, o_ref, wb_ref = refs
    else:
        x_ref, w_ref, o_ref, wb_ref = refs
    j = pl.program_id(0) + col0

    @pl.when(pl.program_id(1) == 0)
    def _():
        wb_ref[...] = w_ref[...].astype(BF16)

    is_q = j == PROJ_Q_TILE
    is_k = j == PROJ_K_TILE
    is_gate = (j >= PROJ_GATE_TILE) & (j < PROJ_Q_TILE)
    x = x_ref[...]

    def column_chunks(epilogue):
        for c in range(PROJ_TN // MXU_COLS):
            cols = slice(c * MXU_COLS, (c + 1) * MXU_COLS)
            acc = jnp.dot(x, wb_ref[:, cols], preferred_element_type=F32)
            o_ref[:, cols] = epilogue(acc).astype(o_ref.dtype)

    def rotary(acc):
        qscale = jnp.where(is_q, ATTN_SCALE * math.log2(math.e), 1.0)
        half = HEAD_DIM // 4
        out = []
        for hd in range(MXU_COLS // LANES):
            a = acc[:, hd * LANES:(hd + 1) * LANES]
            if rope:
                a = (a * cos_ref[...] + pltpu.roll(a, LANES - half, 1) * sin_up_ref[...]
                     + pltpu.roll(a, half, 1) * sin_dn_ref[...])
            out.append(a * qscale)
        return jnp.concatenate(out, axis=1)

    pl.when(is_q | is_k)(lambda: column_chunks(rotary))
    pl.when(is_gate)(lambda: column_chunks(jax.nn.sigmoid))
    pl.when(jnp.logical_not(is_q | is_k | is_gate))(lambda: column_chunks(lambda acc: acc))


def _in_proj(xm, w_in, layer, rope_tabs, tm, col0, ncols, seq_len):
    m, d = xm.shape
    tn = PROJ_TN
    n_plain = PROJ_GATE_TILE
    n_gate = PROJ_Q_TILE - PROJ_GATE_TILE
    n_qkv = PROJ_TILES - PROJ_Q_TILE

    def w_tile(j):
        j = j + col0
        return jnp.where(j < n_plain, j, jnp.where(j < n_plain + n_gate, j + n_qkv, j - n_gate))

    in_specs = [
        pl.BlockSpec((tm, d), lambda j, i: (i, 0)),
        pl.BlockSpec((None, d, tn), lambda j, i: (layer, 0, w_tile(j))),
    ]
    args = [xm, w_in]
    rope = rope_tabs is not None
    if rope:
        tps = seq_len // tm
        in_specs += [pl.BlockSpec((tm, LANES), lambda j, i: (i % tps, 0))] * 3
        args += list(rope_tabs)
    return pl.pallas_call(
        functools.partial(_proj_kernel, col0=col0, rope=rope),
        grid=(ncols, m // tm),
        in_specs=in_specs,
        out_specs=pl.BlockSpec((tm, tn), lambda j, i: (i, j)),
        out_shape=jax.ShapeDtypeStruct((m, ncols * tn), BF16),
        scratch_shapes=[pltpu.VMEM((d, tn), BF16)],
        compiler_params=_cparams("arbitrary", "arbitrary"),
        name="in_proj",
    )(*args)


def _fourier_kernel(f_ref, cs_ref, c0_ref, s0_ref, rc_ref, rs_ref, o_ref, uc_ref, us_ref, *, chunk, scale):
    i = pl.program_id(1)
    seq = f_ref.shape[0]
    gw = FOURIER_GROUP_W

    @pl.when(i == 0)
    def _():
        for r in range(seq // chunk):
            rows = slice(r * chunk, (r + 1) * chunk)
            for g in range(FOURIER_GROUPS):
                cols = slice(g * gw, (g + 1) * gw)
                y = jnp.dot(f_ref[rows, cols], cs_ref[...], preferred_element_type=F32)
                uc_ref[rows, cols] = y[:, :gw].astype(BF16)
                us_ref[rows, cols] = y[:, gw:].astype(BF16)

    c0, s0 = c0_ref[...].astype(F32), s0_ref[...].astype(F32)
    rc, rs = rc_ref[pl.ds(i, 1), :], rs_ref[pl.ds(i, 1), :]
    cos_t = (c0 * rc - s0 * rs).astype(BF16)
    sin_t = (s0 * rc + c0 * rs).astype(BF16)
    y = jnp.dot(cos_t, uc_ref[...], preferred_element_type=F32)
    y -= jnp.dot(sin_t, us_ref[...], preferred_element_type=F32)
    o_ref[...] = (y * scale).astype(o_ref.dtype)


def _fourier(p3, cs, dft, tr):
    b, seq, _ = p3.shape
    chunk = min(seq, 1024)
    scale = 1.0 / math.sqrt(seq * FOURIER_GROUP_W)
    c0, s0, rc, rs = dft
    full = lambda a: pl.BlockSpec(a.shape, lambda bi, i: (0, 0), pipeline_mode=pl.Buffered(1))
    return pl.pallas_call(
        functools.partial(_fourier_kernel, chunk=chunk, scale=scale),
        grid=(b, seq // tr),
        in_specs=[
            pl.BlockSpec((None, seq, FOURIER_W), lambda bi, i: (bi, 0, 0)),
            pl.BlockSpec((FOURIER_GROUP_W, 2 * FOURIER_GROUP_W), lambda bi, i: (0, 0)),
            full(c0), full(s0), full(rc), full(rs),
        ],
        out_specs=pl.BlockSpec((None, tr, FOURIER_W), lambda bi, i: (bi, i, 0)),
        out_shape=jax.ShapeDtypeStruct((b, seq, FOURIER_W), BF16),
        scratch_shapes=[pltpu.VMEM((seq, FOURIER_W), BF16), pltpu.VMEM((seq, FOURIER_W), BF16)],
        compiler_params=_cparams("arbitrary", "arbitrary"),
        name="fourier",
    )(p3, cs, c0, s0, rc, rs)


def _attn_kernel(*refs, tq, tk, n_lat, lam_init):
    if n_lat:
        lam_ref, q_ref, kc_ref, vc_ref, kl_ref, vl_ref, g_ref, o_ref = refs
    else:
        lam_ref, q_ref, kc_ref, vc_ref, g_ref, o_ref = refs
        kl_ref = vl_ref = None
    n_ctx = kc_ref.shape[0]
    q = q_ref[...]
    lane = lax.broadcasted_iota(jnp.int32, (1, LANES), 1)
    comp0 = lane < HEAD_DIM
    zero = jnp.zeros_like(q)
    qq = jnp.concatenate([jnp.where(comp0, q, zero), jnp.where(comp0, zero, q)], axis=0)
    chunks = [(kc_ref, vc_ref, 0, n_ctx)] + [(kl_ref, vl_ref, c * tk, tk) for c in range(n_lat)]

    def lane_fold(x, op):
        part = x[:, 0:LANES]
        for c in range(1, x.shape[1] // LANES):
            part = op(part, x[:, c * LANES:(c + 1) * LANES])
        return part

    m = l = acc = None
    for kr, vr, r0, sz in chunks:
        s = lax.dot_general(qq, kr[r0:r0 + sz, :], (((1,), (1,)), ((), ())), preferred_element_type=F32)
        mc = jnp.max(lane_fold(s, jnp.maximum), axis=-1, keepdims=True)
        m_new = mc if m is None else jnp.maximum(m, mc)
        p = jnp.exp2(s - m_new)
        lc = lane_fold(p, jnp.add)
        y = jnp.dot(p.astype(BF16), vr[r0:r0 + sz, :], preferred_element_type=F32)
        if m is None:
            l, acc = lc, y
        else:
            alpha = jnp.exp2(m - m_new)
            l = alpha * l + lc
            acc = alpha * acc + y
        m = m_new
    o = acc / jnp.sum(l, axis=-1, keepdims=True)
    lf = lam_ref[...]
    lam = (jnp.exp(jnp.sum(lf[0:1] * lf[1:2], axis=-1, keepdims=True))
           - jnp.exp(jnp.sum(lf[2:3] * lf[3:4], axis=-1, keepdims=True)) + lam_init)
    o = o[:tq] - lam * o[tq:]
    o = o * lax.rsqrt(jnp.mean(o * o, axis=-1, keepdims=True) + SUBLN_EPS)
    o_ref[...] = (o * g_ref[...] * (1.0 - lam_init)).astype(o_ref.dtype)


def _attention(q3, qblk, c3, kcblk, vcblk, l3, lam, subln_g, lam_init, tq):
    b, lq, _ = q3.shape
    n_ctx = c3.shape[1]
    tk = ATTN_TK
    in_specs = [
        pl.BlockSpec((4, HEAD_DIM), lambda bi, h, i: (0, 0)),
        pl.BlockSpec((None, tq, LANES), lambda bi, h, i: (bi, i, qblk + h)),
        pl.BlockSpec((None, n_ctx, LANES), lambda bi, h, i: (bi, 0, kcblk + h)),
        pl.BlockSpec((None, n_ctx, LANES), lambda bi, h, i: (bi, 0, vcblk + h)),
    ]
    args = [lam, q3, c3, c3]
    n_lat = 0
    if l3 is not None:
        seq = l3.shape[1]
        n_lat = seq // min(tk, seq)
        tk = seq // n_lat
        in_specs += [
            pl.BlockSpec((None, seq, LANES), lambda bi, h, i: (bi, 0, P_K_OFF // LANES + h)),
            pl.BlockSpec((None, seq, LANES), lambda bi, h, i: (bi, 0, P_V_OFF // LANES + h)),
        ]
        args += [l3, l3]
    in_specs.append(pl.BlockSpec((1, VAL_DIM), lambda bi, h, i: (0, 0)))
    args.append(subln_g.reshape(1, VAL_DIM))
    return pl.pallas_call(
        functools.partial(_attn_kernel, tq=tq, tk=tk, n_lat=n_lat, lam_init=lam_init),
        grid=(b, N_HEADS, lq // tq),
        in_specs=in_specs,
        out_specs=pl.BlockSpec((None, tq, LANES), lambda bi, h, i: (bi, i, h)),
        out_shape=jax.ShapeDtypeStruct((b, lq, ATTN_V_W), BF16),
        compiler_params=_cparams("arbitrary", "arbitrary", "arbitrary"),
        name="diff_attention",
    )(*args)


def _conv3_rows(buf_ref, w, tm):
    h = BF16_SUBLANES
    return (buf_ref[h - 1:h - 1 + tm, :] * w[0:1] + buf_ref[h:h + tm, :] * w[1:2]
            + buf_ref[h + 1:h + 1 + tm, :] * w[2:3])


def _halo_specs(tm, width, colblk, m, row_axis=0):
    per = tm // BF16_SUBLANES
    last = m // BF16_SUBLANES - 1
    prev = pl.BlockSpec((BF16_SUBLANES, width),
                        lambda *ids: (jnp.maximum(ids[row_axis] * per - 1, 0), colblk))
    nxt = pl.BlockSpec((BF16_SUBLANES, width),
                       lambda *ids: (jnp.minimum((ids[row_axis] + 1) * per, last), colblk))
    return prev, nxt


def _merge_kernel(f_ref, cb_ref, cc_ref, cx_ref, ccp_ref, cxp_ref, ccn_ref, cxn_ref, a_ref,
                  gf_ref, gc_ref, ga_ref, cw_ref, wbf_ref, wbc_ref, wba_ref, o_ref, z_ref, *, tps, nchunk):
    i = pl.program_id(0)
    tm = cc_ref.shape[0]
    h = BF16_SUBLANES
    first = (i % tps) == 0
    last = (i % tps) == tps - 1
    zp = ccp_ref[...].astype(F32) * cxp_ref[...].astype(F32)
    zn = ccn_ref[...].astype(F32) * cxn_ref[...].astype(F32)
    z_ref[0:h, :] = jnp.where(first, 0.0, zp)
    z_ref[h:h + tm, :] = cc_ref[...].astype(F32) * cx_ref[...].astype(F32)
    z_ref[h + tm:2 * h + tm, :] = jnp.where(last, 0.0, zn)
    cv = (cb_ref[...].astype(F32) * _conv3_rows(z_ref, cw_ref[...], tm)).astype(BF16)
    f = f_ref[...]
    a = a_ref[...]
    cn = D_MODEL // nchunk
    for n in range(nchunk):
        cols = slice(n * cn, (n + 1) * cn)
        y = gf_ref[:, cols].astype(F32) * jnp.dot(f, wbf_ref[:, cols], preferred_element_type=F32)
        y += gc_ref[:, cols].astype(F32) * jnp.dot(cv, wbc_ref[:, cols], preferred_element_type=F32)
        y += ga_ref[:, cols].astype(F32) * jnp.dot(a, wba_ref[:, cols], preferred_element_type=F32)
        o_ref[:, cols] = y.astype(o_ref.dtype)


def _merge(p, fmix, attn, conv_w, w_bf, w_bc, w_ba, tm, seq_len):
    m = p.shape[0]
    d = D_MODEL
    cw = CONV_W
    tps = seq_len // tm
    blk = lambda width, c: pl.BlockSpec((tm, width), lambda i: (i, c))
    ccp, ccn = _halo_specs(tm, cw, 2, m)
    cxp, cxn = _halo_specs(tm, cw, 3, m)
    gate0 = P_GATE_OFF // d
    full = lambda r, c: pl.BlockSpec((r, c), lambda i: (0, 0), pipeline_mode=pl.Buffered(1))
    return pl.pallas_call(
        functools.partial(_merge_kernel, tps=tps, nchunk=4),
        grid=(m // tm,),
        in_specs=[
            blk(FOURIER_W, 0),
            blk(cw, 1), blk(cw, 2), blk(cw, 3),
            ccp, cxp, ccn, cxn,
            blk(ATTN_V_W, 0),
            pl.BlockSpec((tm, d), lambda i: (i, gate0)),
            pl.BlockSpec((tm, d), lambda i: (i, gate0 + 1)),
            pl.BlockSpec((tm, d), lambda i: (i, gate0 + 2)),
            full(3, cw), full(FOURIER_W, d), full(cw, d), full(ATTN_V_W, d),
        ],
        out_specs=pl.BlockSpec((tm, d), lambda i: (i, 0)),
        out_shape=jax.ShapeDtypeStruct((m, d), BF16),
        scratch_shapes=[pltpu.VMEM((tm + 2 * BF16_SUBLANES, cw), F32)],
        compiler_params=_cparams("arbitrary"),
        name="branch_merge",
    )(fmix, p, p, p, p, p, p, p, attn, p, p, p, conv_w, w_bf, w_bc, w_ba)


def _outproj_kernel(m_ref, w_ref, h_ref, gt_ref, g_ref, sh_ref, sc_ref, ho_ref, xo_ref):
    y = jnp.dot(m_ref[...], w_ref[...], preferred_element_type=F32)
    h = h_ref[...] + gt_ref[...] * y
    ho_ref[...] = h
    xo_ref[...] = (_rms(h, g_ref[...]) * (1.0 + sc_ref[...]) + sh_ref[...]).astype(xo_ref.dtype)


def _out_proj(merged, w_out, h, g2, mods, mod_row, tm):
    m, d = h.shape
    return pl.pallas_call(
        _outproj_kernel,
        grid=(m // tm,),
        in_specs=[
            pl.BlockSpec((tm, d), lambda i: (i, 0)),
            pl.BlockSpec((d, d), lambda i: (0, 0), pipeline_mode=pl.Buffered(1)),
            pl.BlockSpec((tm, d), lambda i: (i, 0)),
            _mod_spec(2, mod_row, tm),
            pl.BlockSpec((1, d), lambda i: (0, 0)),
            _mod_spec(3, mod_row, tm),
            _mod_spec(4, mod_row, tm),
        ],
        out_specs=[pl.BlockSpec((tm, d), lambda i: (i, 0)), pl.BlockSpec((tm, d), lambda i: (i, 0))],
        out_shape=[jax.ShapeDtypeStruct((m, d), F32), jax.ShapeDtypeStruct((m, d), BF16)],
        compiler_params=_cparams("arbitrary"),
        name="out_proj",
    )(merged, w_out, h, mods, g2.reshape(1, d), mods, mods)


def _ffn_up_kernel(x_ref, xp_ref, xn_ref, wa_ref, wv_ref, ca_ref, cv_ref, o_ref,
                   wab_ref, wvb_ref, xh_ref, ua_ref, uv_ref, *, tps):
    i = pl.program_id(1)
    tm = x_ref.shape[0]
    hl = BF16_SUBLANES

    @pl.when(i == 0)
    def _():
        wab_ref[...] = wa_ref[...].astype(BF16)
        wvb_ref[...] = wv_ref[...].astype(BF16)

    first = (i % tps) == 0
    last = (i % tps) == tps - 1
    xh_ref[0:hl, :] = jnp.where(first, jnp.zeros_like(xp_ref[...]), xp_ref[...])
    xh_ref[hl:hl + tm, :] = x_ref[...]
    xh_ref[hl + tm:2 * hl + tm, :] = jnp.where(last, jnp.zeros_like(xn_ref[...]), xn_ref[...])
    xh = xh_ref[...]
    ua_ref[...] = jnp.dot(xh, wab_ref[...], preferred_element_type=F32)
    uv_ref[...] = jnp.dot(xh, wvb_ref[...], preferred_element_type=F32)
    a = _conv3_rows(ua_ref, ca_ref[...], tm)
    v = _conv3_rows(uv_ref, cv_ref[...], tm)
    o_ref[...] = (a * jax.nn.sigmoid(a) * v).astype(o_ref.dtype)


def _ffn_down_kernel(a_ref, w_ref, h_ref, gt_ref, o_ref, wb_ref):
    @pl.when(pl.program_id(1) == 0)
    def _():
        wb_ref[...] = w_ref[...].astype(BF16)

    y = jnp.dot(a_ref[...], wb_ref[...], preferred_element_type=F32)
    o_ref[...] = h_ref[...] + gt_ref[...] * y


def _ffn(xm, w_up, conv_w, w_down, layer, h, mods, mod_row, tm, seq_len):
    m, d = xm.shape
    tf = FFN_TF
    nj = D_FF // tf
    tps = seq_len // tm
    xp, xn = _halo_specs(tm, d, 0, m, row_axis=1)
    act = pl.pallas_call(
        functools.partial(_ffn_up_kernel, tps=tps),
        grid=(nj, m // tm),
        in_specs=[
            pl.BlockSpec((tm, d), lambda j, i: (i, 0)),
            xp, xn,
            pl.BlockSpec((None, d, tf), lambda j, i: (layer, 0, j)),
            pl.BlockSpec((None, d, tf), lambda j, i: (layer, 0, nj + j)),
            pl.BlockSpec((None, 3, tf), lambda j, i: (layer, 0, j)),
            pl.BlockSpec((None, 3, tf), lambda j, i: (layer, 0, nj + j)),
        ],
        out_specs=pl.BlockSpec((tm, tf), lambda j, i: (i, j)),
        out_shape=jax.ShapeDtypeStruct((m, D_FF), BF16),
        scratch_shapes=[
            pltpu.VMEM((d, tf), BF16),
            pltpu.VMEM((d, tf), BF16),
            pltpu.VMEM((tm + 2 * BF16_SUBLANES, d), BF16),
            pltpu.VMEM((tm + 2 * BF16_SUBLANES, tf), F32),
            pltpu.VMEM((tm + 2 * BF16_SUBLANES, tf), F32),
        ],
        compiler_params=_cparams("arbitrary", "arbitrary"),
        name="ffn_up",
    )(xm, xm, xm, w_up, w_up, conv_w, conv_w)

    tn = FFN_DOWN_TN
    tmd = min(m, FFN_DOWN_TM)
    gate_row = lambda n, i: mod_row(i * tmd)
    return pl.pallas_call(
        _ffn_down_kernel,
        grid=(d // tn, m // tmd),
        in_specs=[
            pl.BlockSpec((tmd, D_FF), lambda n, i: (i, 0)),
            pl.BlockSpec((None, D_FF, tn), lambda n, i: (layer, 0, n)),
            pl.BlockSpec((tmd, tn), lambda n, i: (i, n)),
            pl.BlockSpec((None, None, 1, tn), lambda n, i: (gate_row(n, i), 5, 0, n)),
        ],
        out_specs=pl.BlockSpec((tmd, tn), lambda n, i: (i, n)),
        out_shape=jax.ShapeDtypeStruct((m, d), F32),
        scratch_shapes=[pltpu.VMEM((D_FF, tn), BF16)],
        compiler_params=_cparams("arbitrary", "arbitrary"),
        name="ffn_down",
    )(act, w_down, h, mods)


def _rope_tables(length):
    n_freq = HEAD_DIM // 4
    pos = jnp.arange(length)
    row = (pos // GRID_W).astype(F32)
    col = (pos % GRID_W).astype(F32)
    inv = ROPE_THETA ** (-(2.0 * jnp.arange(n_freq, dtype=F32)) / (HEAD_DIM // 2))
    ang_r, ang_c = row[:, None] * inv, col[:, None] * inv
    ang = jnp.concatenate([ang_r, ang_r, ang_c, ang_c], axis=1)
    cos = jnp.tile(jnp.cos(ang), (1, 2))
    sin = jnp.tile(jnp.sin(ang), (1, 2))
    is_x1 = (jnp.arange(LANES) % (2 * n_freq)) < n_freq
    return cos, jnp.where(is_x1, -sin, 0.0), jnp.where(is_x1, 0.0, sin)


def _dft_tables(n, tr):
    m = jnp.arange(n, dtype=jnp.int32)

    def cos_sin(rows):
        ang = ((rows[:, None] * m[None, :]) % n).astype(F32) * (2.0 * math.pi / n)
        return jnp.cos(ang), jnp.sin(ang)

    c0, s0 = cos_sin(jnp.arange(tr, dtype=jnp.int32))
    rc, rs = cos_sin(jnp.arange(n // tr, dtype=jnp.int32) * tr)
    return c0.astype(BF16), s0.astype(BF16), rc, rs


def _channel_dft():
    c = np.arange(FOURIER_GROUP_W)
    ang = 2.0 * np.pi * ((c[:, None] * c[None, :]) % FOURIER_GROUP_W) / FOURIER_GROUP_W
    return jnp.asarray(np.concatenate([np.cos(ang), np.sin(ang)], axis=1), dtype=BF16)


def _mixer(h, xm, seq_len, rope_tabs, dft, ctx_kv, w, mods, mod_row, lam_init, tm, tq,
           proj_cols=(0, PROJ_TILES)):
    m = h.shape[0]
    b = m // seq_len
    p = _in_proj(xm, w["w_in"], w["layer"], rope_tabs, min(m, PROJ_TM), proj_cols[0], proj_cols[1], seq_len)
    if proj_cols[1] != PROJ_TILES:
        return None, p
    p3 = p.reshape(b, seq_len, N_IN)
    fmix = _fourier(p3, w["cs"], dft, min(seq_len, FOURIER_TR)).reshape(m, FOURIER_W)
    if ctx_kv is None:
        attn = _attention(p3, P_Q_OFF // LANES, p3, P_K_OFF // LANES, P_V_OFF // LANES, None,
                          w["lambdas"], w["subln_g"], lam_init, tq)
    else:
        c3, kcblk, vcblk = ctx_kv
        attn = _attention(p3, P_Q_OFF // LANES, c3, kcblk, vcblk, p3,
                          w["lambdas"], w["subln_g"], lam_init, tq)
    merged = _merge(p, fmix, attn.reshape(m, ATTN_V_W), w["conv_mix_w"], w["w_br_fourier"],
                    w["w_br_conv"], w["w_br_attn"], tm, seq_len)
    h_mid, xm2 = _out_proj(merged, w["w_out"], h, w["g_norm2"], mods, mod_row, tm)
    h_new = _ffn(xm2, w["w_ffn_up"], w["ffn_conv_w"], w["w_ffn_down"], w["layer"], h_mid, mods, mod_row,
                 min(seq_len, FFN_UP_TM), seq_len)
    return h_new, p


def kernel(x, c, ctx, c_ctx, w_mod, b_mod, g_norm1, g_norm2, w_in, conv_mix_w, lambdas, subln_g,
           w_br_fourier, w_br_conv, w_br_attn, w_out, w_ffn_up, ffn_conv_w, w_ffn_down, g_final):
    b, seq, d = x.shape
    n_ctx = ctx.shape[1]
    depth = w_mod.shape[0]
    assert d == D_MODEL and b + 1 <= MOD_ROWS

    wb = lambda a: a.astype(BF16)
    w_bf, w_bc, w_ba, w_o = wb(w_br_fourier), wb(w_br_conv), wb(w_br_attn), wb(w_out)

    rope_tabs = _rope_tables(seq)
    dft_l = _dft_tables(seq, min(seq, FOURIER_TR))
    dft_c = _dft_tables(n_ctx, min(n_ctx, FOURIER_TR))
    cs = _channel_dft()

    cvec = jnp.zeros((MOD_ROWS, d), F32).at[:b].set(c).at[b].set(c_ctx)
    mods_all = _modulation(cvec, w_mod, b_mod).reshape(depth, MOD_ROWS, 6, 1, d)

    tm_l = min(512, seq)
    tm_c = n_ctx
    tq = min(512, seq)
    lat_row = lambda r: r // seq
    ctx_row = lambda r: b

    h = x.reshape(b * seq, d)
    hc = ctx.reshape(b * n_ctx, d)
    for i in range(depth):
        last = i == depth - 1
        lam_init = 0.8 - 0.6 * math.exp(-0.3 * i)
        mods = mods_all[i]
        w = dict(w_in=w_in, layer=i, cs=cs, lambdas=lambdas[i], subln_g=subln_g[i], conv_mix_w=conv_mix_w[i],
                 w_br_fourier=w_bf[i], w_br_conv=w_bc[i], w_br_attn=w_ba[i], w_out=w_o[i],
                 g_norm2=g_norm2[i], w_ffn_up=w_ffn_up, ffn_conv_w=ffn_conv_w, w_ffn_down=w_ffn_down)

        xc = _norm_modulate(hc, g_norm1[i], mods, ctx_row, tm_c)
        xl = _norm_modulate(h, g_norm1[i], mods, lat_row, tm_l)
        if last:
            _, pc = _mixer(hc, xc, n_ctx, None, dft_c, None, w, mods, ctx_row, lam_init, tm_c, n_ctx,
                           proj_cols=(PROJ_K_TILE, 2))
            ctx_kv = (pc.reshape(b, n_ctx, 2 * PROJ_TN), 0, PROJ_TN // LANES)
            hc_new = hc
        else:
            hc_new, pc = _mixer(hc, xc, n_ctx, None, dft_c, None, w, mods, ctx_row, lam_init, tm_c, n_ctx)
            ctx_kv = (pc.reshape(b, n_ctx, N_IN), P_K_OFF // LANES, P_V_OFF // LANES)
        h, _ = _mixer(h, xl, seq, rope_tabs, dft_l, ctx_kv, w, mods, lat_row, lam_init, tm_l, tq)
        hc = hc_new
    return _final_norm(h, g_final, tm_l).reshape(b, seq, d)
```

```python
import functools
import math

import numpy as np
import jax
import jax.numpy as jnp
from jax import lax
from jax.experimental import pallas as pl
from jax.experimental.pallas import tpu as pltpu

F32 = jnp.float32
BF16 = jnp.bfloat16

D_MODEL = 2048
GRID_W = 64
FOURIER_GROUPS = 4
FOURIER_GROUP_W = D_MODEL // 16
FOURIER_W = FOURIER_GROUPS * FOURIER_GROUP_W
CONV_W = D_MODEL // 4
N_HEADS = 8
HEAD_DIM = D_MODEL // (4 * N_HEADS)
VAL_DIM = 2 * HEAD_DIM
ATTN_QK_W = N_HEADS * 2 * HEAD_DIM
ATTN_V_W = N_HEADS * VAL_DIM
ROPE_THETA = 10000.0
ATTN_SCALE = HEAD_DIM ** -0.5
SUBLN_EPS = 1e-5
Q_OFF = FOURIER_W + 3 * CONV_W
K_OFF = Q_OFF + ATTN_QK_W
V_OFF = K_OFF + ATTN_QK_W
V_END = V_OFF + ATTN_V_W
N_IN = V_END + 3 * D_MODEL
D_FF = ((8 * D_MODEL // 3 + 255) // 256) * 256
EPS = 1e-6

LANES = 128
MXU_COLS = 256
BF16_SUBLANES = 16
VMEM_LIMIT = 56 * 1024 * 1024
MOD_ROWS = 8

PROJ_TN = 1024
PROJ_TM = 1024
P_GATE_OFF = Q_OFF
P_Q_OFF = P_GATE_OFF + 3 * D_MODEL
P_K_OFF = P_Q_OFF + ATTN_QK_W
P_V_OFF = P_K_OFF + ATTN_QK_W
PROJ_GATE_TILE = P_GATE_OFF // PROJ_TN
PROJ_Q_TILE = P_Q_OFF // PROJ_TN
PROJ_K_TILE = P_K_OFF // PROJ_TN
PROJ_TILES = N_IN // PROJ_TN
ATTN_TK = 512
FOURIER_TR = 512
FFN_TF = 512
FFN_UP_TM = 1024
FFN_DOWN_TN = 512
FFN_DOWN_TM = 512


def _cparams(*sem):
    return pltpu.CompilerParams(dimension_semantics=sem, vmem_limit_bytes=VMEM_LIMIT)


def _mod_kernel(c_ref, w_ref, b_ref, o_ref):
    c = c_ref[...]
    s = c * jax.nn.sigmoid(c)
    s_hi = s.astype(BF16)
    s_lo = (s - s_hi.astype(F32)).astype(BF16)
    w = w_ref[...]
    w_hi = w.astype(BF16)
    w_lo = (w - w_hi.astype(F32)).astype(BF16)
    y = jnp.dot(s_hi, w_hi, preferred_element_type=F32)
    y += jnp.dot(s_lo, w_hi, preferred_element_type=F32)
    y += jnp.dot(s_hi, w_lo, preferred_element_type=F32)
    o_ref[...] = y + b_ref[...]


def _modulation(cvec, w_mod, b_mod):
    depth, d, n = w_mod.shape
    tn = 1024
    return pl.pallas_call(
        _mod_kernel,
        grid=(depth, n // tn),
        in_specs=[
            pl.BlockSpec((MOD_ROWS, d), lambda l, j: (0, 0)),
            pl.BlockSpec((None, d, tn), lambda l, j: (l, 0, j)),
            pl.BlockSpec((None, 1, tn), lambda l, j: (l, 0, j)),
        ],
        out_specs=pl.BlockSpec((None, MOD_ROWS, tn), lambda l, j: (l, 0, j)),
        out_shape=jax.ShapeDtypeStruct((depth, MOD_ROWS, n), F32),
        compiler_params=_cparams("arbitrary", "arbitrary"),
        name="modulation",
    )(cvec, w_mod, b_mod.reshape(depth, 1, n))


def _mod_spec(which, mod_row, tm):
    return pl.BlockSpec((None, None, 1, D_MODEL), lambda i, *_: (mod_row(i * tm), which, 0, 0))


def _rms(x, g):
    return x * lax.rsqrt(jnp.mean(x * x, axis=-1, keepdims=True) + EPS) * g


def _norm_mod_kernel(h_ref, g_ref, sh_ref, sc_ref, o_ref):
    y = _rms(h_ref[...], g_ref[...])
    o_ref[...] = (y * (1.0 + sc_ref[...]) + sh_ref[...]).astype(o_ref.dtype)


def _norm_kernel(h_ref, g_ref, o_ref):
    o_ref[...] = _rms(h_ref[...], g_ref[...]).astype(o_ref.dtype)


def _norm_modulate(h, g, mods, mod_row, tm):
    m, d = h.shape
    return pl.pallas_call(
        _norm_mod_kernel,
        grid=(m // tm,),
        in_specs=[
            pl.BlockSpec((tm, d), lambda i: (i, 0)),
            pl.BlockSpec((1, d), lambda i: (0, 0)),
            _mod_spec(0, mod_row, tm),
            _mod_spec(1, mod_row, tm),
        ],
        out_specs=pl.BlockSpec((tm, d), lambda i: (i, 0)),
        out_shape=jax.ShapeDtypeStruct((m, d), BF16),
        compiler_params=_cparams("arbitrary"),
        name="norm_modulate",
    )(h, g.reshape(1, d), mods, mods)


def _final_norm(h, g, tm):
    m, d = h.shape
    return pl.pallas_call(
        _norm_kernel,
        grid=(m // tm,),
        in_specs=[pl.BlockSpec((tm, d), lambda i: (i, 0)), pl.BlockSpec((1, d), lambda i: (0, 0))],
        out_specs=pl.BlockSpec((tm, d), lambda i: (i, 0)),
        out_shape=jax.ShapeDtypeStruct((m, d), F32),
        compiler_params=_cparams("arbitrary"),
        name="final_norm",
    )(h, g.reshape(1, d))


def _proj_kernel(*refs, col0, rope):
    if rope:
        x_ref, w_ref, cos_ref, sin_up_ref, sin_dn_ref, o_ref, wb_ref = refs
    else:
        x_ref, w_ref, o_ref, wb_ref = refs
    j = pl.program_id(0) + col0

    @pl.when(pl.program_id(1) == 0)
    def _():
        wb_ref[...] = w_ref[...].astype(BF16)

    is_q = j == PROJ_Q_TILE
    is_k = j == PROJ_K_TILE
    is_gate = (j >= PROJ_GATE_TILE) & (j < PROJ_Q_TILE)
    x = x_ref[...]

    def column_chunks(epilogue):
        for c in range(PROJ_TN // MXU_COLS):
            cols = slice(c * MXU_COLS, (c + 1) * MXU_COLS)
            acc = jnp.dot(x, wb_ref[:, cols], preferred_element_type=F32)
            o_ref[:, cols] = epilogue(acc).astype(o_ref.dtype)

    def rotary(acc):
        qscale = jnp.where(is_q, ATTN_SCALE * math.log2(math.e), 1.0)
        half = HEAD_DIM // 4
        out = []
        for hd in range(MXU_COLS // LANES):
            a = acc[:, hd * LANES:(hd + 1) * LANES]
            if rope:
                a = (a * cos_ref[...] + pltpu.roll(a, LANES - half, 1) * sin_up_ref[...]
                     + pltpu.roll(a, half, 1) * sin_dn_ref[...])
            out.append(a * qscale)
        return jnp.concatenate(out, axis=1)

    pl.when(is_q | is_k)(lambda: column_chunks(rotary))
    pl.when(is_gate)(lambda: column_chunks(jax.nn.sigmoid))
    pl.when(jnp.logical_not(is_q | is_k | is_gate))(lambda: column_chunks(lambda acc: acc))


def _in_proj(xm, w_in, layer, rope_tabs, tm, col0, ncols, seq_len):
    m, d = xm.shape
    tn = PROJ_TN
    n_plain = PROJ_GATE_TILE
    n_gate = PROJ_Q_TILE - PROJ_GATE_TILE
    n_qkv = PROJ_TILES - PROJ_Q_TILE

    def w_tile(j):
        j = j + col0
        return jnp.where(j < n_plain, j, jnp.where(j < n_plain + n_gate, j + n_qkv, j - n_gate))

    in_specs = [
        pl.BlockSpec((tm, d), lambda j, i: (i, 0)),
        pl.BlockSpec((None, d, tn), lambda j, i: (layer, 0, w_tile(j))),
    ]
    args = [xm, w_in]
    rope = rope_tabs is not None
    if rope:
        tps = seq_len // tm
        in_specs += [pl.BlockSpec((tm, LANES), lambda j, i: (i % tps, 0))] * 3
        args += list(rope_tabs)
    return pl.pallas_call(
        functools.partial(_proj_kernel, col0=col0, rope=rope),
        grid=(ncols, m // tm),
        in_specs=in_specs,
        out_specs=pl.BlockSpec((tm, tn), lambda j, i: (i, j)),
        out_shape=jax.ShapeDtypeStruct((m, ncols * tn), BF16),
        scratch_shapes=[pltpu.VMEM((d, tn), BF16)],
        compiler_params=_cparams("arbitrary", "arbitrary"),
        name="in_proj",
    )(*args)


def _fourier_kernel(f_ref, cs_ref, c0_ref, s0_ref, rc_ref, rs_ref, o_ref, uc_ref, us_ref, *, chunk, scale):
    i = pl.program_id(1)
    seq = f_ref.shape[0]
    gw = FOURIER_GROUP_W

    @pl.when(i == 0)
    def _():
        for r in range(seq // chunk):
            rows = slice(r * chunk, (r + 1) * chunk)
            for g in range(FOURIER_GROUPS):
                cols = slice(g * gw, (g + 1) * gw)
                y = jnp.dot(f_ref[rows, cols], cs_ref[...], preferred_element_type=F32)
                uc_ref[rows, cols] = y[:, :gw].astype(BF16)
                us_ref[rows, cols] = y[:, gw:].astype(BF16)

    c0, s0 = c0_ref[...].astype(F32), s0_ref[...].astype(F32)
    rc, rs = rc_ref[pl.ds(i, 1), :], rs_ref[pl.ds(i, 1), :]
    cos_t = (c0 * rc - s0 * rs).astype(BF16)
    sin_t = (s0 * rc + c0 * rs).astype(BF16)
    y = jnp.dot(cos_t, uc_ref[...], preferred_element_type=F32)
    y -= jnp.dot(sin_t, us_ref[...], preferred_element_type=F32)
    o_ref[...] = (y * scale).astype(o_ref.dtype)


def _fourier(p3, cs, dft, tr):
    b, seq, _ = p3.shape
    chunk = min(seq, 1024)
    scale = 1.0 / math.sqrt(seq * FOURIER_GROUP_W)
    c0, s0, rc, rs = dft
    full = lambda a: pl.BlockSpec(a.shape, lambda bi, i: (0, 0), pipeline_mode=pl.Buffered(1))
    return pl.pallas_call(
        functools.partial(_fourier_kernel, chunk=chunk, scale=scale),
        grid=(b, seq // tr),
        in_specs=[
            pl.BlockSpec((None, seq, FOURIER_W), lambda bi, i: (bi, 0, 0)),
            pl.BlockSpec((FOURIER_GROUP_W, 2 * FOURIER_GROUP_W), lambda bi, i: (0, 0)),
            full(c0), full(s0), full(rc), full(rs),
        ],
        out_specs=pl.BlockSpec((None, tr, FOURIER_W), lambda bi, i: (bi, i, 0)),
        out_shape=jax.ShapeDtypeStruct((b, seq, FOURIER_W), BF16),
        scratch_shapes=[pltpu.VMEM((seq, FOURIER_W), BF16), pltpu.VMEM((seq, FOURIER_W), BF16)],
        compiler_params=_cparams("arbitrary", "arbitrary"),
        name="fourier",
    )(p3, cs, c0, s0, rc, rs)


def _attn_kernel(*refs, tq, tk, n_lat, lam_init):
    if n_lat:
        lam_ref, q_ref, kc_ref, vc_ref, kl_ref, vl_ref, g_ref, o_ref = refs
    else:
        lam_ref, q_ref, kc_ref, vc_ref, g_ref, o_ref = refs
        kl_ref = vl_ref = None
    n_ctx = kc_ref.shape[0]
    q = q_ref[...]
    lane = lax.broadcasted_iota(jnp.int32, (1, LANES), 1)
    comp0 = lane < HEAD_DIM
    zero = jnp.zeros_like(q)
    qq = jnp.concatenate([jnp.where(comp0, q, zero), jnp.where(comp0, zero, q)], axis=0)
    chunks = [(kc_ref, vc_ref, 0, n_ctx)] + [(kl_ref, vl_ref, c * tk, tk) for c in range(n_lat)]

    def lane_fold(x, op):
        part = x[:, 0:LANES]
        for c in range(1, x.shape[1] // LANES):
            part = op(part, x[:, c * LANES:(c + 1) * LANES])
        return part

    m = l = acc = None
    for kr, vr, r0, sz in chunks:
        s = lax.dot_general(qq, kr[r0:r0 + sz, :], (((1,), (1,)), ((), ())), preferred_element_type=F32)
        mc = jnp.max(lane_fold(s, jnp.maximum), axis=-1, keepdims=True)
        m_new = mc if m is None else jnp.maximum(m, mc)
        p = jnp.exp2(s - m_new)
        lc = lane_fold(p, jnp.add)
        y = jnp.dot(p.astype(BF16), vr[r0:r0 + sz, :], preferred_element_type=F32)
        if m is None:
            l, acc = lc, y
        else:
            alpha = jnp.exp2(m - m_new)
            l = alpha * l + lc
            acc = alpha * acc + y
        m = m_new
    o = acc / jnp.sum(l, axis=-1, keepdims=True)
    lf = lam_ref[...]
    lam = (jnp.exp(jnp.sum(lf[0:1] * lf[1:2], axis=-1, keepdims=True))
           - jnp.exp(jnp.sum(lf[2:3] * lf[3:4], axis=-1, keepdims=True)) + lam_init)
    o = o[:tq] - lam * o[tq:]
    o = o * lax.rsqrt(jnp.mean(o * o, axis=-1, keepdims=True) + SUBLN_EPS)
    o_ref[...] = (o * g_ref[...] * (1.0 - lam_init)).astype(o_ref.dtype)


def _attention(q3, qblk, c3, kcblk, vcblk, l3, lam, subln_g, lam_init, tq):
    b, lq, _ = q3.shape
    n_ctx = c3.shape[1]
    tk = ATTN_TK
    in_specs = [
        pl.BlockSpec((4, HEAD_DIM), lambda bi, h, i: (0, 0)),
        pl.BlockSpec((None, tq, LANES), lambda bi, h, i: (bi, i, qblk + h)),
        pl.BlockSpec((None, n_ctx, LANES), lambda bi, h, i: (bi, 0, kcblk + h)),
        pl.BlockSpec((None, n_ctx, LANES), lambda bi, h, i: (bi, 0, vcblk + h)),
    ]
    args = [lam, q3, c3, c3]
    n_lat = 0
    if l3 is not None:
        seq = l3.shape[1]
        n_lat = seq // min(tk, seq)
        tk = seq // n_lat
        in_specs += [
            pl.BlockSpec((None, seq, LANES), lambda bi, h, i: (bi, 0, P_K_OFF // LANES + h)),
            pl.BlockSpec((None, seq, LANES), lambda bi, h, i: (bi, 0, P_V_OFF // LANES + h)),
        ]
        args += [l3, l3]
    in_specs.append(pl.BlockSpec((1, VAL_DIM), lambda bi, h, i: (0, 0)))
    args.append(subln_g.reshape(1, VAL_DIM))
    return pl.pallas_call(
        functools.partial(_attn_kernel, tq=tq, tk=tk, n_lat=n_lat, lam_init=lam_init),
        grid=(b, N_HEADS, lq // tq),
        in_specs=in_specs,
        out_specs=pl.BlockSpec((None, tq, LANES), lambda bi, h, i: (bi, i, h)),
        out_shape=jax.ShapeDtypeStruct((b, lq, ATTN_V_W), BF16),
        compiler_params=_cparams("arbitrary", "arbitrary", "arbitrary"),
        name="diff_attention",
    )(*args)


def _conv3_rows(buf_ref, w, tm, base=BF16_SUBLANES):
    return (buf_ref[base - 1:base - 1 + tm, :] * w[0:1] + buf_ref[base:base + tm, :] * w[1:2]
            + buf_ref[base + 1:base + 1 + tm, :] * w[2:3])


def _halo_specs(tm, width, colblk, m, row_axis=0):
    per = tm // BF16_SUBLANES
    last = m // BF16_SUBLANES - 1
    prev = pl.BlockSpec((BF16_SUBLANES, width),
                        lambda *ids: (jnp.maximum(ids[row_axis] * per - 1, 0), colblk))
    nxt = pl.BlockSpec((BF16_SUBLANES, width),
                       lambda *ids: (jnp.minimum((ids[row_axis] + 1) * per, last), colblk))
    return prev, nxt


def _merge_kernel(f_ref, cb_ref, cc_ref, cx_ref, ccp_ref, cxp_ref, ccn_ref, cxn_ref, a_ref,
                  gf_ref, gc_ref, ga_ref, cw_ref, wbf_ref, wbc_ref, wba_ref, o_ref, z_ref, *, tps, nchunk):
    i = pl.program_id(0)
    tm = cc_ref.shape[0]
    h = BF16_SUBLANES
    first = (i % tps) == 0
    last = (i % tps) == tps - 1
    zp = ccp_ref[...].astype(F32) * cxp_ref[...].astype(F32)
    zn = ccn_ref[...].astype(F32) * cxn_ref[...].astype(F32)
    z_ref[0:h, :] = jnp.where(first, 0.0, zp)
    z_ref[h:h + tm, :] = cc_ref[...].astype(F32) * cx_ref[...].astype(F32)
    z_ref[h + tm:2 * h + tm, :] = jnp.where(last, 0.0, zn)
    cv = (cb_ref[...].astype(F32) * _conv3_rows(z_ref, cw_ref[...], tm)).astype(BF16)
    f = f_ref[...]
    a = a_ref[...]
    cn = D_MODEL // nchunk
    for n in range(nchunk):
        cols = slice(n * cn, (n + 1) * cn)
        y = gf_ref[:, cols].astype(F32) * jnp.dot(f, wbf_ref[:, cols], preferred_element_type=F32)
        y += gc_ref[:, cols].astype(F32) * jnp.dot(cv, wbc_ref[:, cols], preferred_element_type=F32)
        y += ga_ref[:, cols].astype(F32) * jnp.dot(a, wba_ref[:, cols], preferred_element_type=F32)
        o_ref[:, cols] = y.astype(o_ref.dtype)


def _merge(p, fmix, attn, conv_w, w_bf, w_bc, w_ba, layer, tm, seq_len):
    m = p.shape[0]
    d = D_MODEL
    cw = CONV_W
    tps = seq_len // tm
    blk = lambda width, c: pl.BlockSpec((tm, width), lambda i: (i, c))
    ccp, ccn = _halo_specs(tm, cw, 2, m)
    cxp, cxn = _halo_specs(tm, cw, 3, m)
    gate0 = P_GATE_OFF // d
    full = lambda r, c: pl.BlockSpec((None, r, c), lambda i: (layer, 0, 0), pipeline_mode=pl.Buffered(1))
    return pl.pallas_call(
        functools.partial(_merge_kernel, tps=tps, nchunk=4),
        grid=(m // tm,),
        in_specs=[
            blk(FOURIER_W, 0),
            blk(cw, 1), blk(cw, 2), blk(cw, 3),
            ccp, cxp, ccn, cxn,
            blk(ATTN_V_W, 0),
            pl.BlockSpec((tm, d), lambda i: (i, gate0)),
            pl.BlockSpec((tm, d), lambda i: (i, gate0 + 1)),
            pl.BlockSpec((tm, d), lambda i: (i, gate0 + 2)),
            full(3, cw), full(FOURIER_W, d), full(cw, d), full(ATTN_V_W, d),
        ],
        out_specs=pl.BlockSpec((tm, d), lambda i: (i, 0)),
        out_shape=jax.ShapeDtypeStruct((m, d), BF16),
        scratch_shapes=[pltpu.VMEM((tm + 2 * BF16_SUBLANES, cw), F32)],
        compiler_params=_cparams("arbitrary"),
        name="branch_merge",
    )(fmix, p, p, p, p, p, p, p, attn, p, p, p, conv_w, w_bf, w_bc, w_ba)


def _outproj_kernel(m_ref, w_ref, h_ref, gt_ref, g_ref, sh_ref, sc_ref, ho_ref, xo_ref):
    y = jnp.dot(m_ref[...], w_ref[...], preferred_element_type=F32)
    h = h_ref[...] + gt_ref[...] * y
    ho_ref[...] = h
    xo_ref[...] = (_rms(h, g_ref[...]) * (1.0 + sc_ref[...]) + sh_ref[...]).astype(xo_ref.dtype)


def _out_proj(merged, w_out, layer, h, g2, mods, mod_row, tm):
    m, d = h.shape
    return pl.pallas_call(
        _outproj_kernel,
        grid=(m // tm,),
        in_specs=[
            pl.BlockSpec((tm, d), lambda i: (i, 0)),
            pl.BlockSpec((None, d, d), lambda i: (layer, 0, 0), pipeline_mode=pl.Buffered(1)),
            pl.BlockSpec((tm, d), lambda i: (i, 0)),
            _mod_spec(2, mod_row, tm),
            pl.BlockSpec((1, d), lambda i: (0, 0)),
            _mod_spec(3, mod_row, tm),
            _mod_spec(4, mod_row, tm),
        ],
        out_specs=[pl.BlockSpec((tm, d), lambda i: (i, 0)), pl.BlockSpec((tm, d), lambda i: (i, 0))],
        out_shape=[jax.ShapeDtypeStruct((m, d), F32), jax.ShapeDtypeStruct((m, d), BF16)],
        compiler_params=_cparams("arbitrary"),
        name="out_proj",
    )(merged, w_out, h, mods, g2.reshape(1, d), mods, mods)


def _ffn_up_kernel(x_ref, xp_ref, xn_ref, wa_ref, wv_ref, ca_ref, cv_ref, o_ref,
                   wab_ref, wvb_ref, xh_ref, ua_ref, uv_ref, *, tps, nseq):
    i = pl.program_id(1)
    tm = x_ref.shape[0]
    hl = BF16_SUBLANES
    seq = tm // nseq

    @pl.when(i == 0)
    def _():
        wab_ref[...] = wa_ref[...].astype(BF16)
        wvb_ref[...] = wv_ref[...].astype(BF16)

    base = [hl + s * (seq + hl) for s in range(nseq)]
    zeros = jnp.zeros((hl, x_ref.shape[1]), BF16)
    if nseq == 1:
        first = (i % tps) == 0
        last = (i % tps) == tps - 1
        xh_ref[0:hl, :] = jnp.where(first, zeros, xp_ref[...])
        xh_ref[hl + tm:2 * hl + tm, :] = jnp.where(last, zeros, xn_ref[...])
    else:
        for s in range(nseq + 1):
            xh_ref[s * (seq + hl):s * (seq + hl) + hl, :] = zeros
    for s in range(nseq):
        xh_ref[base[s]:base[s] + seq, :] = x_ref[s * seq:(s + 1) * seq, :]
    xh = xh_ref[...]
    ua_ref[...] = jnp.dot(xh, wab_ref[...], preferred_element_type=F32)
    uv_ref[...] = jnp.dot(xh, wvb_ref[...], preferred_element_type=F32)
    for s in range(nseq):
        a = _conv3_rows(ua_ref, ca_ref[...], seq, base[s])
        v = _conv3_rows(uv_ref, cv_ref[...], seq, base[s])
        o_ref[s * seq:(s + 1) * seq, :] = (a * jax.nn.sigmoid(a) * v).astype(o_ref.dtype)


def _ffn_down_kernel(a_ref, w_ref, h_ref, gt_ref, o_ref, wb_ref):
    @pl.when(pl.program_id(1) == 0)
    def _():
        wb_ref[...] = w_ref[...].astype(BF16)

    y = jnp.dot(a_ref[...], wb_ref[...], preferred_element_type=F32)
    o_ref[...] = h_ref[...] + gt_ref[...] * y


def _ffn(xm, w_up, conv_w, w_down, layer, h, mods, mod_row, seq_len):
    m, d = xm.shape
    tf = FFN_TF
    nj = D_FF // tf
    tm = FFN_UP_TM if seq_len >= FFN_UP_TM else min(m, FFN_UP_TM)
    tps = max(seq_len // tm, 1)
    nseq = max(tm // seq_len, 1)
    stage_rows = nseq * (tm // nseq + BF16_SUBLANES) + BF16_SUBLANES
    xp, xn = _halo_specs(tm, d, 0, m, row_axis=1)
    act = pl.pallas_call(
        functools.partial(_ffn_up_kernel, tps=tps, nseq=nseq),
        grid=(nj, m // tm),
        in_specs=[
            pl.BlockSpec((tm, d), lambda j, i: (i, 0)),
            xp, xn,
            pl.BlockSpec((None, d, tf), lambda j, i: (layer, 0, j)),
            pl.BlockSpec((None, d, tf), lambda j, i: (layer, 0, nj + j)),
            pl.BlockSpec((None, 3, tf), lambda j, i: (layer, 0, j)),
            pl.BlockSpec((None, 3, tf), lambda j, i: (layer, 0, nj + j)),
        ],
        out_specs=pl.BlockSpec((tm, tf), lambda j, i: (i, j)),
        out_shape=jax.ShapeDtypeStruct((m, D_FF), BF16),
        scratch_shapes=[
            pltpu.VMEM((d, tf), BF16),
            pltpu.VMEM((d, tf), BF16),
            pltpu.VMEM((stage_rows, d), BF16),
            pltpu.VMEM((stage_rows, tf), F32),
            pltpu.VMEM((stage_rows, tf), F32),
        ],
        compiler_params=_cparams("arbitrary", "arbitrary"),
        name="ffn_up",
    )(xm, xm, xm, w_up, w_up, conv_w, conv_w)

    tn = FFN_DOWN_TN
    tmd = min(m, FFN_DOWN_TM)
    gate_row = lambda n, i: mod_row(i * tmd)
    return pl.pallas_call(
        _ffn_down_kernel,
        grid=(d // tn, m // tmd),
        in_specs=[
            pl.BlockSpec((tmd, D_FF), lambda n, i: (i, 0)),
            pl.BlockSpec((None, D_FF, tn), lambda n, i: (layer, 0, n)),
            pl.BlockSpec((tmd, tn), lambda n, i: (i, n)),
            pl.BlockSpec((None, None, 1, tn), lambda n, i: (gate_row(n, i), 5, 0, n)),
        ],
        out_specs=pl.BlockSpec((tmd, tn), lambda n, i: (i, n)),
        out_shape=jax.ShapeDtypeStruct((m, d), F32),
        scratch_shapes=[pltpu.VMEM((D_FF, tn), BF16)],
        compiler_params=_cparams("arbitrary", "arbitrary"),
        name="ffn_down",
    )(act, w_down, h, mods)


def _rope_tables(length):
    n_freq = HEAD_DIM // 4
    pos = jnp.arange(length)
    row = (pos // GRID_W).astype(F32)
    col = (pos % GRID_W).astype(F32)
    inv = ROPE_THETA ** (-(2.0 * jnp.arange(n_freq, dtype=F32)) / (HEAD_DIM // 2))
    ang_r, ang_c = row[:, None] * inv, col[:, None] * inv
    ang = jnp.concatenate([ang_r, ang_r, ang_c, ang_c], axis=1)
    cos = jnp.tile(jnp.cos(ang), (1, 2))
    sin = jnp.tile(jnp.sin(ang), (1, 2))
    is_x1 = (jnp.arange(LANES) % (2 * n_freq)) < n_freq
    return cos, jnp.where(is_x1, -sin, 0.0), jnp.where(is_x1, 0.0, sin)


def _dft_tables(n, tr):
    m = jnp.arange(n, dtype=jnp.int32)

    def cos_sin(rows):
        ang = ((rows[:, None] * m[None, :]) % n).astype(F32) * (2.0 * math.pi / n)
        return jnp.cos(ang), jnp.sin(ang)

    rc, rs = cos_sin(jnp.arange(n // tr, dtype=jnp.int32) * tr)
    g = 1 << ((tr.bit_length() - 1 + 1) // 2)
    assert tr % g == 0
    ca, sa = cos_sin(jnp.arange(tr // g, dtype=jnp.int32) * g)
    cb, sb = cos_sin(jnp.arange(g, dtype=jnp.int32))
    c0 = (ca[:, None, :] * cb[None, :, :] - sa[:, None, :] * sb[None, :, :]).reshape(tr, n)
    s0 = (sa[:, None, :] * cb[None, :, :] + ca[:, None, :] * sb[None, :, :]).reshape(tr, n)
    return c0.astype(BF16), s0.astype(BF16), rc, rs


def _channel_dft():
    c = np.arange(FOURIER_GROUP_W)
    ang = 2.0 * np.pi * ((c[:, None] * c[None, :]) % FOURIER_GROUP_W) / FOURIER_GROUP_W
    return jnp.asarray(np.concatenate([np.cos(ang), np.sin(ang)], axis=1), dtype=BF16)


def _mixer(h, xm, seq_len, rope_tabs, dft, ctx_kv, w, mods, mod_row, lam_init, tm, tq,
           proj_cols=(0, PROJ_TILES)):
    m = h.shape[0]
    b = m // seq_len
    p = _in_proj(xm, w["w_in"], w["layer"], rope_tabs, min(m, PROJ_TM), proj_cols[0], proj_cols[1], seq_len)
    if proj_cols[1] != PROJ_TILES:
        return None, p
    p3 = p.reshape(b, seq_len, N_IN)
    fmix = _fourier(p3, w["cs"], dft, min(seq_len, FOURIER_TR)).reshape(m, FOURIER_W)
    if ctx_kv is None:
        attn = _attention(p3, P_Q_OFF // LANES, p3, P_K_OFF // LANES, P_V_OFF // LANES, None,
                          w["lambdas"], w["subln_g"], lam_init, tq)
    else:
        c3, kcblk, vcblk = ctx_kv
        attn = _attention(p3, P_Q_OFF // LANES, c3, kcblk, vcblk, p3,
                          w["lambdas"], w["subln_g"], lam_init, tq)
    merged = _merge(p, fmix, attn.reshape(m, ATTN_V_W), w["conv_mix_w"], w["w_br_fourier"],
                    w["w_br_conv"], w["w_br_attn"], w["layer"], tm, seq_len)
    h_mid, xm2 = _out_proj(merged, w["w_out"], w["layer"], h, w["g_norm2"], mods, mod_row, tm)
    h_new = _ffn(xm2, w["w_ffn_up"], w["ffn_conv_w"], w["w_ffn_down"], w["layer"], h_mid, mods, mod_row, seq_len)
    return h_new, p


def kernel(x, c, ctx, c_ctx, w_mod, b_mod, g_norm1, g_norm2, w_in, conv_mix_w, lambdas, subln_g,
           w_br_fourier, w_br_conv, w_br_attn, w_out, w_ffn_up, ffn_conv_w, w_ffn_down, g_final):
    b, seq, d = x.shape
    n_ctx = ctx.shape[1]
    depth = w_mod.shape[0]
    assert d == D_MODEL and b + 1 <= MOD_ROWS

    wb = lambda a: a.astype(BF16)
    w_bf, w_bc, w_ba, w_o = wb(w_br_fourier), wb(w_br_conv), wb(w_br_attn), wb(w_out)

    rope_tabs = _rope_tables(seq)
    dft_l = _dft_tables(seq, min(seq, FOURIER_TR))
    dft_c = _dft_tables(n_ctx, min(n_ctx, FOURIER_TR))
    cs = _channel_dft()

    cvec = jnp.zeros((MOD_ROWS, d), F32).at[:b].set(c).at[b].set(c_ctx)
    mods_all = _modulation(cvec, w_mod, b_mod).reshape(depth, MOD_ROWS, 6, 1, d)

    tm_l = min(512, seq)
    tm_c = n_ctx
    tq = min(512, seq)
    lat_row = lambda r: r // seq
    ctx_row = lambda r: b

    h = x.reshape(b * seq, d)
    hc = ctx.reshape(b * n_ctx, d)
    for i in range(depth):
        last = i == depth - 1
        lam_init = 0.8 - 0.6 * math.exp(-0.3 * i)
        mods = mods_all[i]
        w = dict(w_in=w_in, layer=i, cs=cs, lambdas=lambdas[i], subln_g=subln_g[i], conv_mix_w=conv_mix_w,
                 w_br_fourier=w_bf, w_br_conv=w_bc, w_br_attn=w_ba, w_out=w_o,
                 g_norm2=g_norm2[i], w_ffn_up=w_ffn_up, ffn_conv_w=ffn_conv_w, w_ffn_down=w_ffn_down)

        xc = _norm_modulate(hc, g_norm1[i], mods, ctx_row, tm_c)
        xl = _norm_modulate(h, g_norm1[i], mods, lat_row, tm_l)
        if last:
            _, pc = _mixer(hc, xc, n_ctx, None, dft_c, None, w, mods, ctx_row, lam_init, tm_c, n_ctx,
                           proj_cols=(PROJ_K_TILE, 2))
            ctx_kv = (pc.reshape(b, n_ctx, 2 * PROJ_TN), 0, PROJ_TN // LANES)
            hc_new = hc
        else:
            hc_new, pc = _mixer(hc, xc, n_ctx, None, dft_c, None, w, mods, ctx_row, lam_init, tm_c, n_ctx)
            ctx_kv = (pc.reshape(b, n_ctx, N_IN), P_K_OFF // LANES, P_V_OFF // LANES)
        h, _ = _mixer(h, xl, seq, rope_tabs, dft_l, ctx_kv, w, mods, lat_row, lam_init, tm_l, tq)
        hc = hc_new
    return _final_norm(h, g_final, tm_l).reshape(b, seq, d)
```

```python
import functools
import math

import numpy as np
import jax
import jax.numpy as jnp
from jax import lax
from jax.experimental import pallas as pl
from jax.experimental.pallas import tpu as pltpu

F32 = jnp.float32
BF16 = jnp.bfloat16

D_MODEL = 2048
GRID_W = 64
FOURIER_GROUPS = 4
FOURIER_GROUP_W = D_MODEL // 16
FOURIER_W = FOURIER_GROUPS * FOURIER_GROUP_W
CONV_W = D_MODEL // 4
N_HEADS = 8
HEAD_DIM = D_MODEL // (4 * N_HEADS)
VAL_DIM = 2 * HEAD_DIM
ATTN_QK_W = N_HEADS * 2 * HEAD_DIM
ATTN_V_W = N_HEADS * VAL_DIM
ROPE_THETA = 10000.0
ATTN_SCALE = HEAD_DIM ** -0.5
SUBLN_EPS = 1e-5
Q_OFF = FOURIER_W + 3 * CONV_W
K_OFF = Q_OFF + ATTN_QK_W
V_OFF = K_OFF + ATTN_QK_W
V_END = V_OFF + ATTN_V_W
N_IN = V_END + 3 * D_MODEL
D_FF = ((8 * D_MODEL // 3 + 255) // 256) * 256
EPS = 1e-6

LANES = 128
MXU_COLS = 256
BF16_SUBLANES = 16
VMEM_LIMIT = 56 * 1024 * 1024
MOD_ROWS = 8

PROJ_TN = 1024
PROJ_TM = 1024
P_GATE_OFF = Q_OFF
P_Q_OFF = P_GATE_OFF + 3 * D_MODEL
P_K_OFF = P_Q_OFF + ATTN_QK_W
P_V_OFF = P_K_OFF + ATTN_QK_W
PROJ_GATE_TILE = P_GATE_OFF // PROJ_TN
PROJ_Q_TILE = P_Q_OFF // PROJ_TN
PROJ_K_TILE = P_K_OFF // PROJ_TN
PROJ_TILES = N_IN // PROJ_TN
ATTN_TK = 512
ATTN_SCORES_AHEAD = 1
FOURIER_TR = 512
MERGE_TM = 256
FFN_TF = 512
FFN_UP_TM = 1024
FFN_DOWN_TN = 512
FFN_DOWN_TM = 512


def _cparams(*sem):
    return pltpu.CompilerParams(dimension_semantics=sem, vmem_limit_bytes=VMEM_LIMIT)


def _mod_kernel(c_ref, w_ref, b_ref, o_ref):
    c = c_ref[...]
    s = c * jax.nn.sigmoid(c)
    s_hi = s.astype(BF16)
    s_lo = (s - s_hi.astype(F32)).astype(BF16)
    w = w_ref[...]
    w_hi = w.astype(BF16)
    w_lo = (w - w_hi.astype(F32)).astype(BF16)
    y = jnp.dot(s_hi, w_hi, preferred_element_type=F32)
    y += jnp.dot(s_lo, w_hi, preferred_element_type=F32)
    y += jnp.dot(s_hi, w_lo, preferred_element_type=F32)
    o_ref[...] = y + b_ref[...]


def _modulation(cvec, w_mod, b_mod):
    depth, d, n = w_mod.shape
    tn = 1024
    return pl.pallas_call(
        _mod_kernel,
        grid=(depth, n // tn),
        in_specs=[
            pl.BlockSpec((MOD_ROWS, d), lambda l, j: (0, 0)),
            pl.BlockSpec((None, d, tn), lambda l, j: (l, 0, j)),
            pl.BlockSpec((None, 1, tn), lambda l, j: (l, 0, j)),
        ],
        out_specs=pl.BlockSpec((None, MOD_ROWS, tn), lambda l, j: (l, 0, j)),
        out_shape=jax.ShapeDtypeStruct((depth, MOD_ROWS, n), F32),
        compiler_params=_cparams("arbitrary", "arbitrary"),
        name="modulation",
    )(cvec, w_mod, b_mod.reshape(depth, 1, n))


def _mod_spec(which, mod_row, tm):
    return pl.BlockSpec((None, None, 1, D_MODEL), lambda i, *_: (mod_row(i * tm), which, 0, 0))


def _rms(x, g):
    return x * lax.rsqrt(jnp.mean(x * x, axis=-1, keepdims=True) + EPS) * g


def _norm_mod_kernel(h_ref, g_ref, sh_ref, sc_ref, o_ref):
    y = _rms(h_ref[...], g_ref[...])
    o_ref[...] = (y * (1.0 + sc_ref[...]) + sh_ref[...]).astype(o_ref.dtype)


def _norm_kernel(h_ref, g_ref, o_ref):
    o_ref[...] = _rms(h_ref[...], g_ref[...]).astype(o_ref.dtype)


def _norm_modulate(h, g, mods, mod_row, tm):
    m, d = h.shape
    return pl.pallas_call(
        _norm_mod_kernel,
        grid=(m // tm,),
        in_specs=[
            pl.BlockSpec((tm, d), lambda i: (i, 0)),
            pl.BlockSpec((1, d), lambda i: (0, 0)),
            _mod_spec(0, mod_row, tm),
            _mod_spec(1, mod_row, tm),
        ],
        out_specs=pl.BlockSpec((tm, d), lambda i: (i, 0)),
        out_shape=jax.ShapeDtypeStruct((m, d), BF16),
        compiler_params=_cparams("arbitrary"),
        name="norm_modulate",
    )(h, g.reshape(1, d), mods, mods)


def _final_norm(h, g, tm):
    m, d = h.shape
    return pl.pallas_call(
        _norm_kernel,
        grid=(m // tm,),
        in_specs=[pl.BlockSpec((tm, d), lambda i: (i, 0)), pl.BlockSpec((1, d), lambda i: (0, 0))],
        out_specs=pl.BlockSpec((tm, d), lambda i: (i, 0)),
        out_shape=jax.ShapeDtypeStruct((m, d), F32),
        compiler_params=_cparams("arbitrary"),
        name="final_norm",
    )(h, g.reshape(1, d))


def _proj_kernel(*refs, col0, rope):
    if rope:
        x_ref, w_ref, cos_ref, sin_up_ref, sin_dn_ref, o_ref, wb_ref = refs
    else:
        x_ref, w_ref, o_ref, wb_ref = refs
    j = pl.program_id(0) + col0

    @pl.when(pl.program_id(1) == 0)
    def _():
        wb_ref[...] = w_ref[...].astype(BF16)

    is_q = j == PROJ_Q_TILE
    is_k = j == PROJ_K_TILE
    is_gate = (j >= PROJ_GATE_TILE) & (j < PROJ_Q_TILE)
    x = x_ref[...]

    def column_chunks(epilogue):
        for c in range(PROJ_TN // MXU_COLS):
            cols = slice(c * MXU_COLS, (c + 1) * MXU_COLS)
            acc = jnp.dot(x, wb_ref[:, cols], preferred_element_type=F32)
            o_ref[:, cols] = epilogue(acc).astype(o_ref.dtype)

    def rotary(acc):
        qscale = jnp.where(is_q, ATTN_SCALE * math.log2(math.e), 1.0)
        half = HEAD_DIM // 4
        out = []
        for hd in range(MXU_COLS // LANES):
            a = acc[:, hd * LANES:(hd + 1) * LANES]
            if rope:
                a = (a * cos_ref[...] + pltpu.roll(a, LANES - half, 1) * sin_up_ref[...]
                     + pltpu.roll(a, half, 1) * sin_dn_ref[...])
            out.append(a * qscale)
        return jnp.concatenate(out, axis=1)

    pl.when(is_q | is_k)(lambda: column_chunks(rotary))
    pl.when(is_gate)(lambda: column_chunks(jax.nn.sigmoid))
    pl.when(jnp.logical_not(is_q | is_k | is_gate))(lambda: column_chunks(lambda acc: acc))


def _in_proj(xm, w_in, layer, rope_tabs, tm, col0, ncols, seq_len):
    m, d = xm.shape
    tn = PROJ_TN
    n_plain = PROJ_GATE_TILE
    n_gate = PROJ_Q_TILE - PROJ_GATE_TILE
    n_qkv = PROJ_TILES - PROJ_Q_TILE

    def w_tile(j):
        j = j + col0
        return jnp.where(j < n_plain, j, jnp.where(j < n_plain + n_gate, j + n_qkv, j - n_gate))

    in_specs = [
        pl.BlockSpec((tm, d), lambda j, i: (i, 0)),
        pl.BlockSpec((None, d, tn), lambda j, i: (layer, 0, w_tile(j))),
    ]
    args = [xm, w_in]
    rope = rope_tabs is not None
    if rope:
        tps = seq_len // tm
        in_specs += [pl.BlockSpec((tm, LANES), lambda j, i: (i % tps, 0))] * 3
        args += list(rope_tabs)
    return pl.pallas_call(
        functools.partial(_proj_kernel, col0=col0, rope=rope),
        grid=(ncols, m // tm),
        in_specs=in_specs,
        out_specs=pl.BlockSpec((tm, tn), lambda j, i: (i, j)),
        out_shape=jax.ShapeDtypeStruct((m, ncols * tn), BF16),
        scratch_shapes=[pltpu.VMEM((d, tn), BF16)],
        compiler_params=_cparams("arbitrary", "arbitrary"),
        name="in_proj",
    )(*args)


def _fourier_kernel(f_ref, cs_ref, c0_ref, s0_ref, rc_ref, rs_ref, o_ref, uc_ref, us_ref, *, chunk, scale):
    i = pl.program_id(1)
    seq = f_ref.shape[0]
    gw = FOURIER_GROUP_W

    @pl.when(i == 0)
    def _():
        for r in range(seq // chunk):
            rows = slice(r * chunk, (r + 1) * chunk)
            for g in range(FOURIER_GROUPS):
                cols = slice(g * gw, (g + 1) * gw)
                y = jnp.dot(f_ref[rows, cols], cs_ref[...], preferred_element_type=F32)
                uc_ref[rows, cols] = y[:, :gw].astype(BF16)
                us_ref[rows, cols] = y[:, gw:].astype(BF16)

    c0, s0 = c0_ref[...].astype(F32), s0_ref[...].astype(F32)
    rc, rs = rc_ref[pl.ds(i, 1), :], rs_ref[pl.ds(i, 1), :]
    cos_t = (c0 * rc - s0 * rs).astype(BF16)
    sin_t = (s0 * rc + c0 * rs).astype(BF16)
    y = jnp.dot(cos_t, uc_ref[...], preferred_element_type=F32)
    y -= jnp.dot(sin_t, us_ref[...], preferred_element_type=F32)
    o_ref[...] = (y * scale).astype(o_ref.dtype)


def _fourier(p3, cs, dft, tr):
    b, seq, _ = p3.shape
    chunk = min(seq, 1024)
    scale = 1.0 / math.sqrt(seq * FOURIER_GROUP_W)
    c0, s0, rc, rs = dft
    full = lambda a: pl.BlockSpec(a.shape, lambda bi, i: (0, 0), pipeline_mode=pl.Buffered(1))
    return pl.pallas_call(
        functools.partial(_fourier_kernel, chunk=chunk, scale=scale),
        grid=(b, seq // tr),
        in_specs=[
            pl.BlockSpec((None, seq, FOURIER_W), lambda bi, i: (bi, 0, 0)),
            pl.BlockSpec((FOURIER_GROUP_W, 2 * FOURIER_GROUP_W), lambda bi, i: (0, 0)),
            full(c0), full(s0), full(rc), full(rs),
        ],
        out_specs=pl.BlockSpec((None, tr, FOURIER_W), lambda bi, i: (bi, i, 0)),
        out_shape=jax.ShapeDtypeStruct((b, seq, FOURIER_W), BF16),
        scratch_shapes=[pltpu.VMEM((seq, FOURIER_W), BF16), pltpu.VMEM((seq, FOURIER_W), BF16)],
        compiler_params=_cparams("arbitrary", "arbitrary"),
        name="fourier",
    )(p3, cs, c0, s0, rc, rs)


def _attn_kernel(*refs, tq, tk, n_lat, lam_init):
    if n_lat:
        lam_ref, q_ref, kc_ref, vc_ref, kl_ref, vl_ref, g_ref, o_ref = refs
    else:
        lam_ref, q_ref, kc_ref, vc_ref, g_ref, o_ref = refs
        kl_ref = vl_ref = None
    n_ctx = kc_ref.shape[0]
    q = q_ref[...]
    lane = lax.broadcasted_iota(jnp.int32, (1, LANES), 1)
    comp0 = lane < HEAD_DIM
    zero = jnp.zeros_like(q)
    qq = jnp.concatenate([jnp.where(comp0, q, zero), jnp.where(comp0, zero, q)], axis=0)
    chunks = [(kc_ref, vc_ref, 0, n_ctx)] + [(kl_ref, vl_ref, c * tk, tk) for c in range(n_lat)]

    def lane_fold(x, op):
        part = x[:, 0:LANES]
        for c in range(1, x.shape[1] // LANES):
            part = op(part, x[:, c * LANES:(c + 1) * LANES])
        return part

    def scores(c):
        kr, _, r0, sz = chunks[c]
        return lax.dot_general(qq, kr[r0:r0 + sz, :], (((1,), (1,)), ((), ())), preferred_element_type=F32)

    m = l = acc = None
    ahead = ATTN_SCORES_AHEAD
    pending = [scores(c) for c in range(min(ahead, len(chunks)))]
    for c, (_, vr, r0, sz) in enumerate(chunks):
        s = pending.pop(0)
        if c + ahead < len(chunks):
            pending.append(scores(c + ahead))
        mc = jnp.max(lane_fold(s, jnp.maximum), axis=-1, keepdims=True)
        m_new = mc if m is None else jnp.maximum(m, mc)
        p = jnp.exp2(s - m_new)
        lc = lane_fold(p, jnp.add)
        y = jnp.dot(p.astype(BF16), vr[r0:r0 + sz, :], preferred_element_type=F32)
        if m is None:
            l, acc = lc, y
        else:
            alpha = jnp.exp2(m - m_new)
            l = alpha * l + lc
            acc = alpha * acc + y
        m = m_new
    o = acc / jnp.sum(l, axis=-1, keepdims=True)
    lf = lam_ref[...]
    lam = (jnp.exp(jnp.sum(lf[0:1] * lf[1:2], axis=-1, keepdims=True))
           - jnp.exp(jnp.sum(lf[2:3] * lf[3:4], axis=-1, keepdims=True)) + lam_init)
    o = o[:tq] - lam * o[tq:]
    o = o * lax.rsqrt(jnp.mean(o * o, axis=-1, keepdims=True) + SUBLN_EPS)
    o_ref[...] = (o * g_ref[...] * (1.0 - lam_init)).astype(o_ref.dtype)


def _attention(q3, qblk, c3, kcblk, vcblk, l3, lam, subln_g, lam_init, tq):
    b, lq, _ = q3.shape
    n_ctx = c3.shape[1]
    tk = ATTN_TK
    in_specs = [
        pl.BlockSpec((4, HEAD_DIM), lambda bi, h, i: (0, 0)),
        pl.BlockSpec((None, tq, LANES), lambda bi, h, i: (bi, i, qblk + h)),
        pl.BlockSpec((None, n_ctx, LANES), lambda bi, h, i: (bi, 0, kcblk + h)),
        pl.BlockSpec((None, n_ctx, LANES), lambda bi, h, i: (bi, 0, vcblk + h)),
    ]
    args = [lam, q3, c3, c3]
    n_lat = 0
    if l3 is not None:
        seq = l3.shape[1]
        n_lat = seq // min(tk, seq)
        tk = seq // n_lat
        in_specs += [
            pl.BlockSpec((None, seq, LANES), lambda bi, h, i: (bi, 0, P_K_OFF // LANES + h)),
            pl.BlockSpec((None, seq, LANES), lambda bi, h, i: (bi, 0, P_V_OFF // LANES + h)),
        ]
        args += [l3, l3]
    in_specs.append(pl.BlockSpec((1, VAL_DIM), lambda bi, h, i: (0, 0)))
    args.append(subln_g.reshape(1, VAL_DIM))
    return pl.pallas_call(
        functools.partial(_attn_kernel, tq=tq, tk=tk, n_lat=n_lat, lam_init=lam_init),
        grid=(b, N_HEADS, lq // tq),
        in_specs=in_specs,
        out_specs=pl.BlockSpec((None, tq, LANES), lambda bi, h, i: (bi, i, h)),
        out_shape=jax.ShapeDtypeStruct((b, lq, ATTN_V_W), BF16),
        compiler_params=_cparams("arbitrary", "arbitrary", "arbitrary"),
        name="diff_attention",
    )(*args)


def _conv3_rows(buf_ref, w, tm, base=BF16_SUBLANES):
    return (buf_ref[base - 1:base - 1 + tm, :] * w[0:1] + buf_ref[base:base + tm, :] * w[1:2]
            + buf_ref[base + 1:base + 1 + tm, :] * w[2:3])


def _halo_specs(tm, width, colblk, m, row_axis=0):
    per = tm // BF16_SUBLANES
    last = m // BF16_SUBLANES - 1
    prev = pl.BlockSpec((BF16_SUBLANES, width),
                        lambda *ids: (jnp.maximum(ids[row_axis] * per - 1, 0), colblk))
    nxt = pl.BlockSpec((BF16_SUBLANES, width),
                       lambda *ids: (jnp.minimum((ids[row_axis] + 1) * per, last), colblk))
    return prev, nxt


def _merge_kernel(f_ref, cb_ref, cc_ref, cx_ref, ccp_ref, cxp_ref, ccn_ref, cxn_ref, a_ref,
                  gf_ref, gc_ref, ga_ref, cw_ref, wbf_ref, wbc_ref, wba_ref, wo_ref,
                  h_ref, gt_ref, g2_ref, sh_ref, sc_ref, ho_ref, xo_ref, z_ref, mg_ref, *, tps, nchunk):
    i = pl.program_id(0)
    tm = cc_ref.shape[0]
    h = BF16_SUBLANES
    first = (i % tps) == 0
    last = (i % tps) == tps - 1
    zp = ccp_ref[...].astype(F32) * cxp_ref[...].astype(F32)
    zn = ccn_ref[...].astype(F32) * cxn_ref[...].astype(F32)
    z_ref[0:h, :] = jnp.where(first, 0.0, zp)
    z_ref[h:h + tm, :] = cc_ref[...].astype(F32) * cx_ref[...].astype(F32)
    z_ref[h + tm:2 * h + tm, :] = jnp.where(last, 0.0, zn)
    cv = (cb_ref[...].astype(F32) * _conv3_rows(z_ref, cw_ref[...], tm)).astype(BF16)
    f = f_ref[...]
    a = a_ref[...]
    cn = D_MODEL // nchunk
    for n in range(nchunk):
        cols = slice(n * cn, (n + 1) * cn)
        y = gf_ref[:, cols].astype(F32) * jnp.dot(f, wbf_ref[:, cols], preferred_element_type=F32)
        y += gc_ref[:, cols].astype(F32) * jnp.dot(cv, wbc_ref[:, cols], preferred_element_type=F32)
        y += ga_ref[:, cols].astype(F32) * jnp.dot(a, wba_ref[:, cols], preferred_element_type=F32)
        mg_ref[:, cols] = y.astype(BF16)
    y = jnp.dot(mg_ref[...], wo_ref[...], preferred_element_type=F32)
    h_new = h_ref[...] + gt_ref[...] * y
    ho_ref[...] = h_new
    xo_ref[...] = (_rms(h_new, g2_ref[...]) * (1.0 + sc_ref[...]) + sh_ref[...]).astype(xo_ref.dtype)


def _merge_out(p, fmix, attn, conv_w, w_bf, w_bc, w_ba, w_out, layer, h, g2, mods, mod_row, tm, seq_len):
    m = p.shape[0]
    d = D_MODEL
    cw = CONV_W
    tps = seq_len // tm
    blk = lambda width, c: pl.BlockSpec((tm, width), lambda i: (i, c))
    ccp, ccn = _halo_specs(tm, cw, 2, m)
    cxp, cxn = _halo_specs(tm, cw, 3, m)
    gate0 = P_GATE_OFF // d
    full = lambda r, c: pl.BlockSpec((None, r, c), lambda i: (layer, 0, 0), pipeline_mode=pl.Buffered(1))
    return pl.pallas_call(
        functools.partial(_merge_kernel, tps=tps, nchunk=4),
        grid=(m // tm,),
        in_specs=[
            blk(FOURIER_W, 0),
            blk(cw, 1), blk(cw, 2), blk(cw, 3),
            ccp, cxp, ccn, cxn,
            blk(ATTN_V_W, 0),
            pl.BlockSpec((tm, d), lambda i: (i, gate0)),
            pl.BlockSpec((tm, d), lambda i: (i, gate0 + 1)),
            pl.BlockSpec((tm, d), lambda i: (i, gate0 + 2)),
            full(3, cw), full(FOURIER_W, d), full(cw, d), full(ATTN_V_W, d), full(d, d),
            pl.BlockSpec((tm, d), lambda i: (i, 0)),
            _mod_spec(2, mod_row, tm),
            pl.BlockSpec((1, d), lambda i: (0, 0)),
            _mod_spec(3, mod_row, tm),
            _mod_spec(4, mod_row, tm),
        ],
        out_specs=[pl.BlockSpec((tm, d), lambda i: (i, 0)), pl.BlockSpec((tm, d), lambda i: (i, 0))],
        out_shape=[jax.ShapeDtypeStruct((m, d), F32), jax.ShapeDtypeStruct((m, d), BF16)],
        scratch_shapes=[pltpu.VMEM((tm + 2 * BF16_SUBLANES, cw), F32), pltpu.VMEM((tm, d), BF16)],
        compiler_params=_cparams("arbitrary"),
        name="merge_out",
    )(fmix, p, p, p, p, p, p, p, attn, p, p, p, conv_w, w_bf, w_bc, w_ba, w_out,
      h, mods, g2.reshape(1, d), mods, mods)


def _ffn_up_kernel(x_ref, xp_ref, xn_ref, wa_ref, wv_ref, ca_ref, cv_ref, o_ref,
                   wab_ref, wvb_ref, xh_ref, ua_ref, uv_ref, *, tps, nseq):
    i = pl.program_id(1)
    tm = x_ref.shape[0]
    hl = BF16_SUBLANES
    seq = tm // nseq

    @pl.when(i == 0)
    def _():
        wab_ref[...] = wa_ref[...].astype(BF16)
        wvb_ref[...] = wv_ref[...].astype(BF16)

    base = [hl + s * (seq + hl) for s in range(nseq)]
    zeros = jnp.zeros((hl, x_ref.shape[1]), BF16)
    if nseq == 1:
        first = (i % tps) == 0
        last = (i % tps) == tps - 1
        xh_ref[0:hl, :] = jnp.where(first, zeros, xp_ref[...])
        xh_ref[hl + tm:2 * hl + tm, :] = jnp.where(last, zeros, xn_ref[...])
    else:
        for s in range(nseq + 1):
            xh_ref[s * (seq + hl):s * (seq + hl) + hl, :] = zeros
    for s in range(nseq):
        xh_ref[base[s]:base[s] + seq, :] = x_ref[s * seq:(s + 1) * seq, :]
    xh = xh_ref[...]
    ua_ref[...] = jnp.dot(xh, wab_ref[...], preferred_element_type=F32)
    uv_ref[...] = jnp.dot(xh, wvb_ref[...], preferred_element_type=F32)
    for s in range(nseq):
        a = _conv3_rows(ua_ref, ca_ref[...], seq, base[s])
        v = _conv3_rows(uv_ref, cv_ref[...], seq, base[s])
        o_ref[s * seq:(s + 1) * seq, :] = (a * jax.nn.sigmoid(a) * v).astype(o_ref.dtype)


def _ffn_down_kernel(a_ref, w_ref, h_ref, gt_ref, o_ref, wb_ref):
    @pl.when(pl.program_id(1) == 0)
    def _():
        wb_ref[...] = w_ref[...].astype(BF16)

    y = jnp.dot(a_ref[...], wb_ref[...], preferred_element_type=F32)
    o_ref[...] = h_ref[...] + gt_ref[...] * y


def _ffn(xm, w_up, conv_w, w_down, layer, h, mods, mod_row, seq_len):
    m, d = xm.shape
    tf = FFN_TF
    nj = D_FF // tf
    tm = FFN_UP_TM if seq_len >= FFN_UP_TM else min(m, FFN_UP_TM)
    tps = max(seq_len // tm, 1)
    nseq = max(tm // seq_len, 1)
    stage_rows = nseq * (tm // nseq + BF16_SUBLANES) + BF16_SUBLANES
    xp, xn = _halo_specs(tm, d, 0, m, row_axis=1)
    act = pl.pallas_call(
        functools.partial(_ffn_up_kernel, tps=tps, nseq=nseq),
        grid=(nj, m // tm),
        in_specs=[
            pl.BlockSpec((tm, d), lambda j, i: (i, 0)),
            xp, xn,
            pl.BlockSpec((None, d, tf), lambda j, i: (layer, 0, j)),
            pl.BlockSpec((None, d, tf), lambda j, i: (layer, 0, nj + j)),
            pl.BlockSpec((None, 3, tf), lambda j, i: (layer, 0, j)),
            pl.BlockSpec((None, 3, tf), lambda j, i: (layer, 0, nj + j)),
        ],
        out_specs=pl.BlockSpec((tm, tf), lambda j, i: (i, j)),
        out_shape=jax.ShapeDtypeStruct((m, D_FF), BF16),
        scratch_shapes=[
            pltpu.VMEM((d, tf), BF16),
            pltpu.VMEM((d, tf), BF16),
            pltpu.VMEM((stage_rows, d), BF16),
            pltpu.VMEM((stage_rows, tf), F32),
            pltpu.VMEM((stage_rows, tf), F32),
        ],
        compiler_params=_cparams("arbitrary", "arbitrary"),
        name="ffn_up",
    )(xm, xm, xm, w_up, w_up, conv_w, conv_w)

    tn = FFN_DOWN_TN
    tmd = min(m, FFN_DOWN_TM)
    gate_row = lambda n, i: mod_row(i * tmd)
    return pl.pallas_call(
        _ffn_down_kernel,
        grid=(d // tn, m // tmd),
        in_specs=[
            pl.BlockSpec((tmd, D_FF), lambda n, i: (i, 0)),
            pl.BlockSpec((None, D_FF, tn), lambda n, i: (layer, 0, n)),
            pl.BlockSpec((tmd, tn), lambda n, i: (i, n)),
            pl.BlockSpec((None, None, 1, tn), lambda n, i: (gate_row(n, i), 5, 0, n)),
        ],
        out_specs=pl.BlockSpec((tmd, tn), lambda n, i: (i, n)),
        out_shape=jax.ShapeDtypeStruct((m, d), F32),
        scratch_shapes=[pltpu.VMEM((D_FF, tn), BF16)],
        compiler_params=_cparams("arbitrary", "arbitrary"),
        name="ffn_down",
    )(act, w_down, h, mods)


def _rope_tables(length):
    n_freq = HEAD_DIM // 4
    pos = jnp.arange(length)
    row = (pos // GRID_W).astype(F32)
    col = (pos % GRID_W).astype(F32)
    inv = ROPE_THETA ** (-(2.0 * jnp.arange(n_freq, dtype=F32)) / (HEAD_DIM // 2))
    ang_r, ang_c = row[:, None] * inv, col[:, None] * inv
    ang = jnp.concatenate([ang_r, ang_r, ang_c, ang_c], axis=1)
    cos = jnp.tile(jnp.cos(ang), (1, 2))
    sin = jnp.tile(jnp.sin(ang), (1, 2))
    is_x1 = (jnp.arange(LANES) % (2 * n_freq)) < n_freq
    return cos, jnp.where(is_x1, -sin, 0.0), jnp.where(is_x1, 0.0, sin)


def _dft_tables(n, tr):
    m = jnp.arange(n, dtype=jnp.int32)

    def cos_sin(rows):
        ang = ((rows[:, None] * m[None, :]) % n).astype(F32) * (2.0 * math.pi / n)
        return jnp.cos(ang), jnp.sin(ang)

    rc, rs = cos_sin(jnp.arange(n // tr, dtype=jnp.int32) * tr)
    g = 1 << ((tr.bit_length() - 1 + 1) // 2)
    assert tr % g == 0
    ca, sa = cos_sin(jnp.arange(tr // g, dtype=jnp.int32) * g)
    cb, sb = cos_sin(jnp.arange(g, dtype=jnp.int32))
    c0 = (ca[:, None, :] * cb[None, :, :] - sa[:, None, :] * sb[None, :, :]).reshape(tr, n)
    s0 = (sa[:, None, :] * cb[None, :, :] + ca[:, None, :] * sb[None, :, :]).reshape(tr, n)
    return c0.astype(BF16), s0.astype(BF16), rc, rs


def _channel_dft():
    c = np.arange(FOURIER_GROUP_W)
    ang = 2.0 * np.pi * ((c[:, None] * c[None, :]) % FOURIER_GROUP_W) / FOURIER_GROUP_W
    return jnp.asarray(np.concatenate([np.cos(ang), np.sin(ang)], axis=1), dtype=BF16)


def _mixer(h, xm, seq_len, rope_tabs, dft, ctx_kv, w, mods, mod_row, lam_init, tm, tq,
           proj_cols=(0, PROJ_TILES)):
    m = h.shape[0]
    b = m // seq_len
    p = _in_proj(xm, w["w_in"], w["layer"], rope_tabs, min(m, PROJ_TM), proj_cols[0], proj_cols[1], seq_len)
    if proj_cols[1] != PROJ_TILES:
        return None, p
    p3 = p.reshape(b, seq_len, N_IN)
    fmix = _fourier(p3, w["cs"], dft, min(seq_len, FOURIER_TR)).reshape(m, FOURIER_W)
    if ctx_kv is None:
        attn = _attention(p3, P_Q_OFF // LANES, p3, P_K_OFF // LANES, P_V_OFF // LANES, None,
                          w["lambdas"], w["subln_g"], lam_init, tq)
    else:
        c3, kcblk, vcblk = ctx_kv
        attn = _attention(p3, P_Q_OFF // LANES, c3, kcblk, vcblk, p3,
                          w["lambdas"], w["subln_g"], lam_init, tq)
    h_mid, xm2 = _merge_out(p, fmix, attn.reshape(m, ATTN_V_W), w["conv_mix_w"], w["w_br_fourier"],
                            w["w_br_conv"], w["w_br_attn"], w["w_out"], w["layer"], h, w["g_norm2"],
                            mods, mod_row, min(tm, MERGE_TM), seq_len)
    h_new = _ffn(xm2, w["w_ffn_up"], w["ffn_conv_w"], w["w_ffn_down"], w["layer"], h_mid, mods, mod_row, seq_len)
    return h_new, p


def kernel(x, c, ctx, c_ctx, w_mod, b_mod, g_norm1, g_norm2, w_in, conv_mix_w, lambdas, subln_g,
           w_br_fourier, w_br_conv, w_br_attn, w_out, w_ffn_up, ffn_conv_w, w_ffn_down, g_final):
    b, seq, d = x.shape
    n_ctx = ctx.shape[1]
    depth = w_mod.shape[0]
    assert d == D_MODEL and b + 1 <= MOD_ROWS

    wb = lambda a: a.astype(BF16)
    w_bf, w_bc, w_ba, w_o = wb(w_br_fourier), wb(w_br_conv), wb(w_br_attn), wb(w_out)

    rope_tabs = _rope_tables(seq)
    dft_l = _dft_tables(seq, min(seq, FOURIER_TR))
    dft_c = _dft_tables(n_ctx, min(n_ctx, FOURIER_TR))
    cs = _channel_dft()

    cvec = jnp.zeros((MOD_ROWS, d), F32).at[:b].set(c).at[b].set(c_ctx)
    mods_all = _modulation(cvec, w_mod, b_mod).reshape(depth, MOD_ROWS, 6, 1, d)

    tm_l = min(512, seq)
    tm_c = n_ctx
    tq = min(512, seq)
    lat_row = lambda r: r // seq
    ctx_row = lambda r: b

    h = x.reshape(b * seq, d)
    hc = ctx.reshape(b * n_ctx, d)
    for i in range(depth):
        last = i == depth - 1
        lam_init = 0.8 - 0.6 * math.exp(-0.3 * i)
        mods = mods_all[i]
        w = dict(w_in=w_in, layer=i, cs=cs, lambdas=lambdas[i], subln_g=subln_g[i], conv_mix_w=conv_mix_w,
                 w_br_fourier=w_bf, w_br_conv=w_bc, w_br_attn=w_ba, w_out=w_o,
                 g_norm2=g_norm2[i], w_ffn_up=w_ffn_up, ffn_conv_w=ffn_conv_w, w_ffn_down=w_ffn_down)

        xc = _norm_modulate(hc, g_norm1[i], mods, ctx_row, tm_c)
        xl = _norm_modulate(h, g_norm1[i], mods, lat_row, tm_l)
        if last:
            _, pc = _mixer(hc, xc, n_ctx, None, dft_c, None, w, mods, ctx_row, lam_init, tm_c, n_ctx,
                           proj_cols=(PROJ_K_TILE, 2))
            ctx_kv = (pc.reshape(b, n_ctx, 2 * PROJ_TN), 0, PROJ_TN // LANES)
            hc_new = hc
        else:
            hc_new, pc = _mixer(hc, xc, n_ctx, None, dft_c, None, w, mods, ctx_row, lam_init, tm_c, n_ctx)
            ctx_kv = (pc.reshape(b, n_ctx, N_IN), P_K_OFF // LANES, P_V_OFF // LANES)
        h, _ = _mixer(h, xl, seq, rope_tabs, dft_l, ctx_kv, w, mods, lat_row, lam_init, tm_l, tq)
        hc = hc_new
    return _final_norm(h, g_final, tm_l).reshape(b, seq, d)
```

```python
import functools
import math

import numpy as np
import jax
import jax.numpy as jnp
from jax import lax
from jax.experimental import pallas as pl
from jax.experimental.pallas import tpu as pltpu

F32 = jnp.float32
BF16 = jnp.bfloat16

D_MODEL = 2048
GRID_W = 64
FOURIER_GROUPS = 4
FOURIER_GROUP_W = D_MODEL // 16
FOURIER_W = FOURIER_GROUPS * FOURIER_GROUP_W
CONV_W = D_MODEL // 4
N_HEADS = 8
HEAD_DIM = D_MODEL // (4 * N_HEADS)
VAL_DIM = 2 * HEAD_DIM
ATTN_QK_W = N_HEADS * 2 * HEAD_DIM
ATTN_V_W = N_HEADS * VAL_DIM
ROPE_THETA = 10000.0
ATTN_SCALE = HEAD_DIM ** -0.5
SUBLN_EPS = 1e-5
Q_OFF = FOURIER_W + 3 * CONV_W
K_OFF = Q_OFF + ATTN_QK_W
V_OFF = K_OFF + ATTN_QK_W
V_END = V_OFF + ATTN_V_W
N_IN = V_END + 3 * D_MODEL
D_FF = ((8 * D_MODEL // 3 + 255) // 256) * 256
EPS = 1e-6

LANES = 128
MXU_COLS = 256
BF16_SUBLANES = 16
VMEM_LIMIT = 56 * 1024 * 1024
MOD_ROWS = 8

PROJ_TN = 1024
PROJ_TM = 1024
P_GATE_OFF = Q_OFF
P_Q_OFF = P_GATE_OFF + 3 * D_MODEL
P_K_OFF = P_Q_OFF + ATTN_QK_W
P_V_OFF = P_K_OFF + ATTN_QK_W
PROJ_GATE_TILE = P_GATE_OFF // PROJ_TN
PROJ_Q_TILE = P_Q_OFF // PROJ_TN
PROJ_K_TILE = P_K_OFF // PROJ_TN
PROJ_TILES = N_IN // PROJ_TN
ATTN_TK = 512
ATTN_SCORES_AHEAD = 1
FOURIER_TR = 512
MERGE_TM = 256
FFN_TF = 512
FFN_UP_TM = 1024
FFN_DOWN_TM = 256


def _cparams(*sem):
    return pltpu.CompilerParams(dimension_semantics=sem, vmem_limit_bytes=VMEM_LIMIT)


def _mod_kernel(c_ref, w_ref, b_ref, o_ref):
    c = c_ref[...]
    s = c * jax.nn.sigmoid(c)
    s_hi = s.astype(BF16)
    s_lo = (s - s_hi.astype(F32)).astype(BF16)
    w = w_ref[...]
    w_hi = w.astype(BF16)
    w_lo = (w - w_hi.astype(F32)).astype(BF16)
    y = jnp.dot(s_hi, w_hi, preferred_element_type=F32)
    y += jnp.dot(s_lo, w_hi, preferred_element_type=F32)
    y += jnp.dot(s_hi, w_lo, preferred_element_type=F32)
    o_ref[...] = y + b_ref[...]


def _modulation(cvec, w_mod, b_mod):
    depth, d, n = w_mod.shape
    tn = 1024
    return pl.pallas_call(
        _mod_kernel,
        grid=(depth, n // tn),
        in_specs=[
            pl.BlockSpec((MOD_ROWS, d), lambda l, j: (0, 0)),
            pl.BlockSpec((None, d, tn), lambda l, j: (l, 0, j)),
            pl.BlockSpec((None, 1, tn), lambda l, j: (l, 0, j)),
        ],
        out_specs=pl.BlockSpec((None, MOD_ROWS, tn), lambda l, j: (l, 0, j)),
        out_shape=jax.ShapeDtypeStruct((depth, MOD_ROWS, n), F32),
        compiler_params=_cparams("arbitrary", "arbitrary"),
        name="modulation",
    )(cvec, w_mod, b_mod.reshape(depth, 1, n))


def _mod_spec(which, mod_row, tm):
    return pl.BlockSpec((None, None, 1, D_MODEL), lambda i, *_: (mod_row(i * tm), which, 0, 0))


def _rms(x, g):
    return x * lax.rsqrt(jnp.mean(x * x, axis=-1, keepdims=True) + EPS) * g


def _norm_mod_kernel(h_ref, g_ref, sh_ref, sc_ref, o_ref):
    y = _rms(h_ref[...], g_ref[...])
    o_ref[...] = (y * (1.0 + sc_ref[...]) + sh_ref[...]).astype(o_ref.dtype)


def _norm_modulate(h, g, mods, mod_row, tm):
    m, d = h.shape
    return pl.pallas_call(
        _norm_mod_kernel,
        grid=(m // tm,),
        in_specs=[
            pl.BlockSpec((tm, d), lambda i: (i, 0)),
            pl.BlockSpec((1, d), lambda i: (0, 0)),
            _mod_spec(0, mod_row, tm),
            _mod_spec(1, mod_row, tm),
        ],
        out_specs=pl.BlockSpec((tm, d), lambda i: (i, 0)),
        out_shape=jax.ShapeDtypeStruct((m, d), BF16),
        compiler_params=_cparams("arbitrary"),
        name="norm_modulate",
    )(h, g.reshape(1, d), mods, mods)


def _proj_kernel(*refs, col0, rope):
    if rope:
        x_ref, w_ref, cos_ref, sin_up_ref, sin_dn_ref, o_ref, wb_ref = refs
    else:
        x_ref, w_ref, o_ref, wb_ref = refs
    j = pl.program_id(0) + col0

    @pl.when(pl.program_id(1) == 0)
    def _():
        wb_ref[...] = w_ref[...].astype(BF16)

    is_q = j == PROJ_Q_TILE
    is_k = j == PROJ_K_TILE
    is_gate = (j >= PROJ_GATE_TILE) & (j < PROJ_Q_TILE)
    x = x_ref[...]

    def column_chunks(epilogue):
        for c in range(PROJ_TN // MXU_COLS):
            cols = slice(c * MXU_COLS, (c + 1) * MXU_COLS)
            acc = jnp.dot(x, wb_ref[:, cols], preferred_element_type=F32)
            o_ref[:, cols] = epilogue(acc).astype(o_ref.dtype)

    def rotary(acc):
        qscale = jnp.where(is_q, ATTN_SCALE * math.log2(math.e), 1.0)
        half = HEAD_DIM // 4
        out = []
        for hd in range(MXU_COLS // LANES):
            a = acc[:, hd * LANES:(hd + 1) * LANES]
            if rope:
                a = (a * cos_ref[...] + pltpu.roll(a, LANES - half, 1) * sin_up_ref[...]
                     + pltpu.roll(a, half, 1) * sin_dn_ref[...])
            out.append(a * qscale)
        return jnp.concatenate(out, axis=1)

    pl.when(is_q | is_k)(lambda: column_chunks(rotary))
    pl.when(is_gate)(lambda: column_chunks(jax.nn.sigmoid))
    pl.when(jnp.logical_not(is_q | is_k | is_gate))(lambda: column_chunks(lambda acc: acc))


def _in_proj(xm, w_in, layer, rope_tabs, tm, col0, ncols, seq_len):
    m, d = xm.shape
    tn = PROJ_TN
    n_plain = PROJ_GATE_TILE
    n_gate = PROJ_Q_TILE - PROJ_GATE_TILE
    n_qkv = PROJ_TILES - PROJ_Q_TILE

    def w_tile(j):
        j = j + col0
        return jnp.where(j < n_plain, j, jnp.where(j < n_plain + n_gate, j + n_qkv, j - n_gate))

    in_specs = [
        pl.BlockSpec((tm, d), lambda j, i: (i, 0)),
        pl.BlockSpec((None, d, tn), lambda j, i: (layer, 0, w_tile(j))),
    ]
    args = [xm, w_in]
    rope = rope_tabs is not None
    if rope:
        tps = seq_len // tm
        in_specs += [pl.BlockSpec((tm, LANES), lambda j, i: (i % tps, 0))] * 3
        args += list(rope_tabs)
    return pl.pallas_call(
        functools.partial(_proj_kernel, col0=col0, rope=rope),
        grid=(ncols, m // tm),
        in_specs=in_specs,
        out_specs=pl.BlockSpec((tm, tn), lambda j, i: (i, j)),
        out_shape=jax.ShapeDtypeStruct((m, ncols * tn), BF16),
        scratch_shapes=[pltpu.VMEM((d, tn), BF16)],
        compiler_params=_cparams("arbitrary", "arbitrary"),
        name="in_proj",
    )(*args)


def _fourier_kernel(f_ref, cs_ref, c0_ref, s0_ref, rc_ref, rs_ref, o_ref, uc_ref, us_ref, *, chunk, scale):
    i = pl.program_id(1)
    seq = f_ref.shape[0]
    gw = FOURIER_GROUP_W

    @pl.when(i == 0)
    def _():
        for r in range(seq // chunk):
            rows = slice(r * chunk, (r + 1) * chunk)
            for g in range(FOURIER_GROUPS):
                cols = slice(g * gw, (g + 1) * gw)
                y = jnp.dot(f_ref[rows, cols], cs_ref[...], preferred_element_type=F32)
                uc_ref[rows, cols] = y[:, :gw].astype(BF16)
                us_ref[rows, cols] = y[:, gw:].astype(BF16)

    c0, s0 = c0_ref[...].astype(F32), s0_ref[...].astype(F32)
    rc, rs = rc_ref[pl.ds(i, 1), :], rs_ref[pl.ds(i, 1), :]
    cos_t = (c0 * rc - s0 * rs).astype(BF16)
    sin_t = (s0 * rc + c0 * rs).astype(BF16)
    y = jnp.dot(cos_t, uc_ref[...], preferred_element_type=F32)
    y -= jnp.dot(sin_t, us_ref[...], preferred_element_type=F32)
    o_ref[...] = (y * scale).astype(o_ref.dtype)


def _fourier(p3, cs, dft, tr):
    b, seq, _ = p3.shape
    chunk = min(seq, 1024)
    scale = 1.0 / math.sqrt(seq * FOURIER_GROUP_W)
    c0, s0, rc, rs = dft
    full = lambda a: pl.BlockSpec(a.shape, lambda bi, i: (0, 0), pipeline_mode=pl.Buffered(1))
    return pl.pallas_call(
        functools.partial(_fourier_kernel, chunk=chunk, scale=scale),
        grid=(b, seq // tr),
        in_specs=[
            pl.BlockSpec((None, seq, FOURIER_W), lambda bi, i: (bi, 0, 0)),
            pl.BlockSpec((FOURIER_GROUP_W, 2 * FOURIER_GROUP_W), lambda bi, i: (0, 0)),
            full(c0), full(s0), full(rc), full(rs),
        ],
        out_specs=pl.BlockSpec((None, tr, FOURIER_W), lambda bi, i: (bi, i, 0)),
        out_shape=jax.ShapeDtypeStruct((b, seq, FOURIER_W), BF16),
        scratch_shapes=[pltpu.VMEM((seq, FOURIER_W), BF16), pltpu.VMEM((seq, FOURIER_W), BF16)],
        compiler_params=_cparams("arbitrary", "arbitrary"),
        name="fourier",
    )(p3, cs, c0, s0, rc, rs)


def _attn_kernel(*refs, tq, tk, n_lat, lam_init):
    if n_lat:
        lam_ref, q_ref, kc_ref, vc_ref, kl_ref, vl_ref, g_ref, o_ref = refs
    else:
        lam_ref, q_ref, kc_ref, vc_ref, g_ref, o_ref = refs
        kl_ref = vl_ref = None
    n_ctx = kc_ref.shape[0]
    q = q_ref[...]
    lane = lax.broadcasted_iota(jnp.int32, (1, LANES), 1)
    comp0 = lane < HEAD_DIM
    zero = jnp.zeros_like(q)
    qq = jnp.concatenate([jnp.where(comp0, q, zero), jnp.where(comp0, zero, q)], axis=0)
    chunks = [(kc_ref, vc_ref, 0, n_ctx)] + [(kl_ref, vl_ref, c * tk, tk) for c in range(n_lat)]

    def lane_fold(x, op):
        part = x[:, 0:LANES]
        for c in range(1, x.shape[1] // LANES):
            part = op(part, x[:, c * LANES:(c + 1) * LANES])
        return part

    def scores(c):
        kr, _, r0, sz = chunks[c]
        return lax.dot_general(qq, kr[r0:r0 + sz, :], (((1,), (1,)), ((), ())), preferred_element_type=F32)

    m = l = acc = None
    ahead = ATTN_SCORES_AHEAD
    pending = [scores(c) for c in range(min(ahead, len(chunks)))]
    for c, (_, vr, r0, sz) in enumerate(chunks):
        s = pending.pop(0)
        if c + ahead < len(chunks):
            pending.append(scores(c + ahead))
        mc = jnp.max(lane_fold(s, jnp.maximum), axis=-1, keepdims=True)
        m_new = mc if m is None else jnp.maximum(m, mc)
        p = jnp.exp2(s - m_new)
        lc = lane_fold(p, jnp.add)
        y = jnp.dot(p.astype(BF16), vr[r0:r0 + sz, :], preferred_element_type=F32)
        if m is None:
            l, acc = lc, y
        else:
            alpha = jnp.exp2(m - m_new)
            l = alpha * l + lc
            acc = alpha * acc + y
        m = m_new
    o = acc / jnp.sum(l, axis=-1, keepdims=True)
    lf = lam_ref[...]
    lam = (jnp.exp(jnp.sum(lf[0:1] * lf[1:2], axis=-1, keepdims=True))
           - jnp.exp(jnp.sum(lf[2:3] * lf[3:4], axis=-1, keepdims=True)) + lam_init)
    o = o[:tq] - lam * o[tq:]
    o = o * lax.rsqrt(jnp.mean(o * o, axis=-1, keepdims=True) + SUBLN_EPS)
    o_ref[...] = (o * g_ref[...] * (1.0 - lam_init)).astype(o_ref.dtype)


def _attention(q3, qblk, c3, kcblk, vcblk, l3, lam, subln_g, lam_init, tq):
    b, lq, _ = q3.shape
    n_ctx = c3.shape[1]
    tk = ATTN_TK
    in_specs = [
        pl.BlockSpec((4, HEAD_DIM), lambda bi, h, i: (0, 0)),
        pl.BlockSpec((None, tq, LANES), lambda bi, h, i: (bi, i, qblk + h)),
        pl.BlockSpec((None, n_ctx, LANES), lambda bi, h, i: (bi, 0, kcblk + h)),
        pl.BlockSpec((None, n_ctx, LANES), lambda bi, h, i: (bi, 0, vcblk + h)),
    ]
    args = [lam, q3, c3, c3]
    n_lat = 0
    if l3 is not None:
        seq = l3.shape[1]
        n_lat = seq // min(tk, seq)
        tk = seq // n_lat
        in_specs += [
            pl.BlockSpec((None, seq, LANES), lambda bi, h, i: (bi, 0, P_K_OFF // LANES + h)),
            pl.BlockSpec((None, seq, LANES), lambda bi, h, i: (bi, 0, P_V_OFF // LANES + h)),
        ]
        args += [l3, l3]
    in_specs.append(pl.BlockSpec((1, VAL_DIM), lambda bi, h, i: (0, 0)))
    args.append(subln_g.reshape(1, VAL_DIM))
    return pl.pallas_call(
        functools.partial(_attn_kernel, tq=tq, tk=tk, n_lat=n_lat, lam_init=lam_init),
        grid=(b, N_HEADS, lq // tq),
        in_specs=in_specs,
        out_specs=pl.BlockSpec((None, tq, LANES), lambda bi, h, i: (bi, i, h)),
        out_shape=jax.ShapeDtypeStruct((b, lq, ATTN_V_W), BF16),
        compiler_params=_cparams("arbitrary", "arbitrary", "arbitrary"),
        name="diff_attention",
    )(*args)


def _conv3_rows(buf_ref, w, tm, base=BF16_SUBLANES):
    return (buf_ref[base - 1:base - 1 + tm, :] * w[0:1] + buf_ref[base:base + tm, :] * w[1:2]
            + buf_ref[base + 1:base + 1 + tm, :] * w[2:3])


def _halo_specs(tm, width, colblk, m, row_axis=0):
    per = tm // BF16_SUBLANES
    last = m // BF16_SUBLANES - 1
    prev = pl.BlockSpec((BF16_SUBLANES, width),
                        lambda *ids: (jnp.maximum(ids[row_axis] * per - 1, 0), colblk))
    nxt = pl.BlockSpec((BF16_SUBLANES, width),
                       lambda *ids: (jnp.minimum((ids[row_axis] + 1) * per, last), colblk))
    return prev, nxt


def _merge_kernel(f_ref, cb_ref, cc_ref, cx_ref, ccp_ref, cxp_ref, ccn_ref, cxn_ref, a_ref,
                  gf_ref, gc_ref, ga_ref, cw_ref, wbf_ref, wbc_ref, wba_ref, wo_ref,
                  h_ref, gt_ref, g2_ref, sh_ref, sc_ref, ho_ref, xo_ref, z_ref, mg_ref, *, tps, nchunk):
    i = pl.program_id(0)
    tm = cc_ref.shape[0]
    h = BF16_SUBLANES
    first = (i % tps) == 0
    last = (i % tps) == tps - 1
    zp = ccp_ref[...].astype(F32) * cxp_ref[...].astype(F32)
    zn = ccn_ref[...].astype(F32) * cxn_ref[...].astype(F32)
    z_ref[0:h, :] = jnp.where(first, 0.0, zp)
    z_ref[h:h + tm, :] = cc_ref[...].astype(F32) * cx_ref[...].astype(F32)
    z_ref[h + tm:2 * h + tm, :] = jnp.where(last, 0.0, zn)
    cv = (cb_ref[...].astype(F32) * _conv3_rows(z_ref, cw_ref[...], tm)).astype(BF16)
    f = f_ref[...]
    a = a_ref[...]
    cn = D_MODEL // nchunk
    for n in range(nchunk):
        cols = slice(n * cn, (n + 1) * cn)
        y = gf_ref[:, cols].astype(F32) * jnp.dot(f, wbf_ref[:, cols], preferred_element_type=F32)
        y += gc_ref[:, cols].astype(F32) * jnp.dot(cv, wbc_ref[:, cols], preferred_element_type=F32)
        y += ga_ref[:, cols].astype(F32) * jnp.dot(a, wba_ref[:, cols], preferred_element_type=F32)
        mg_ref[:, cols] = y.astype(BF16)
    y = jnp.dot(mg_ref[...], wo_ref[...], preferred_element_type=F32)
    h_new = h_ref[...] + gt_ref[...] * y
    ho_ref[...] = h_new
    xo_ref[...] = (_rms(h_new, g2_ref[...]) * (1.0 + sc_ref[...]) + sh_ref[...]).astype(xo_ref.dtype)


def _merge_out(p, fmix, attn, conv_w, w_bf, w_bc, w_ba, w_out, layer, h, g2, mods, mod_row, tm, seq_len):
    m = p.shape[0]
    d = D_MODEL
    cw = CONV_W
    tps = seq_len // tm
    blk = lambda width, c: pl.BlockSpec((tm, width), lambda i: (i, c))
    ccp, ccn = _halo_specs(tm, cw, 2, m)
    cxp, cxn = _halo_specs(tm, cw, 3, m)
    gate0 = P_GATE_OFF // d
    full = lambda r, c: pl.BlockSpec((None, r, c), lambda i: (layer, 0, 0), pipeline_mode=pl.Buffered(1))
    return pl.pallas_call(
        functools.partial(_merge_kernel, tps=tps, nchunk=4),
        grid=(m // tm,),
        in_specs=[
            blk(FOURIER_W, 0),
            blk(cw, 1), blk(cw, 2), blk(cw, 3),
            ccp, cxp, ccn, cxn,
            blk(ATTN_V_W, 0),
            pl.BlockSpec((tm, d), lambda i: (i, gate0)),
            pl.BlockSpec((tm, d), lambda i: (i, gate0 + 1)),
            pl.BlockSpec((tm, d), lambda i: (i, gate0 + 2)),
            full(3, cw), full(FOURIER_W, d), full(cw, d), full(ATTN_V_W, d), full(d, d),
            pl.BlockSpec((tm, d), lambda i: (i, 0)),
            _mod_spec(2, mod_row, tm),
            pl.BlockSpec((1, d), lambda i: (0, 0)),
            _mod_spec(3, mod_row, tm),
            _mod_spec(4, mod_row, tm),
        ],
        out_specs=[pl.BlockSpec((tm, d), lambda i: (i, 0)), pl.BlockSpec((tm, d), lambda i: (i, 0))],
        out_shape=[jax.ShapeDtypeStruct((m, d), F32), jax.ShapeDtypeStruct((m, d), BF16)],
        scratch_shapes=[pltpu.VMEM((tm + 2 * BF16_SUBLANES, cw), F32), pltpu.VMEM((tm, d), BF16)],
        compiler_params=_cparams("arbitrary"),
        name="merge_out",
    )(fmix, p, p, p, p, p, p, p, attn, p, p, p, conv_w, w_bf, w_bc, w_ba, w_out,
      h, mods, g2.reshape(1, d), mods, mods)


def _ffn_up_kernel(x_ref, xp_ref, xn_ref, wa_ref, wv_ref, ca_ref, cv_ref, o_ref,
                   wab_ref, wvb_ref, xh_ref, ua_ref, uv_ref, *, tps, nseq):
    i = pl.program_id(1)
    tm = x_ref.shape[0]
    hl = BF16_SUBLANES
    seq = tm // nseq

    @pl.when(i == 0)
    def _():
        wab_ref[...] = wa_ref[...].astype(BF16)
        wvb_ref[...] = wv_ref[...].astype(BF16)

    base = [hl + s * (seq + hl) for s in range(nseq)]
    zeros = jnp.zeros((hl, x_ref.shape[1]), BF16)
    if nseq == 1:
        first = (i % tps) == 0
        last = (i % tps) == tps - 1
        xh_ref[0:hl, :] = jnp.where(first, zeros, xp_ref[...])
        xh_ref[hl + tm:2 * hl + tm, :] = jnp.where(last, zeros, xn_ref[...])
    else:
        for s in range(nseq + 1):
            xh_ref[s * (seq + hl):s * (seq + hl) + hl, :] = zeros
    for s in range(nseq):
        xh_ref[base[s]:base[s] + seq, :] = x_ref[s * seq:(s + 1) * seq, :]
    xh = xh_ref[...]
    ua_ref[...] = jnp.dot(xh, wab_ref[...], preferred_element_type=F32)
    uv_ref[...] = jnp.dot(xh, wvb_ref[...], preferred_element_type=F32)
    for s in range(nseq):
        a = _conv3_rows(ua_ref, ca_ref[...], seq, base[s])
        v = _conv3_rows(uv_ref, cv_ref[...], seq, base[s])
        o_ref[s * seq:(s + 1) * seq, :] = (a * jax.nn.sigmoid(a) * v).astype(o_ref.dtype)


def _ffn_down_kernel(a_ref, w_ref, h_ref, gt_ref, g_ref, sh_ref, sc_ref, ho_ref, xo_ref):
    y = jnp.dot(a_ref[...], w_ref[...], preferred_element_type=F32)
    h_new = h_ref[...] + gt_ref[...] * y
    ho_ref[...] = h_new
    xo_ref[...] = (_rms(h_new, g_ref[...]) * (1.0 + sc_ref[...]) + sh_ref[...]).astype(xo_ref.dtype)


def _ffn_down_final_kernel(a_ref, w_ref, h_ref, gt_ref, g_ref, o_ref):
    y = jnp.dot(a_ref[...], w_ref[...], preferred_element_type=F32)
    o_ref[...] = _rms(h_ref[...] + gt_ref[...] * y, g_ref[...])


def _ffn(xm, w_up, conv_w, w_down, layer, h, mods, mod_row, seq_len, nxt):
    m, d = xm.shape
    tf = FFN_TF
    nj = D_FF // tf
    tm = FFN_UP_TM if seq_len >= FFN_UP_TM else min(m, FFN_UP_TM)
    tps = max(seq_len // tm, 1)
    nseq = max(tm // seq_len, 1)
    stage_rows = nseq * (tm // nseq + BF16_SUBLANES) + BF16_SUBLANES
    xp, xn = _halo_specs(tm, d, 0, m, row_axis=1)
    act = pl.pallas_call(
        functools.partial(_ffn_up_kernel, tps=tps, nseq=nseq),
        grid=(nj, m // tm),
        in_specs=[
            pl.BlockSpec((tm, d), lambda j, i: (i, 0)),
            xp, xn,
            pl.BlockSpec((None, d, tf), lambda j, i: (layer, 0, j)),
            pl.BlockSpec((None, d, tf), lambda j, i: (layer, 0, nj + j)),
            pl.BlockSpec((None, 3, tf), lambda j, i: (layer, 0, j)),
            pl.BlockSpec((None, 3, tf), lambda j, i: (layer, 0, nj + j)),
        ],
        out_specs=pl.BlockSpec((tm, tf), lambda j, i: (i, j)),
        out_shape=jax.ShapeDtypeStruct((m, D_FF), BF16),
        scratch_shapes=[
            pltpu.VMEM((d, tf), BF16),
            pltpu.VMEM((d, tf), BF16),
            pltpu.VMEM((stage_rows, d), BF16),
            pltpu.VMEM((stage_rows, tf), F32),
            pltpu.VMEM((stage_rows, tf), F32),
        ],
        compiler_params=_cparams("arbitrary", "arbitrary"),
        name="ffn_up",
    )(xm, xm, xm, w_up, w_up, conv_w, conv_w)

    tmd = min(m, FFN_DOWN_TM)
    row = lambda i: (i, 0)
    in_specs = [
        pl.BlockSpec((tmd, D_FF), row),
        pl.BlockSpec((None, D_FF, d), lambda i: (layer, 0, 0), pipeline_mode=pl.Buffered(1)),
        pl.BlockSpec((tmd, d), row),
        _mod_spec(5, mod_row, tmd),
        pl.BlockSpec((1, d), lambda i: (0, 0)),
    ]
    g_next, mods_next = nxt
    if mods_next is None:
        return pl.pallas_call(
            _ffn_down_final_kernel,
            grid=(m // tmd,),
            in_specs=in_specs,
            out_specs=pl.BlockSpec((tmd, d), row),
            out_shape=jax.ShapeDtypeStruct((m, d), F32),
            compiler_params=_cparams("arbitrary"),
            name="ffn_down_final",
        )(act, w_down, h, mods, g_next.reshape(1, d)), None
    return pl.pallas_call(
        _ffn_down_kernel,
        grid=(m // tmd,),
        in_specs=in_specs + [_mod_spec(0, mod_row, tmd), _mod_spec(1, mod_row, tmd)],
        out_specs=[pl.BlockSpec((tmd, d), row), pl.BlockSpec((tmd, d), row)],
        out_shape=[jax.ShapeDtypeStruct((m, d), F32), jax.ShapeDtypeStruct((m, d), BF16)],
        compiler_params=_cparams("arbitrary"),
        name="ffn_down",
    )(act, w_down, h, mods, g_next.reshape(1, d), mods_next, mods_next)


def _rope_tables(length):
    n_freq = HEAD_DIM // 4
    pos = jnp.arange(length)
    row = (pos // GRID_W).astype(F32)
    col = (pos % GRID_W).astype(F32)
    inv = ROPE_THETA ** (-(2.0 * jnp.arange(n_freq, dtype=F32)) / (HEAD_DIM // 2))
    ang_r, ang_c = row[:, None] * inv, col[:, None] * inv
    ang = jnp.concatenate([ang_r, ang_r, ang_c, ang_c], axis=1)
    cos = jnp.tile(jnp.cos(ang), (1, 2))
    sin = jnp.tile(jnp.sin(ang), (1, 2))
    is_x1 = (jnp.arange(LANES) % (2 * n_freq)) < n_freq
    return cos, jnp.where(is_x1, -sin, 0.0), jnp.where(is_x1, 0.0, sin)


def _dft_tables(n, tr):
    m = jnp.arange(n, dtype=jnp.int32)

    def cos_sin(rows):
        ang = ((rows[:, None] * m[None, :]) % n).astype(F32) * (2.0 * math.pi / n)
        return jnp.cos(ang), jnp.sin(ang)

    rc, rs = cos_sin(jnp.arange(n // tr, dtype=jnp.int32) * tr)
    g = 1 << ((tr.bit_length() - 1 + 1) // 2)
    assert tr % g == 0
    ca, sa = cos_sin(jnp.arange(tr // g, dtype=jnp.int32) * g)
    cb, sb = cos_sin(jnp.arange(g, dtype=jnp.int32))
    c0 = (ca[:, None, :] * cb[None, :, :] - sa[:, None, :] * sb[None, :, :]).reshape(tr, n)
    s0 = (sa[:, None, :] * cb[None, :, :] + ca[:, None, :] * sb[None, :, :]).reshape(tr, n)
    return c0.astype(BF16), s0.astype(BF16), rc, rs


def _channel_dft():
    c = np.arange(FOURIER_GROUP_W)
    ang = 2.0 * np.pi * ((c[:, None] * c[None, :]) % FOURIER_GROUP_W) / FOURIER_GROUP_W
    return jnp.asarray(np.concatenate([np.cos(ang), np.sin(ang)], axis=1), dtype=BF16)


def _mixer(h, xm, seq_len, rope_tabs, dft, ctx_kv, w, mods, mod_row, lam_init, tm, tq, nxt,
           proj_cols=(0, PROJ_TILES)):
    m = h.shape[0]
    b = m // seq_len
    p = _in_proj(xm, w["w_in"], w["layer"], rope_tabs, min(m, PROJ_TM), proj_cols[0], proj_cols[1], seq_len)
    if proj_cols[1] != PROJ_TILES:
        return None, None, p
    p3 = p.reshape(b, seq_len, N_IN)
    fmix = _fourier(p3, w["cs"], dft, min(seq_len, FOURIER_TR)).reshape(m, FOURIER_W)
    if ctx_kv is None:
        attn = _attention(p3, P_Q_OFF // LANES, p3, P_K_OFF // LANES, P_V_OFF // LANES, None,
                          w["lambdas"], w["subln_g"], lam_init, tq)
    else:
        c3, kcblk, vcblk = ctx_kv
        attn = _attention(p3, P_Q_OFF // LANES, c3, kcblk, vcblk, p3,
                          w["lambdas"], w["subln_g"], lam_init, tq)
    h_mid, xm2 = _merge_out(p, fmix, attn.reshape(m, ATTN_V_W), w["conv_mix_w"], w["w_br_fourier"],
                            w["w_br_conv"], w["w_br_attn"], w["w_out"], w["layer"], h, w["g_norm2"],
                            mods, mod_row, min(tm, MERGE_TM), seq_len)
    h_new, x_next = _ffn(xm2, w["w_ffn_up"], w["ffn_conv_w"], w["w_ffn_down"], w["layer"], h_mid, mods,
                         mod_row, seq_len, nxt)
    return h_new, x_next, p


def kernel(x, c, ctx, c_ctx, w_mod, b_mod, g_norm1, g_norm2, w_in, conv_mix_w, lambdas, subln_g,
           w_br_fourier, w_br_conv, w_br_attn, w_out, w_ffn_up, ffn_conv_w, w_ffn_down, g_final):
    b, seq, d = x.shape
    n_ctx = ctx.shape[1]
    depth = w_mod.shape[0]
    assert d == D_MODEL and b + 1 <= MOD_ROWS

    wb = lambda a: a.astype(BF16)
    w_bf, w_bc, w_ba, w_o = wb(w_br_fourier), wb(w_br_conv), wb(w_br_attn), wb(w_out)
    w_dn = wb(w_ffn_down)

    rope_tabs = _rope_tables(seq)
    dft_l = _dft_tables(seq, min(seq, FOURIER_TR))
    dft_c = _dft_tables(n_ctx, min(n_ctx, FOURIER_TR))
    cs = _channel_dft()

    cvec = jnp.zeros((MOD_ROWS, d), F32).at[:b].set(c).at[b].set(c_ctx)
    mods_all = _modulation(cvec, w_mod, b_mod).reshape(depth, MOD_ROWS, 6, 1, d)

    tm_l = min(512, seq)
    tm_c = n_ctx
    tq = min(512, seq)
    lat_row = lambda r: r // seq
    ctx_row = lambda r: b

    h = x.reshape(b * seq, d)
    hc = ctx.reshape(b * n_ctx, d)
    xc = _norm_modulate(hc, g_norm1[0], mods_all[0], ctx_row, tm_c)
    xl = _norm_modulate(h, g_norm1[0], mods_all[0], lat_row, tm_l)
    for i in range(depth):
        last = i == depth - 1
        lam_init = 0.8 - 0.6 * math.exp(-0.3 * i)
        mods = mods_all[i]
        w = dict(w_in=w_in, layer=i, cs=cs, lambdas=lambdas[i], subln_g=subln_g[i], conv_mix_w=conv_mix_w,
                 w_br_fourier=w_bf, w_br_conv=w_bc, w_br_attn=w_ba, w_out=w_o,
                 g_norm2=g_norm2[i], w_ffn_up=w_ffn_up, ffn_conv_w=ffn_conv_w, w_ffn_down=w_dn)
        nxt = (g_final, None) if last else (g_norm1[i + 1], mods_all[i + 1])

        if last:
            _, _, pc = _mixer(hc, xc, n_ctx, None, dft_c, None, w, mods, ctx_row, lam_init, tm_c, n_ctx, None,
                              proj_cols=(PROJ_K_TILE, 2))
            ctx_kv = (pc.reshape(b, n_ctx, 2 * PROJ_TN), 0, PROJ_TN // LANES)
        else:
            hc_new, xc_new, pc = _mixer(hc, xc, n_ctx, None, dft_c, None, w, mods, ctx_row, lam_init,
                                        tm_c, n_ctx, nxt)
            ctx_kv = (pc.reshape(b, n_ctx, N_IN), P_K_OFF // LANES, P_V_OFF // LANES)
        h, xl, _ = _mixer(h, xl, seq, rope_tabs, dft_l, ctx_kv, w, mods, lat_row, lam_init, tm_l, tq, nxt)
        if not last:
            hc, xc = hc_new, xc_new
    return h.reshape(b, seq, d)
```

```python
import functools
import math

import numpy as np
import jax
import jax.numpy as jnp
from jax import lax
from jax.experimental import pallas as pl
from jax.experimental.pallas import tpu as pltpu

F32 = jnp.float32
BF16 = jnp.bfloat16

D_MODEL = 2048
GRID_W = 64
FOURIER_GROUPS = 4
FOURIER_GROUP_W = D_MODEL // 16
FOURIER_W = FOURIER_GROUPS * FOURIER_GROUP_W
CONV_W = D_MODEL // 4
N_HEADS = 8
HEAD_DIM = D_MODEL // (4 * N_HEADS)
VAL_DIM = 2 * HEAD_DIM
ATTN_QK_W = N_HEADS * 2 * HEAD_DIM
ATTN_V_W = N_HEADS * VAL_DIM
ROPE_THETA = 10000.0
ATTN_SCALE = HEAD_DIM ** -0.5
SUBLN_EPS = 1e-5
Q_OFF = FOURIER_W + 3 * CONV_W
K_OFF = Q_OFF + ATTN_QK_W
V_OFF = K_OFF + ATTN_QK_W
V_END = V_OFF + ATTN_V_W
N_IN = V_END + 3 * D_MODEL
D_FF = ((8 * D_MODEL // 3 + 255) // 256) * 256
EPS = 1e-6

LANES = 128
MXU_COLS = 256
BF16_SUBLANES = 16
VMEM_LIMIT = 56 * 1024 * 1024
MOD_ROWS = 8

PROJ_TN = 1024
PROJ_TM = 1024
P_GATE_OFF = Q_OFF
P_Q_OFF = P_GATE_OFF + 3 * D_MODEL
P_K_OFF = P_Q_OFF + ATTN_QK_W
P_V_OFF = P_K_OFF + ATTN_QK_W
PROJ_GATE_TILE = P_GATE_OFF // PROJ_TN
PROJ_Q_TILE = P_Q_OFF // PROJ_TN
PROJ_K_TILE = P_K_OFF // PROJ_TN
PROJ_TILES = N_IN // PROJ_TN
ATTN_TK = 512
ATTN_SCORES_AHEAD = 1
FOURIER_TR = 512
MERGE_TM = 256
FFN_TF = 512
FFN_UP_TM = 1024
FFN_DOWN_TM = 256


def _cparams(*sem):
    return pltpu.CompilerParams(dimension_semantics=sem, vmem_limit_bytes=VMEM_LIMIT)


def _mod_kernel(ct_ref, w_ref, b_ref, o_ref, *, n_rows):
    ct = ct_ref[...]
    st = ct * jax.nn.sigmoid(ct)
    w = w_ref[...]
    d, tn = w.shape
    rows = []
    for r in range(n_rows):
        t = (st[:, r:r + 1] * w).reshape(d // MOD_ROWS, MOD_ROWS, tn)
        rows.append(jnp.sum(jnp.sum(t, axis=0), axis=0, keepdims=True))
    rows.append(jnp.zeros((MOD_ROWS - n_rows, tn), F32))
    o_ref[...] = jnp.concatenate(rows, axis=0) + b_ref[...]


def _modulation(cvec_t, n_rows, w_mod, b_mod):
    depth, d, n = w_mod.shape
    tn = 1024
    return pl.pallas_call(
        functools.partial(_mod_kernel, n_rows=n_rows),
        grid=(depth, n // tn),
        in_specs=[
            pl.BlockSpec((d, MOD_ROWS), lambda l, j: (0, 0)),
            pl.BlockSpec((None, d, tn), lambda l, j: (l, 0, j)),
            pl.BlockSpec((None, 1, tn), lambda l, j: (l, 0, j)),
        ],
        out_specs=pl.BlockSpec((None, MOD_ROWS, tn), lambda l, j: (l, 0, j)),
        out_shape=jax.ShapeDtypeStruct((depth, MOD_ROWS, n), F32),
        compiler_params=_cparams("arbitrary", "arbitrary"),
        name="modulation",
    )(cvec_t, w_mod, b_mod.reshape(depth, 1, n))


def _mod_spec(which, mod_row, tm):
    return pl.BlockSpec((None, None, 1, D_MODEL), lambda i, *_: (mod_row(i * tm), which, 0, 0))


def _rms(x, g):
    return x * lax.rsqrt(jnp.mean(x * x, axis=-1, keepdims=True) + EPS) * g


def _norm_mod_kernel(h_ref, g_ref, sh_ref, sc_ref, o_ref):
    y = _rms(h_ref[...], g_ref[...])
    o_ref[...] = (y * (1.0 + sc_ref[...]) + sh_ref[...]).astype(o_ref.dtype)


def _norm_modulate(h, g, mods, mod_row, tm):
    m, d = h.shape
    return pl.pallas_call(
        _norm_mod_kernel,
        grid=(m // tm,),
        in_specs=[
            pl.BlockSpec((tm, d), lambda i: (i, 0)),
            pl.BlockSpec((1, d), lambda i: (0, 0)),
            _mod_spec(0, mod_row, tm),
            _mod_spec(1, mod_row, tm),
        ],
        out_specs=pl.BlockSpec((tm, d), lambda i: (i, 0)),
        out_shape=jax.ShapeDtypeStruct((m, d), BF16),
        compiler_params=_cparams("arbitrary"),
        name="norm_modulate",
    )(h, g.reshape(1, d), mods, mods)


def _proj_kernel(*refs, col0, rope):
    if rope:
        x_ref, w_ref, cos_ref, sin_up_ref, sin_dn_ref, o_ref, wb_ref = refs
    else:
        x_ref, w_ref, o_ref, wb_ref = refs
    j = pl.program_id(0) + col0

    @pl.when(pl.program_id(1) == 0)
    def _():
        wb_ref[...] = w_ref[...].astype(BF16)

    is_q = j == PROJ_Q_TILE
    is_k = j == PROJ_K_TILE
    x = x_ref[...]

    def column_chunks(epilogue):
        for c in range(PROJ_TN // MXU_COLS):
            cols = slice(c * MXU_COLS, (c + 1) * MXU_COLS)
            acc = jnp.dot(x, wb_ref[:, cols], preferred_element_type=F32)
            o_ref[:, cols] = epilogue(acc).astype(o_ref.dtype)

    def rotary(acc):
        qscale = jnp.where(is_q, ATTN_SCALE * math.log2(math.e), 1.0)
        half = HEAD_DIM // 4
        out = []
        for hd in range(MXU_COLS // LANES):
            a = acc[:, hd * LANES:(hd + 1) * LANES]
            if rope:
                a = (a * cos_ref[...] + pltpu.roll(a, LANES - half, 1) * sin_up_ref[...]
                     + pltpu.roll(a, half, 1) * sin_dn_ref[...])
            out.append(a * qscale)
        return jnp.concatenate(out, axis=1)

    pl.when(is_q | is_k)(lambda: column_chunks(rotary))
    pl.when(jnp.logical_not(is_q | is_k))(lambda: column_chunks(lambda acc: acc))


def _in_proj(xm, w_in, layer, rope_tabs, tm, col0, ncols, seq_len):
    m, d = xm.shape
    tn = PROJ_TN
    n_plain = PROJ_GATE_TILE
    n_gate = PROJ_Q_TILE - PROJ_GATE_TILE
    n_qkv = PROJ_TILES - PROJ_Q_TILE

    def w_tile(j):
        j = j + col0
        return jnp.where(j < n_plain, j, jnp.where(j < n_plain + n_gate, j + n_qkv, j - n_gate))

    in_specs = [
        pl.BlockSpec((tm, d), lambda j, i: (i, 0)),
        pl.BlockSpec((None, d, tn), lambda j, i: (layer, 0, w_tile(j))),
    ]
    args = [xm, w_in]
    rope = rope_tabs is not None
    if rope:
        tps = seq_len // tm
        in_specs += [pl.BlockSpec((tm, LANES), lambda j, i: (i % tps, 0))] * 3
        args += list(rope_tabs)
    return pl.pallas_call(
        functools.partial(_proj_kernel, col0=col0, rope=rope),
        grid=(ncols, m // tm),
        in_specs=in_specs,
        out_specs=pl.BlockSpec((tm, tn), lambda j, i: (i, j)),
        out_shape=jax.ShapeDtypeStruct((m, ncols * tn), BF16),
        scratch_shapes=[pltpu.VMEM((d, tn), BF16)],
        compiler_params=_cparams("arbitrary", "arbitrary"),
        name="in_proj",
    )(*args)


def _fourier_kernel(f_ref, cs_ref, c0_ref, s0_ref, rc_ref, rs_ref, o_ref, uc_ref, us_ref, *, chunk, scale):
    i = pl.program_id(1)
    seq = f_ref.shape[0]
    gw = FOURIER_GROUP_W

    @pl.when(i == 0)
    def _():
        for r in range(seq // chunk):
            rows = slice(r * chunk, (r + 1) * chunk)
            for g in range(FOURIER_GROUPS):
                cols = slice(g * gw, (g + 1) * gw)
                y = jnp.dot(f_ref[rows, cols], cs_ref[...], preferred_element_type=F32)
                uc_ref[rows, cols] = y[:, :gw].astype(BF16)
                us_ref[rows, cols] = y[:, gw:].astype(BF16)

    c0, s0 = c0_ref[...].astype(F32), s0_ref[...].astype(F32)
    rc, rs = rc_ref[pl.ds(i, 1), :], rs_ref[pl.ds(i, 1), :]
    cos_t = (c0 * rc - s0 * rs).astype(BF16)
    sin_t = (s0 * rc + c0 * rs).astype(BF16)
    y = jnp.dot(cos_t, uc_ref[...], preferred_element_type=F32)
    y -= jnp.dot(sin_t, us_ref[...], preferred_element_type=F32)
    o_ref[...] = (y * scale).astype(o_ref.dtype)


def _fourier(p3, cs, dft, tr):
    b, seq, _ = p3.shape
    chunk = min(seq, 1024)
    scale = 1.0 / math.sqrt(seq * FOURIER_GROUP_W)
    c0, s0, rc, rs = dft
    full = lambda a: pl.BlockSpec(a.shape, lambda bi, i: (0, 0), pipeline_mode=pl.Buffered(1))
    return pl.pallas_call(
        functools.partial(_fourier_kernel, chunk=chunk, scale=scale),
        grid=(b, seq // tr),
        in_specs=[
            pl.BlockSpec((None, seq, FOURIER_W), lambda bi, i: (bi, 0, 0)),
            pl.BlockSpec((FOURIER_GROUP_W, 2 * FOURIER_GROUP_W), lambda bi, i: (0, 0)),
            full(c0), full(s0), full(rc), full(rs),
        ],
        out_specs=pl.BlockSpec((None, tr, FOURIER_W), lambda bi, i: (bi, i, 0)),
        out_shape=jax.ShapeDtypeStruct((b, seq, FOURIER_W), BF16),
        scratch_shapes=[pltpu.VMEM((seq, FOURIER_W), BF16), pltpu.VMEM((seq, FOURIER_W), BF16)],
        compiler_params=_cparams("arbitrary", "arbitrary"),
        name="fourier",
    )(p3, cs, c0, s0, rc, rs)


def _attn_kernel(*refs, tq, tk, n_lat, lam_init):
    if n_lat:
        lam_ref, q_ref, kc_ref, vc_ref, kl_ref, vl_ref, g_ref, o_ref = refs
    else:
        lam_ref, q_ref, kc_ref, vc_ref, g_ref, o_ref = refs
        kl_ref = vl_ref = None
    n_ctx = kc_ref.shape[0]
    q = q_ref[...]
    lane = lax.broadcasted_iota(jnp.int32, (1, LANES), 1)
    comp0 = lane < HEAD_DIM
    zero = jnp.zeros_like(q)
    qq = jnp.concatenate([jnp.where(comp0, q, zero), jnp.where(comp0, zero, q)], axis=0)
    chunks = [(kc_ref, vc_ref, 0, n_ctx)] + [(kl_ref, vl_ref, c * tk, tk) for c in range(n_lat)]

    def lane_fold(x, op):
        part = x[:, 0:LANES]
        for c in range(1, x.shape[1] // LANES):
            part = op(part, x[:, c * LANES:(c + 1) * LANES])
        return part

    def scores(c):
        kr, _, r0, sz = chunks[c]
        return lax.dot_general(qq, kr[r0:r0 + sz, :], (((1,), (1,)), ((), ())), preferred_element_type=F32)

    m = l = acc = None
    ahead = ATTN_SCORES_AHEAD
    pending = [scores(c) for c in range(min(ahead, len(chunks)))]
    for c, (_, vr, r0, sz) in enumerate(chunks):
        s = pending.pop(0)
        if c + ahead < len(chunks):
            pending.append(scores(c + ahead))
        mc = jnp.max(lane_fold(s, jnp.maximum), axis=-1, keepdims=True)
        m_new = mc if m is None else jnp.maximum(m, mc)
        p = jnp.exp2(s - m_new)
        lc = lane_fold(p, jnp.add)
        y = jnp.dot(p.astype(BF16), vr[r0:r0 + sz, :], preferred_element_type=F32)
        if m is None:
            l, acc = lc, y
        else:
            alpha = jnp.exp2(m - m_new)
            l = alpha * l + lc
            acc = alpha * acc + y
        m = m_new
    o = acc / jnp.sum(l, axis=-1, keepdims=True)
    lf = lam_ref[...]
    lam = (jnp.exp(jnp.sum(lf[0:1] * lf[1:2], axis=-1, keepdims=True))
           - jnp.exp(jnp.sum(lf[2:3] * lf[3:4], axis=-1, keepdims=True)) + lam_init)
    o = o[:tq] - lam * o[tq:]
    o = o * lax.rsqrt(jnp.mean(o * o, axis=-1, keepdims=True) + SUBLN_EPS)
    o_ref[...] = (o * g_ref[...] * (1.0 - lam_init)).astype(o_ref.dtype)


def _attention(q3, qblk, c3, kcblk, vcblk, l3, lam, subln_g, lam_init, tq):
    b, lq, _ = q3.shape
    n_ctx = c3.shape[1]
    tk = ATTN_TK
    in_specs = [
        pl.BlockSpec((4, HEAD_DIM), lambda bi, h, i: (0, 0)),
        pl.BlockSpec((None, tq, LANES), lambda bi, h, i: (bi, i, qblk + h)),
        pl.BlockSpec((None, n_ctx, LANES), lambda bi, h, i: (bi, 0, kcblk + h)),
        pl.BlockSpec((None, n_ctx, LANES), lambda bi, h, i: (bi, 0, vcblk + h)),
    ]
    args = [lam, q3, c3, c3]
    n_lat = 0
    if l3 is not None:
        seq = l3.shape[1]
        n_lat = seq // min(tk, seq)
        tk = seq // n_lat
        in_specs += [
            pl.BlockSpec((None, seq, LANES), lambda bi, h, i: (bi, 0, P_K_OFF // LANES + h)),
            pl.BlockSpec((None, seq, LANES), lambda bi, h, i: (bi, 0, P_V_OFF // LANES + h)),
        ]
        args += [l3, l3]
    in_specs.append(pl.BlockSpec((1, VAL_DIM), lambda bi, h, i: (0, 0)))
    args.append(subln_g.reshape(1, VAL_DIM))
    return pl.pallas_call(
        functools.partial(_attn_kernel, tq=tq, tk=tk, n_lat=n_lat, lam_init=lam_init),
        grid=(b, N_HEADS, lq // tq),
        in_specs=in_specs,
        out_specs=pl.BlockSpec((None, tq, LANES), lambda bi, h, i: (bi, i, h)),
        out_shape=jax.ShapeDtypeStruct((b, lq, ATTN_V_W), BF16),
        compiler_params=_cparams("arbitrary", "arbitrary", "arbitrary"),
        name="diff_attention",
    )(*args)


def _conv3_rows(buf_ref, w, tm, base=BF16_SUBLANES):
    return (buf_ref[base - 1:base - 1 + tm, :] * w[0:1] + buf_ref[base:base + tm, :] * w[1:2]
            + buf_ref[base + 1:base + 1 + tm, :] * w[2:3])


def _halo_specs(tm, width, colblk, m, row_axis=0):
    per = tm // BF16_SUBLANES
    last = m // BF16_SUBLANES - 1
    prev = pl.BlockSpec((BF16_SUBLANES, width),
                        lambda *ids: (jnp.maximum(ids[row_axis] * per - 1, 0), colblk))
    nxt = pl.BlockSpec((BF16_SUBLANES, width),
                       lambda *ids: (jnp.minimum((ids[row_axis] + 1) * per, last), colblk))
    return prev, nxt


def _merge_kernel(f_ref, cb_ref, cc_ref, cx_ref, ccp_ref, cxp_ref, ccn_ref, cxn_ref, a_ref,
                  gf_ref, gc_ref, ga_ref, cw_ref, wbf_ref, wbc_ref, wba_ref, wo_ref,
                  h_ref, gt_ref, g2_ref, sh_ref, sc_ref, ho_ref, xo_ref, z_ref, mg_ref, *, tps, nchunk):
    i = pl.program_id(0)
    tm = cc_ref.shape[0]
    h = BF16_SUBLANES
    first = (i % tps) == 0
    last = (i % tps) == tps - 1
    zp = ccp_ref[...].astype(F32) * cxp_ref[...].astype(F32)
    zn = ccn_ref[...].astype(F32) * cxn_ref[...].astype(F32)
    z_ref[0:h, :] = jnp.where(first, 0.0, zp)
    z_ref[h:h + tm, :] = cc_ref[...].astype(F32) * cx_ref[...].astype(F32)
    z_ref[h + tm:2 * h + tm, :] = jnp.where(last, 0.0, zn)
    cv = (cb_ref[...].astype(F32) * _conv3_rows(z_ref, cw_ref[...], tm)).astype(BF16)
    f = f_ref[...]
    a = a_ref[...]
    cn = D_MODEL // nchunk
    for n in range(nchunk):
        cols = slice(n * cn, (n + 1) * cn)
        gate = lambda g_ref: jax.nn.sigmoid(g_ref[:, cols].astype(F32))
        y = gate(gf_ref) * jnp.dot(f, wbf_ref[:, cols], preferred_element_type=F32)
        y += gate(gc_ref) * jnp.dot(cv, wbc_ref[:, cols], preferred_element_type=F32)
        y += gate(ga_ref) * jnp.dot(a, wba_ref[:, cols], preferred_element_type=F32)
        mg_ref[:, cols] = y.astype(BF16)
    y = jnp.dot(mg_ref[...], wo_ref[...], preferred_element_type=F32)
    h_new = h_ref[...] + gt_ref[...] * y
    ho_ref[...] = h_new
    xo_ref[...] = (_rms(h_new, g2_ref[...]) * (1.0 + sc_ref[...]) + sh_ref[...]).astype(xo_ref.dtype)


def _merge_out(p, fmix, attn, conv_w, w_bf, w_bc, w_ba, w_out, layer, h, g2, mods, mod_row, tm, seq_len):
    m = p.shape[0]
    d = D_MODEL
    cw = CONV_W
    tps = seq_len // tm
    blk = lambda width, c: pl.BlockSpec((tm, width), lambda i: (i, c))
    ccp, ccn = _halo_specs(tm, cw, 2, m)
    cxp, cxn = _halo_specs(tm, cw, 3, m)
    gate0 = P_GATE_OFF // d
    full = lambda r, c: pl.BlockSpec((None, r, c), lambda i: (layer, 0, 0), pipeline_mode=pl.Buffered(1))
    return pl.pallas_call(
        functools.partial(_merge_kernel, tps=tps, nchunk=4),
        grid=(m // tm,),
        in_specs=[
            blk(FOURIER_W, 0),
            blk(cw, 1), blk(cw, 2), blk(cw, 3),
            ccp, cxp, ccn, cxn,
            blk(ATTN_V_W, 0),
            pl.BlockSpec((tm, d), lambda i: (i, gate0)),
            pl.BlockSpec((tm, d), lambda i: (i, gate0 + 1)),
            pl.BlockSpec((tm, d), lambda i: (i, gate0 + 2)),
            full(3, cw), full(FOURIER_W, d), full(cw, d), full(ATTN_V_W, d), full(d, d),
            pl.BlockSpec((tm, d), lambda i: (i, 0)),
            _mod_spec(2, mod_row, tm),
            pl.BlockSpec((1, d), lambda i: (0, 0)),
            _mod_spec(3, mod_row, tm),
            _mod_spec(4, mod_row, tm),
        ],
        out_specs=[pl.BlockSpec((tm, d), lambda i: (i, 0)), pl.BlockSpec((tm, d), lambda i: (i, 0))],
        out_shape=[jax.ShapeDtypeStruct((m, d), F32), jax.ShapeDtypeStruct((m, d), BF16)],
        scratch_shapes=[pltpu.VMEM((tm + 2 * BF16_SUBLANES, cw), F32), pltpu.VMEM((tm, d), BF16)],
        compiler_params=_cparams("arbitrary"),
        name="merge_out",
    )(fmix, p, p, p, p, p, p, p, attn, p, p, p, conv_w, w_bf, w_bc, w_ba, w_out,
      h, mods, g2.reshape(1, d), mods, mods)


def _ffn_up_kernel(x_ref, xp_ref, xn_ref, wa_ref, wv_ref, ca_ref, cv_ref, o_ref,
                   wab_ref, wvb_ref, xh_ref, ua_ref, uv_ref, *, tps, nseq):
    i = pl.program_id(1)
    tm = x_ref.shape[0]
    hl = BF16_SUBLANES
    seq = tm // nseq

    @pl.when(i == 0)
    def _():
        wab_ref[...] = wa_ref[...].astype(BF16)
        wvb_ref[...] = wv_ref[...].astype(BF16)

    base = [hl + s * (seq + hl) for s in range(nseq)]
    zeros = jnp.zeros((hl, x_ref.shape[1]), BF16)
    if nseq == 1:
        first = (i % tps) == 0
        last = (i % tps) == tps - 1
        xh_ref[0:hl, :] = jnp.where(first, zeros, xp_ref[...])
        xh_ref[hl + tm:2 * hl + tm, :] = jnp.where(last, zeros, xn_ref[...])
    else:
        for s in range(nseq + 1):
            xh_ref[s * (seq + hl):s * (seq + hl) + hl, :] = zeros
    for s in range(nseq):
        xh_ref[base[s]:base[s] + seq, :] = x_ref[s * seq:(s + 1) * seq, :]
    xh = xh_ref[...]
    ua_ref[...] = jnp.dot(xh, wab_ref[...], preferred_element_type=F32)
    uv_ref[...] = jnp.dot(xh, wvb_ref[...], preferred_element_type=F32)
    for s in range(nseq):
        a = _conv3_rows(ua_ref, ca_ref[...], seq, base[s])
        v = _conv3_rows(uv_ref, cv_ref[...], seq, base[s])
        o_ref[s * seq:(s + 1) * seq, :] = (a * jax.nn.sigmoid(a) * v).astype(o_ref.dtype)


def _ffn_down_kernel(a_ref, w_ref, h_ref, gt_ref, g_ref, sh_ref, sc_ref, ho_ref, xo_ref):
    y = jnp.dot(a_ref[...], w_ref[...], preferred_element_type=F32)
    h_new = h_ref[...] + gt_ref[...] * y
    ho_ref[...] = h_new
    xo_ref[...] = (_rms(h_new, g_ref[...]) * (1.0 + sc_ref[...]) + sh_ref[...]).astype(xo_ref.dtype)


def _ffn_down_final_kernel(a_ref, w_ref, h_ref, gt_ref, g_ref, o_ref):
    y = jnp.dot(a_ref[...], w_ref[...], preferred_element_type=F32)
    o_ref[...] = _rms(h_ref[...] + gt_ref[...] * y, g_ref[...])


def _ffn(xm, w_up, conv_w, w_down, layer, h, mods, mod_row, seq_len, nxt):
    m, d = xm.shape
    tf = FFN_TF
    nj = D_FF // tf
    tm = FFN_UP_TM if seq_len >= FFN_UP_TM else min(m, FFN_UP_TM)
    tps = max(seq_len // tm, 1)
    nseq = max(tm // seq_len, 1)
    stage_rows = nseq * (tm // nseq + BF16_SUBLANES) + BF16_SUBLANES
    xp, xn = _halo_specs(tm, d, 0, m, row_axis=1)
    act = pl.pallas_call(
        functools.partial(_ffn_up_kernel, tps=tps, nseq=nseq),
        grid=(nj, m // tm),
        in_specs=[
            pl.BlockSpec((tm, d), lambda j, i: (i, 0)),
            xp, xn,
            pl.BlockSpec((None, d, tf), lambda j, i: (layer, 0, j)),
            pl.BlockSpec((None, d, tf), lambda j, i: (layer, 0, nj + j)),
            pl.BlockSpec((None, 3, tf), lambda j, i: (layer, 0, j)),
            pl.BlockSpec((None, 3, tf), lambda j, i: (layer, 0, nj + j)),
        ],
        out_specs=pl.BlockSpec((tm, tf), lambda j, i: (i, j)),
        out_shape=jax.ShapeDtypeStruct((m, D_FF), BF16),
        scratch_shapes=[
            pltpu.VMEM((d, tf), BF16),
            pltpu.VMEM((d, tf), BF16),
            pltpu.VMEM((stage_rows, d), BF16),
            pltpu.VMEM((stage_rows, tf), F32),
            pltpu.VMEM((stage_rows, tf), F32),
        ],
        compiler_params=_cparams("arbitrary", "arbitrary"),
        name="ffn_up",
    )(xm, xm, xm, w_up, w_up, conv_w, conv_w)

    tmd = min(m, FFN_DOWN_TM)
    row = lambda i: (i, 0)
    in_specs = [
        pl.BlockSpec((tmd, D_FF), row),
        pl.BlockSpec((None, D_FF, d), lambda i: (layer, 0, 0), pipeline_mode=pl.Buffered(1)),
        pl.BlockSpec((tmd, d), row),
        _mod_spec(5, mod_row, tmd),
        pl.BlockSpec((1, d), lambda i: (0, 0)),
    ]
    g_next, mods_next = nxt
    if mods_next is None:
        return pl.pallas_call(
            _ffn_down_final_kernel,
            grid=(m // tmd,),
            in_specs=in_specs,
            out_specs=pl.BlockSpec((tmd, d), row),
            out_shape=jax.ShapeDtypeStruct((m, d), F32),
            compiler_params=_cparams("arbitrary"),
            name="ffn_down_final",
        )(act, w_down, h, mods, g_next.reshape(1, d)), None
    return pl.pallas_call(
        _ffn_down_kernel,
        grid=(m // tmd,),
        in_specs=in_specs + [_mod_spec(0, mod_row, tmd), _mod_spec(1, mod_row, tmd)],
        out_specs=[pl.BlockSpec((tmd, d), row), pl.BlockSpec((tmd, d), row)],
        out_shape=[jax.ShapeDtypeStruct((m, d), F32), jax.ShapeDtypeStruct((m, d), BF16)],
        compiler_params=_cparams("arbitrary"),
        name="ffn_down",
    )(act, w_down, h, mods, g_next.reshape(1, d), mods_next, mods_next)


def _rope_tables(length):
    n_freq = HEAD_DIM // 4
    pos = jnp.arange(length)
    row = (pos // GRID_W).astype(F32)
    col = (pos % GRID_W).astype(F32)
    inv = ROPE_THETA ** (-(2.0 * jnp.arange(n_freq, dtype=F32)) / (HEAD_DIM // 2))
    ang_r, ang_c = row[:, None] * inv, col[:, None] * inv
    ang = jnp.concatenate([ang_r, ang_r, ang_c, ang_c], axis=1)
    cos = jnp.tile(jnp.cos(ang), (1, 2))
    sin = jnp.tile(jnp.sin(ang), (1, 2))
    is_x1 = (jnp.arange(LANES) % (2 * n_freq)) < n_freq
    return cos, jnp.where(is_x1, -sin, 0.0), jnp.where(is_x1, 0.0, sin)


def _dft_tables(n, tr):
    m = jnp.arange(n, dtype=jnp.int32)

    def cos_sin(rows):
        ang = ((rows[:, None] * m[None, :]) % n).astype(F32) * (2.0 * math.pi / n)
        return jnp.cos(ang), jnp.sin(ang)

    rc, rs = cos_sin(jnp.arange(n // tr, dtype=jnp.int32) * tr)
    g = 1 << ((tr.bit_length() - 1 + 1) // 2)
    assert tr % g == 0
    ca, sa = cos_sin(jnp.arange(tr // g, dtype=jnp.int32) * g)
    cb, sb = cos_sin(jnp.arange(g, dtype=jnp.int32))
    c0 = (ca[:, None, :] * cb[None, :, :] - sa[:, None, :] * sb[None, :, :]).reshape(tr, n)
    s0 = (sa[:, None, :] * cb[None, :, :] + ca[:, None, :] * sb[None, :, :]).reshape(tr, n)
    return c0.astype(BF16), s0.astype(BF16), rc, rs


def _channel_dft():
    c = np.arange(FOURIER_GROUP_W)
    ang = 2.0 * np.pi * ((c[:, None] * c[None, :]) % FOURIER_GROUP_W) / FOURIER_GROUP_W
    return jnp.asarray(np.concatenate([np.cos(ang), np.sin(ang)], axis=1), dtype=BF16)


def _mixer(h, xm, seq_len, rope_tabs, dft, ctx_kv, w, mods, mod_row, lam_init, tm, tq, nxt,
           proj_cols=(0, PROJ_TILES)):
    m = h.shape[0]
    b = m // seq_len
    p = _in_proj(xm, w["w_in"], w["layer"], rope_tabs, min(m, PROJ_TM), proj_cols[0], proj_cols[1], seq_len)
    if proj_cols[1] != PROJ_TILES:
        return None, None, p
    p3 = p.reshape(b, seq_len, N_IN)
    fmix = _fourier(p3, w["cs"], dft, min(seq_len, FOURIER_TR)).reshape(m, FOURIER_W)
    if ctx_kv is None:
        attn = _attention(p3, P_Q_OFF // LANES, p3, P_K_OFF // LANES, P_V_OFF // LANES, None,
                          w["lambdas"], w["subln_g"], lam_init, tq)
    else:
        c3, kcblk, vcblk = ctx_kv
        attn = _attention(p3, P_Q_OFF // LANES, c3, kcblk, vcblk, p3,
                          w["lambdas"], w["subln_g"], lam_init, tq)
    h_mid, xm2 = _merge_out(p, fmix, attn.reshape(m, ATTN_V_W), w["conv_mix_w"], w["w_br_fourier"],
                            w["w_br_conv"], w["w_br_attn"], w["w_out"], w["layer"], h, w["g_norm2"],
                            mods, mod_row, min(tm, MERGE_TM), seq_len)
    h_new, x_next = _ffn(xm2, w["w_ffn_up"], w["ffn_conv_w"], w["w_ffn_down"], w["layer"], h_mid, mods,
                         mod_row, seq_len, nxt)
    return h_new, x_next, p


def kernel(x, c, ctx, c_ctx, w_mod, b_mod, g_norm1, g_norm2, w_in, conv_mix_w, lambdas, subln_g,
           w_br_fourier, w_br_conv, w_br_attn, w_out, w_ffn_up, ffn_conv_w, w_ffn_down, g_final):
    b, seq, d = x.shape
    n_ctx = ctx.shape[1]
    depth = w_mod.shape[0]
    assert d == D_MODEL and b + 1 <= MOD_ROWS

    wb = lambda a: a.astype(BF16)
    w_bf, w_bc, w_ba, w_o = wb(w_br_fourier), wb(w_br_conv), wb(w_br_attn), wb(w_out)
    w_dn = wb(w_ffn_down)

    rope_tabs = _rope_tables(seq)
    dft_l = _dft_tables(seq, min(seq, FOURIER_TR))
    dft_c = _dft_tables(n_ctx, min(n_ctx, FOURIER_TR))
    cs = _channel_dft()

    cvec = jnp.zeros((MOD_ROWS, d), F32).at[:b].set(c).at[b].set(c_ctx)
    mods_all = _modulation(cvec.T, b + 1, w_mod, b_mod).reshape(depth, MOD_ROWS, 6, 1, d)

    tm_l = min(512, seq)
    tm_c = n_ctx
    tq = min(512, seq)
    lat_row = lambda r: r // seq
    ctx_row = lambda r: b

    h = x.reshape(b * seq, d)
    hc = ctx.reshape(b * n_ctx, d)
    xc = _norm_modulate(hc, g_norm1[0], mods_all[0], ctx_row, tm_c)
    xl = _norm_modulate(h, g_norm1[0], mods_all[0], lat_row, tm_l)
    for i in range(depth):
        last = i == depth - 1
        lam_init = 0.8 - 0.6 * math.exp(-0.3 * i)
        mods = mods_all[i]
        w = dict(w_in=w_in, layer=i, cs=cs, lambdas=lambdas[i], subln_g=subln_g[i], conv_mix_w=conv_mix_w,
                 w_br_fourier=w_bf, w_br_conv=w_bc, w_br_attn=w_ba, w_out=w_o,
                 g_norm2=g_norm2[i], w_ffn_up=w_ffn_up, ffn_conv_w=ffn_conv_w, w_ffn_down=w_dn)
        nxt = (g_final, None) if last else (g_norm1[i + 1], mods_all[i + 1])

        if last:
            _, _, pc = _mixer(hc, xc, n_ctx, None, dft_c, None, w, mods, ctx_row, lam_init, tm_c, n_ctx, None,
                              proj_cols=(PROJ_K_TILE, 2))
            ctx_kv = (pc.reshape(b, n_ctx, 2 * PROJ_TN), 0, PROJ_TN // LANES)
        else:
            hc_new, xc_new, pc = _mixer(hc, xc, n_ctx, None, dft_c, None, w, mods, ctx_row, lam_init,
                                        tm_c, n_ctx, nxt)
            ctx_kv = (pc.reshape(b, n_ctx, N_IN), P_K_OFF // LANES, P_V_OFF // LANES)
        h, xl, _ = _mixer(h, xl, seq, rope_tabs, dft_l, ctx_kv, w, mods, lat_row, lam_init, tm_l, tq, nxt)
        if not last:
            hc, xc = hc_new, xc_new
    return h.reshape(b, seq, d)
```

```python
import functools
import math

import numpy as np
import jax
import jax.numpy as jnp
from jax import lax
from jax.experimental import pallas as pl
from jax.experimental.pallas import tpu as pltpu

F32 = jnp.float32
BF16 = jnp.bfloat16

D_MODEL = 2048
GRID_W = 64
FOURIER_GROUPS = 4
FOURIER_GROUP_W = D_MODEL // 16
FOURIER_W = FOURIER_GROUPS * FOURIER_GROUP_W
CONV_W = D_MODEL // 4
N_HEADS = 8
HEAD_DIM = D_MODEL // (4 * N_HEADS)
VAL_DIM = 2 * HEAD_DIM
ATTN_QK_W = N_HEADS * 2 * HEAD_DIM
ATTN_V_W = N_HEADS * VAL_DIM
ROPE_THETA = 10000.0
ATTN_SCALE = HEAD_DIM ** -0.5
SUBLN_EPS = 1e-5
Q_OFF = FOURIER_W + 3 * CONV_W
K_OFF = Q_OFF + ATTN_QK_W
V_OFF = K_OFF + ATTN_QK_W
V_END = V_OFF + ATTN_V_W
N_IN = V_END + 3 * D_MODEL
D_FF = ((8 * D_MODEL // 3 + 255) // 256) * 256
EPS = 1e-6

LANES = 128
MXU_COLS = 256
BF16_SUBLANES = 16
VMEM_LIMIT = 56 * 1024 * 1024
MOD_ROWS = 8

PROJ_TN = 1024
PROJ_TM = 1024
P_GATE_OFF = Q_OFF
P_Q_OFF = P_GATE_OFF + 3 * D_MODEL
P_K_OFF = P_Q_OFF + ATTN_QK_W
P_V_OFF = P_K_OFF + ATTN_QK_W
PROJ_GATE_TILE = P_GATE_OFF // PROJ_TN
PROJ_Q_TILE = P_Q_OFF // PROJ_TN
PROJ_K_TILE = P_K_OFF // PROJ_TN
PROJ_TILES = N_IN // PROJ_TN
ATTN_TK = 512
ATTN_SCORES_AHEAD = 1
FOURIER_TR = 512
MERGE_TM = 256
FFN_TF = 512
FFN_UP_TM = 1024
FFN_DOWN_TM = 256


def _cparams(*sem):
    return pltpu.CompilerParams(dimension_semantics=sem, vmem_limit_bytes=VMEM_LIMIT)


def _mod_kernel(ct_ref, w_ref, b_ref, o_ref, *, n_rows):
    ct = ct_ref[...]
    st = ct * jax.nn.sigmoid(ct)
    w = w_ref[...]
    d, tn = w.shape
    rows = []
    for r in range(n_rows):
        t = (st[:, r:r + 1] * w).reshape(d // MOD_ROWS, MOD_ROWS, tn)
        rows.append(jnp.sum(jnp.sum(t, axis=0), axis=0, keepdims=True))
    rows.append(jnp.zeros((MOD_ROWS - n_rows, tn), F32))
    o_ref[...] = jnp.concatenate(rows, axis=0) + b_ref[...]


def _modulation(cvec_t, n_rows, w_mod, b_mod):
    depth, d, n = w_mod.shape
    tn = 1024
    return pl.pallas_call(
        functools.partial(_mod_kernel, n_rows=n_rows),
        grid=(depth, n // tn),
        in_specs=[
            pl.BlockSpec((d, MOD_ROWS), lambda l, j: (0, 0)),
            pl.BlockSpec((None, d, tn), lambda l, j: (l, 0, j)),
            pl.BlockSpec((None, 1, tn), lambda l, j: (l, 0, j)),
        ],
        out_specs=pl.BlockSpec((None, MOD_ROWS, tn), lambda l, j: (l, 0, j)),
        out_shape=jax.ShapeDtypeStruct((depth, MOD_ROWS, n), F32),
        compiler_params=_cparams("arbitrary", "arbitrary"),
        name="modulation",
    )(cvec_t, w_mod, b_mod.reshape(depth, 1, n))


def _mod_spec(which, mod_row, tm):
    return pl.BlockSpec((None, None, 1, D_MODEL), lambda i, *_: (mod_row(i * tm), which, 0, 0))


def _rms(x, g):
    return x * lax.rsqrt(jnp.mean(x * x, axis=-1, keepdims=True) + EPS) * g


def _norm_mod_kernel(h_ref, g_ref, sh_ref, sc_ref, o_ref):
    y = _rms(h_ref[...], g_ref[...])
    o_ref[...] = (y * (1.0 + sc_ref[...]) + sh_ref[...]).astype(o_ref.dtype)


def _norm_modulate(h, g, mods, mod_row, tm):
    m, d = h.shape
    return pl.pallas_call(
        _norm_mod_kernel,
        grid=(m // tm,),
        in_specs=[
            pl.BlockSpec((tm, d), lambda i: (i, 0)),
            pl.BlockSpec((1, d), lambda i: (0, 0)),
            _mod_spec(0, mod_row, tm),
            _mod_spec(1, mod_row, tm),
        ],
        out_specs=pl.BlockSpec((tm, d), lambda i: (i, 0)),
        out_shape=jax.ShapeDtypeStruct((m, d), BF16),
        compiler_params=_cparams("arbitrary"),
        name="norm_modulate",
    )(h, g.reshape(1, d), mods, mods)


def _proj_kernel(*refs, col0, rope):
    if rope:
        x_ref, w_ref, cos_ref, sin_up_ref, sin_dn_ref, o_ref, wb_ref = refs
    else:
        x_ref, w_ref, o_ref, wb_ref = refs
    j = pl.program_id(0) + col0

    @pl.when(pl.program_id(1) == 0)
    def _():
        wb_ref[...] = w_ref[...].astype(BF16)

    is_q = j == PROJ_Q_TILE
    is_k = j == PROJ_K_TILE
    x = x_ref[...]

    def column_chunks(epilogue):
        for c in range(PROJ_TN // MXU_COLS):
            cols = slice(c * MXU_COLS, (c + 1) * MXU_COLS)
            acc = jnp.dot(x, wb_ref[:, cols], preferred_element_type=F32)
            o_ref[:, cols] = epilogue(acc).astype(o_ref.dtype)

    def rotary(acc):
        qscale = jnp.where(is_q, ATTN_SCALE * math.log2(math.e), 1.0)
        half = HEAD_DIM // 4
        out = []
        for hd in range(MXU_COLS // LANES):
            a = acc[:, hd * LANES:(hd + 1) * LANES]
            if rope:
                a = (a * cos_ref[...] + pltpu.roll(a, LANES - half, 1) * sin_up_ref[...]
                     + pltpu.roll(a, half, 1) * sin_dn_ref[...])
            out.append(a * qscale)
        return jnp.concatenate(out, axis=1)

    pl.when(is_q | is_k)(lambda: column_chunks(rotary))
    pl.when(jnp.logical_not(is_q | is_k))(lambda: column_chunks(lambda acc: acc))


def _in_proj(xm, w_in, layer, rope_tabs, tm, col0, ncols, seq_len):
    m, d = xm.shape
    tn = PROJ_TN
    n_plain = PROJ_GATE_TILE
    n_gate = PROJ_Q_TILE - PROJ_GATE_TILE
    n_qkv = PROJ_TILES - PROJ_Q_TILE

    def w_tile(j):
        j = j + col0
        return jnp.where(j < n_plain, j, jnp.where(j < n_plain + n_gate, j + n_qkv, j - n_gate))

    in_specs = [
        pl.BlockSpec((tm, d), lambda j, i: (i, 0)),
        pl.BlockSpec((None, d, tn), lambda j, i: (layer, 0, w_tile(j))),
    ]
    args = [xm, w_in]
    rope = rope_tabs is not None
    if rope:
        tps = seq_len // tm
        in_specs += [pl.BlockSpec((tm, LANES), lambda j, i: (i % tps, 0))] * 3
        args += list(rope_tabs)
    return pl.pallas_call(
        functools.partial(_proj_kernel, col0=col0, rope=rope),
        grid=(ncols, m // tm),
        in_specs=in_specs,
        out_specs=pl.BlockSpec((tm, tn), lambda j, i: (i, j)),
        out_shape=jax.ShapeDtypeStruct((m, ncols * tn), BF16),
        scratch_shapes=[pltpu.VMEM((d, tn), BF16)],
        compiler_params=_cparams("arbitrary", "arbitrary"),
        name="in_proj",
    )(*args)


def _fourier_kernel(f_ref, cs_ref, c0_ref, s0_ref, rc_ref, rs_ref, o_ref, uc_ref, us_ref, *, chunk, scale):
    i = pl.program_id(1)
    seq = f_ref.shape[0]
    gw = FOURIER_GROUP_W

    @pl.when(i == 0)
    def _():
        for r in range(seq // chunk):
            rows = slice(r * chunk, (r + 1) * chunk)
            for g in range(FOURIER_GROUPS):
                cols = slice(g * gw, (g + 1) * gw)
                y = jnp.dot(f_ref[rows, cols], cs_ref[...], preferred_element_type=F32)
                uc_ref[rows, cols] = y[:, :gw].astype(BF16)
                us_ref[rows, cols] = y[:, gw:].astype(BF16)

    c0, s0 = c0_ref[...].astype(F32), s0_ref[...].astype(F32)
    rc, rs = rc_ref[pl.ds(i, 1), :], rs_ref[pl.ds(i, 1), :]
    cos_t = (c0 * rc - s0 * rs).astype(BF16)
    sin_t = (s0 * rc + c0 * rs).astype(BF16)
    y = jnp.dot(cos_t, uc_ref[...], preferred_element_type=F32)
    y -= jnp.dot(sin_t, us_ref[...], preferred_element_type=F32)
    o_ref[...] = (y * scale).astype(o_ref.dtype)


def _fourier(p3, cs, dft, tr):
    b, seq, _ = p3.shape
    chunk = min(seq, 1024)
    scale = 1.0 / math.sqrt(seq * FOURIER_GROUP_W)
    c0, s0, rc, rs = dft
    full = lambda a: pl.BlockSpec(a.shape, lambda bi, i: (0, 0), pipeline_mode=pl.Buffered(1))
    return pl.pallas_call(
        functools.partial(_fourier_kernel, chunk=chunk, scale=scale),
        grid=(b, seq // tr),
        in_specs=[
            pl.BlockSpec((None, seq, FOURIER_W), lambda bi, i: (bi, 0, 0)),
            pl.BlockSpec((FOURIER_GROUP_W, 2 * FOURIER_GROUP_W), lambda bi, i: (0, 0)),
            full(c0), full(s0), full(rc), full(rs),
        ],
        out_specs=pl.BlockSpec((None, tr, FOURIER_W), lambda bi, i: (bi, i, 0)),
        out_shape=jax.ShapeDtypeStruct((b, seq, FOURIER_W), BF16),
        scratch_shapes=[pltpu.VMEM((seq, FOURIER_W), BF16), pltpu.VMEM((seq, FOURIER_W), BF16)],
        compiler_params=_cparams("arbitrary", "arbitrary"),
        name="fourier",
    )(p3, cs, c0, s0, rc, rs)


def _attn_kernel(*refs, tq, tk, n_lat, lam_init):
    if n_lat:
        lam_ref, q_ref, kc_ref, vc_ref, kl_ref, vl_ref, g_ref, o_ref = refs
    else:
        lam_ref, q_ref, kc_ref, vc_ref, g_ref, o_ref = refs
        kl_ref = vl_ref = None
    n_ctx = kc_ref.shape[0]
    q = q_ref[...]
    lane = lax.broadcasted_iota(jnp.int32, (1, LANES), 1)
    comp0 = lane < HEAD_DIM
    zero = jnp.zeros_like(q)
    qq = jnp.concatenate([jnp.where(comp0, q, zero), jnp.where(comp0, zero, q)], axis=0)
    chunks = [(kc_ref, vc_ref, 0, n_ctx)] + [(kl_ref, vl_ref, c * tk, tk) for c in range(n_lat)]

    def lane_fold(x, op):
        part = x[:, 0:LANES]
        for c in range(1, x.shape[1] // LANES):
            part = op(part, x[:, c * LANES:(c + 1) * LANES])
        return part

    def scores(c):
        kr, _, r0, sz = chunks[c]
        return lax.dot_general(qq, kr[r0:r0 + sz, :], (((1,), (1,)), ((), ())), preferred_element_type=F32)

    m = l = acc = None
    ahead = ATTN_SCORES_AHEAD
    pending = [scores(c) for c in range(min(ahead, len(chunks)))]
    for c, (_, vr, r0, sz) in enumerate(chunks):
        s = pending.pop(0)
        if c + ahead < len(chunks):
            pending.append(scores(c + ahead))
        mc = jnp.max(lane_fold(s, jnp.maximum), axis=-1, keepdims=True)
        m_new = mc if m is None else jnp.maximum(m, mc)
        p = jnp.exp2(s - m_new)
        lc = lane_fold(p, jnp.add)
        y = jnp.dot(p.astype(BF16), vr[r0:r0 + sz, :], preferred_element_type=F32)
        if m is None:
            l, acc = lc, y
        else:
            alpha = jnp.exp2(m - m_new)
            l = alpha * l + lc
            acc = alpha * acc + y
        m = m_new
    o = acc / jnp.sum(l, axis=-1, keepdims=True)
    lf = lam_ref[...]
    lam = (jnp.exp(jnp.sum(lf[0:1] * lf[1:2], axis=-1, keepdims=True))
           - jnp.exp(jnp.sum(lf[2:3] * lf[3:4], axis=-1, keepdims=True)) + lam_init)
    o = o[:tq] - lam * o[tq:]
    o = o * lax.rsqrt(jnp.mean(o * o, axis=-1, keepdims=True) + SUBLN_EPS)
    o_ref[...] = (o * g_ref[...] * (1.0 - lam_init)).astype(o_ref.dtype)


def _attention(q3, qblk, c3, kcblk, vcblk, l3, lam, subln_g, lam_init, tq):
    b, lq, _ = q3.shape
    n_ctx = c3.shape[1]
    tk = ATTN_TK
    in_specs = [
        pl.BlockSpec((4, HEAD_DIM), lambda bi, h, i: (0, 0)),
        pl.BlockSpec((None, tq, LANES), lambda bi, h, i: (bi, i, qblk + h)),
        pl.BlockSpec((None, n_ctx, LANES), lambda bi, h, i: (bi, 0, kcblk + h)),
        pl.BlockSpec((None, n_ctx, LANES), lambda bi, h, i: (bi, 0, vcblk + h)),
    ]
    args = [lam, q3, c3, c3]
    n_lat = 0
    if l3 is not None:
        seq = l3.shape[1]
        n_lat = seq // min(tk, seq)
        tk = seq // n_lat
        in_specs += [
            pl.BlockSpec((None, seq, LANES), lambda bi, h, i: (bi, 0, P_K_OFF // LANES + h)),
            pl.BlockSpec((None, seq, LANES), lambda bi, h, i: (bi, 0, P_V_OFF // LANES + h)),
        ]
        args += [l3, l3]
    in_specs.append(pl.BlockSpec((1, VAL_DIM), lambda bi, h, i: (0, 0)))
    args.append(subln_g.reshape(1, VAL_DIM))
    return pl.pallas_call(
        functools.partial(_attn_kernel, tq=tq, tk=tk, n_lat=n_lat, lam_init=lam_init),
        grid=(b, N_HEADS, lq // tq),
        in_specs=in_specs,
        out_specs=pl.BlockSpec((None, tq, LANES), lambda bi, h, i: (bi, i, h)),
        out_shape=jax.ShapeDtypeStruct((b, lq, ATTN_V_W), BF16),
        compiler_params=_cparams("arbitrary", "arbitrary", "arbitrary"),
        name="diff_attention",
    )(*args)


def _conv3_rows(buf_ref, w, tm, base=BF16_SUBLANES):
    return (buf_ref[base - 1:base - 1 + tm, :] * w[0:1] + buf_ref[base:base + tm, :] * w[1:2]
            + buf_ref[base + 1:base + 1 + tm, :] * w[2:3])


def _halo_specs(tm, width, colblk, m, row_axis=0):
    per = tm // BF16_SUBLANES
    last = m // BF16_SUBLANES - 1
    prev = pl.BlockSpec((BF16_SUBLANES, width),
                        lambda *ids: (jnp.maximum(ids[row_axis] * per - 1, 0), colblk))
    nxt = pl.BlockSpec((BF16_SUBLANES, width),
                       lambda *ids: (jnp.minimum((ids[row_axis] + 1) * per, last), colblk))
    return prev, nxt


def _merge_kernel(f_ref, cb_ref, cc_ref, cx_ref, ccp_ref, cxp_ref, ccn_ref, cxn_ref, a_ref,
                  gf_ref, gc_ref, ga_ref, cw_ref, wbf_ref, wbc_ref, wba_ref, wo_ref,
                  h_ref, gt_ref, g2_ref, sh_ref, sc_ref, ho_ref, xo_ref, z_ref, mg_ref, *, tps, nchunk):
    i = pl.program_id(0)
    tm = cc_ref.shape[0]
    h = BF16_SUBLANES
    first = (i % tps) == 0
    last = (i % tps) == tps - 1
    zp = ccp_ref[...].astype(F32) * cxp_ref[...].astype(F32)
    zn = ccn_ref[...].astype(F32) * cxn_ref[...].astype(F32)
    z_ref[0:h, :] = jnp.where(first, 0.0, zp)
    z_ref[h:h + tm, :] = cc_ref[...].astype(F32) * cx_ref[...].astype(F32)
    z_ref[h + tm:2 * h + tm, :] = jnp.where(last, 0.0, zn)
    cv = (cb_ref[...].astype(F32) * _conv3_rows(z_ref, cw_ref[...], tm)).astype(BF16)
    f = f_ref[...]
    a = a_ref[...]
    cn = D_MODEL // nchunk
    for n in range(nchunk):
        cols = slice(n * cn, (n + 1) * cn)
        gate = lambda g_ref: jax.nn.sigmoid(g_ref[:, cols].astype(F32))
        y = gate(gf_ref) * jnp.dot(f, wbf_ref[:, cols], preferred_element_type=F32)
        y += gate(gc_ref) * jnp.dot(cv, wbc_ref[:, cols], preferred_element_type=F32)
        y += gate(ga_ref) * jnp.dot(a, wba_ref[:, cols], preferred_element_type=F32)
        mg_ref[:, cols] = y.astype(BF16)
    y = jnp.dot(mg_ref[...], wo_ref[...], preferred_element_type=F32)
    h_new = h_ref[...] + gt_ref[...] * y
    ho_ref[...] = h_new
    xo_ref[...] = (_rms(h_new, g2_ref[...]) * (1.0 + sc_ref[...]) + sh_ref[...]).astype(xo_ref.dtype)


def _merge_out(p, fmix, attn, conv_w, w_bf, w_bc, w_ba, w_out, layer, h, g2, mods, mod_row, tm, seq_len):
    m = p.shape[0]
    d = D_MODEL
    cw = CONV_W
    tps = seq_len // tm
    blk = lambda width, c: pl.BlockSpec((tm, width), lambda i: (i, c))
    ccp, ccn = _halo_specs(tm, cw, 2, m)
    cxp, cxn = _halo_specs(tm, cw, 3, m)
    gate0 = P_GATE_OFF // d
    full = lambda r, c: pl.BlockSpec((None, r, c), lambda i: (layer, 0, 0), pipeline_mode=pl.Buffered(1))
    return pl.pallas_call(
        functools.partial(_merge_kernel, tps=tps, nchunk=4),
        grid=(m // tm,),
        in_specs=[
            blk(FOURIER_W, 0),
            blk(cw, 1), blk(cw, 2), blk(cw, 3),
            ccp, cxp, ccn, cxn,
            blk(ATTN_V_W, 0),
            pl.BlockSpec((tm, d), lambda i: (i, gate0)),
            pl.BlockSpec((tm, d), lambda i: (i, gate0 + 1)),
            pl.BlockSpec((tm, d), lambda i: (i, gate0 + 2)),
            full(3, cw), full(FOURIER_W, d), full(cw, d), full(ATTN_V_W, d), full(d, d),
            pl.BlockSpec((tm, d), lambda i: (i, 0)),
            _mod_spec(2, mod_row, tm),
            pl.BlockSpec((1, d), lambda i: (0, 0)),
            _mod_spec(3, mod_row, tm),
            _mod_spec(4, mod_row, tm),
        ],
        out_specs=[pl.BlockSpec((tm, d), lambda i: (i, 0)), pl.BlockSpec((tm, d), lambda i: (i, 0))],
        out_shape=[jax.ShapeDtypeStruct((m, d), F32), jax.ShapeDtypeStruct((m, d), BF16)],
        scratch_shapes=[pltpu.VMEM((tm + 2 * BF16_SUBLANES, cw), F32), pltpu.VMEM((tm, d), BF16)],
        compiler_params=_cparams("arbitrary"),
        name="merge_out",
    )(fmix, p, p, p, p, p, p, p, attn, p, p, p, conv_w, w_bf, w_bc, w_ba, w_out,
      h, mods, g2.reshape(1, d), mods, mods)


def _ffn_up_kernel(*refs, tps, nseq, cast_down):
    if cast_down:
        (x_ref, xp_ref, xn_ref, wa_ref, wv_ref, ca_ref, cv_ref, wd_ref, o_ref, wdb_ref,
         wab_ref, wvb_ref, xh_ref, ua_ref, uv_ref) = refs
    else:
        (x_ref, xp_ref, xn_ref, wa_ref, wv_ref, ca_ref, cv_ref, o_ref,
         wab_ref, wvb_ref, xh_ref, ua_ref, uv_ref) = refs
    i = pl.program_id(1)
    tm = x_ref.shape[0]
    hl = BF16_SUBLANES
    seq = tm // nseq

    @pl.when(i == 0)
    def _():
        wab_ref[...] = wa_ref[...].astype(BF16)
        wvb_ref[...] = wv_ref[...].astype(BF16)
        if cast_down:
            wdb_ref[...] = wd_ref[...].astype(BF16)

    base = [hl + s * (seq + hl) for s in range(nseq)]
    zeros = jnp.zeros((hl, x_ref.shape[1]), BF16)
    if nseq == 1:
        first = (i % tps) == 0
        last = (i % tps) == tps - 1
        xh_ref[0:hl, :] = jnp.where(first, zeros, xp_ref[...])
        xh_ref[hl + tm:2 * hl + tm, :] = jnp.where(last, zeros, xn_ref[...])
    else:
        for s in range(nseq + 1):
            xh_ref[s * (seq + hl):s * (seq + hl) + hl, :] = zeros
    for s in range(nseq):
        xh_ref[base[s]:base[s] + seq, :] = x_ref[s * seq:(s + 1) * seq, :]
    xh = xh_ref[...]
    ua_ref[...] = jnp.dot(xh, wab_ref[...], preferred_element_type=F32)
    uv_ref[...] = jnp.dot(xh, wvb_ref[...], preferred_element_type=F32)
    for s in range(nseq):
        a = _conv3_rows(ua_ref, ca_ref[...], seq, base[s])
        v = _conv3_rows(uv_ref, cv_ref[...], seq, base[s])
        o_ref[s * seq:(s + 1) * seq, :] = (a * jax.nn.sigmoid(a) * v).astype(o_ref.dtype)


def _ffn_down_kernel(a_ref, w_ref, h_ref, gt_ref, g_ref, sh_ref, sc_ref, ho_ref, xo_ref):
    y = jnp.dot(a_ref[...], w_ref[...], preferred_element_type=F32)
    h_new = h_ref[...] + gt_ref[...] * y
    ho_ref[...] = h_new
    xo_ref[...] = (_rms(h_new, g_ref[...]) * (1.0 + sc_ref[...]) + sh_ref[...]).astype(xo_ref.dtype)


def _ffn_down_final_kernel(a_ref, w_ref, h_ref, gt_ref, g_ref, o_ref):
    y = jnp.dot(a_ref[...], w_ref[...], preferred_element_type=F32)
    o_ref[...] = _rms(h_ref[...] + gt_ref[...] * y, g_ref[...])


def _ffn(xm, w_up, conv_w, w_down, w_down_b, layer, h, mods, mod_row, seq_len, nxt):
    m, d = xm.shape
    tf = FFN_TF
    nj = D_FF // tf
    tm = FFN_UP_TM if seq_len >= FFN_UP_TM else min(m, FFN_UP_TM)
    tps = max(seq_len // tm, 1)
    nseq = max(tm // seq_len, 1)
    stage_rows = nseq * (tm // nseq + BF16_SUBLANES) + BF16_SUBLANES
    xp, xn = _halo_specs(tm, d, 0, m, row_axis=1)
    cast_down = w_down_b is None
    in_specs = [
        pl.BlockSpec((tm, d), lambda j, i: (i, 0)),
        xp, xn,
        pl.BlockSpec((None, d, tf), lambda j, i: (layer, 0, j)),
        pl.BlockSpec((None, d, tf), lambda j, i: (layer, 0, nj + j)),
        pl.BlockSpec((None, 3, tf), lambda j, i: (layer, 0, j)),
        pl.BlockSpec((None, 3, tf), lambda j, i: (layer, 0, nj + j)),
    ]
    args = [xm, xm, xm, w_up, w_up, conv_w, conv_w]
    out_specs = [pl.BlockSpec((tm, tf), lambda j, i: (i, j))]
    out_shape = [jax.ShapeDtypeStruct((m, D_FF), BF16)]
    if cast_down:
        in_specs.append(pl.BlockSpec((None, tf, d), lambda j, i: (layer, j, 0)))
        args.append(w_down)
        out_specs.append(pl.BlockSpec((tf, d), lambda j, i: (j, 0)))
        out_shape.append(jax.ShapeDtypeStruct((D_FF, d), BF16))
    outs = pl.pallas_call(
        functools.partial(_ffn_up_kernel, tps=tps, nseq=nseq, cast_down=cast_down),
        grid=(nj, m // tm),
        in_specs=in_specs,
        out_specs=out_specs,
        out_shape=out_shape,
        scratch_shapes=[
            pltpu.VMEM((d, tf), BF16),
            pltpu.VMEM((d, tf), BF16),
            pltpu.VMEM((stage_rows, d), BF16),
            pltpu.VMEM((stage_rows, tf), F32),
            pltpu.VMEM((stage_rows, tf), F32),
        ],
        compiler_params=_cparams("arbitrary", "arbitrary"),
        name="ffn_up",
    )(*args)
    act = outs[0]
    if cast_down:
        w_down_b = outs[1]

    tmd = min(m, FFN_DOWN_TM)
    row = lambda i: (i, 0)
    in_specs = [
        pl.BlockSpec((tmd, D_FF), row),
        pl.BlockSpec((D_FF, d), lambda i: (0, 0), pipeline_mode=pl.Buffered(1)),
        pl.BlockSpec((tmd, d), row),
        _mod_spec(5, mod_row, tmd),
        pl.BlockSpec((1, d), lambda i: (0, 0)),
    ]
    g_next, mods_next = nxt
    if mods_next is None:
        return pl.pallas_call(
            _ffn_down_final_kernel,
            grid=(m // tmd,),
            in_specs=in_specs,
            out_specs=pl.BlockSpec((tmd, d), row),
            out_shape=jax.ShapeDtypeStruct((m, d), F32),
            compiler_params=_cparams("arbitrary"),
            name="ffn_down_final",
        )(act, w_down_b, h, mods, g_next.reshape(1, d)), None, w_down_b
    h_new, x_next = pl.pallas_call(
        _ffn_down_kernel,
        grid=(m // tmd,),
        in_specs=in_specs + [_mod_spec(0, mod_row, tmd), _mod_spec(1, mod_row, tmd)],
        out_specs=[pl.BlockSpec((tmd, d), row), pl.BlockSpec((tmd, d), row)],
        out_shape=[jax.ShapeDtypeStruct((m, d), F32), jax.ShapeDtypeStruct((m, d), BF16)],
        compiler_params=_cparams("arbitrary"),
        name="ffn_down",
    )(act, w_down_b, h, mods, g_next.reshape(1, d), mods_next, mods_next)
    return h_new, x_next, w_down_b


def _rope_tables(length):
    n_freq = HEAD_DIM // 4
    pos = jnp.arange(length)
    row = (pos // GRID_W).astype(F32)
    col = (pos % GRID_W).astype(F32)
    inv = ROPE_THETA ** (-(2.0 * jnp.arange(n_freq, dtype=F32)) / (HEAD_DIM // 2))
    ang_r, ang_c = row[:, None] * inv, col[:, None] * inv
    ang = jnp.concatenate([ang_r, ang_r, ang_c, ang_c], axis=1)
    cos = jnp.tile(jnp.cos(ang), (1, 2))
    sin = jnp.tile(jnp.sin(ang), (1, 2))
    is_x1 = (jnp.arange(LANES) % (2 * n_freq)) < n_freq
    return cos, jnp.where(is_x1, -sin, 0.0), jnp.where(is_x1, 0.0, sin)


def _dft_tables(n, tr):
    m = jnp.arange(n, dtype=jnp.int32)

    def cos_sin(rows):
        ang = ((rows[:, None] * m[None, :]) % n).astype(F32) * (2.0 * math.pi / n)
        return jnp.cos(ang), jnp.sin(ang)

    rc, rs = cos_sin(jnp.arange(n // tr, dtype=jnp.int32) * tr)
    g = 1 << ((tr.bit_length() - 1 + 1) // 2)
    assert tr % g == 0
    ca, sa = cos_sin(jnp.arange(tr // g, dtype=jnp.int32) * g)
    cb, sb = cos_sin(jnp.arange(g, dtype=jnp.int32))
    c0 = (ca[:, None, :] * cb[None, :, :] - sa[:, None, :] * sb[None, :, :]).reshape(tr, n)
    s0 = (sa[:, None, :] * cb[None, :, :] + ca[:, None, :] * sb[None, :, :]).reshape(tr, n)
    return c0.astype(BF16), s0.astype(BF16), rc, rs


def _channel_dft():
    c = np.arange(FOURIER_GROUP_W)
    ang = 2.0 * np.pi * ((c[:, None] * c[None, :]) % FOURIER_GROUP_W) / FOURIER_GROUP_W
    return jnp.asarray(np.concatenate([np.cos(ang), np.sin(ang)], axis=1), dtype=BF16)


def _mixer(h, xm, seq_len, rope_tabs, dft, ctx_kv, w, mods, mod_row, lam_init, tm, tq, nxt,
           proj_cols=(0, PROJ_TILES)):
    m = h.shape[0]
    b = m // seq_len
    p = _in_proj(xm, w["w_in"], w["layer"], rope_tabs, min(m, PROJ_TM), proj_cols[0], proj_cols[1], seq_len)
    if proj_cols[1] != PROJ_TILES:
        return None, None, p
    p3 = p.reshape(b, seq_len, N_IN)
    fmix = _fourier(p3, w["cs"], dft, min(seq_len, FOURIER_TR)).reshape(m, FOURIER_W)
    if ctx_kv is None:
        attn = _attention(p3, P_Q_OFF // LANES, p3, P_K_OFF // LANES, P_V_OFF // LANES, None,
                          w["lambdas"], w["subln_g"], lam_init, tq)
    else:
        c3, kcblk, vcblk = ctx_kv
        attn = _attention(p3, P_Q_OFF // LANES, c3, kcblk, vcblk, p3,
                          w["lambdas"], w["subln_g"], lam_init, tq)
    h_mid, xm2 = _merge_out(p, fmix, attn.reshape(m, ATTN_V_W), w["conv_mix_w"], w["w_br_fourier"],
                            w["w_br_conv"], w["w_br_attn"], w["w_out"], w["layer"], h, w["g_norm2"],
                            mods, mod_row, min(tm, MERGE_TM), seq_len)
    h_new, x_next, w["w_ffn_down_b"] = _ffn(xm2, w["w_ffn_up"], w["ffn_conv_w"], w["w_ffn_down"],
                                            w["w_ffn_down_b"], w["layer"], h_mid, mods, mod_row, seq_len, nxt)
    return h_new, x_next, p


def kernel(x, c, ctx, c_ctx, w_mod, b_mod, g_norm1, g_norm2, w_in, conv_mix_w, lambdas, subln_g,
           w_br_fourier, w_br_conv, w_br_attn, w_out, w_ffn_up, ffn_conv_w, w_ffn_down, g_final):
    b, seq, d = x.shape
    n_ctx = ctx.shape[1]
    depth = w_mod.shape[0]
    assert d == D_MODEL and b + 1 <= MOD_ROWS

    wb = lambda a: a.astype(BF16)
    w_bf, w_bc, w_ba, w_o = wb(w_br_fourier), wb(w_br_conv), wb(w_br_attn), wb(w_out)

    rope_tabs = _rope_tables(seq)
    dft_l = _dft_tables(seq, min(seq, FOURIER_TR))
    dft_c = _dft_tables(n_ctx, min(n_ctx, FOURIER_TR))
    cs = _channel_dft()

    cvec = jnp.zeros((MOD_ROWS, d), F32).at[:b].set(c).at[b].set(c_ctx)
    mods_all = _modulation(cvec.T, b + 1, w_mod, b_mod).reshape(depth, MOD_ROWS, 6, 1, d)

    tm_l = min(512, seq)
    tm_c = n_ctx
    tq = min(512, seq)
    lat_row = lambda r: r // seq
    ctx_row = lambda r: b

    h = x.reshape(b * seq, d)
    hc = ctx.reshape(b * n_ctx, d)
    xc = _norm_modulate(hc, g_norm1[0], mods_all[0], ctx_row, tm_c)
    xl = _norm_modulate(h, g_norm1[0], mods_all[0], lat_row, tm_l)
    for i in range(depth):
        last = i == depth - 1
        lam_init = 0.8 - 0.6 * math.exp(-0.3 * i)
        mods = mods_all[i]
        w = dict(w_in=w_in, layer=i, cs=cs, lambdas=lambdas[i], subln_g=subln_g[i], conv_mix_w=conv_mix_w,
                 w_br_fourier=w_bf, w_br_conv=w_bc, w_br_attn=w_ba, w_out=w_o,
                 g_norm2=g_norm2[i], w_ffn_up=w_ffn_up, ffn_conv_w=ffn_conv_w, w_ffn_down=w_ffn_down,
                 w_ffn_down_b=None)
        nxt = (g_final, None) if last else (g_norm1[i + 1], mods_all[i + 1])

        if last:
            _, _, pc = _mixer(hc, xc, n_ctx, None, dft_c, None, w, mods, ctx_row, lam_init, tm_c, n_ctx, None,
                              proj_cols=(PROJ_K_TILE, 2))
            ctx_kv = (pc.reshape(b, n_ctx, 2 * PROJ_TN), 0, PROJ_TN // LANES)
        else:
            hc_new, xc_new, pc = _mixer(hc, xc, n_ctx, None, dft_c, None, w, mods, ctx_row, lam_init,
                                        tm_c, n_ctx, nxt)
            ctx_kv = (pc.reshape(b, n_ctx, N_IN), P_K_OFF // LANES, P_V_OFF // LANES)
        h, xl, _ = _mixer(h, xl, seq, rope_tabs, dft_l, ctx_kv, w, mods, lat_row, lam_init, tm_l, tq, nxt)
        if not last:
            hc, xc = hc_new, xc_new
    return h.reshape(b, seq, d)
```

```python
import functools
import math

import numpy as np
import jax
import jax.numpy as jnp
from jax import lax
from jax.experimental import pallas as pl
from jax.experimental.pallas import tpu as pltpu

F32 = jnp.float32
BF16 = jnp.bfloat16

D_MODEL = 2048
GRID_W = 64
FOURIER_GROUPS = 4
FOURIER_GROUP_W = D_MODEL // 16
FOURIER_W = FOURIER_GROUPS * FOURIER_GROUP_W
CONV_W = D_MODEL // 4
N_HEADS = 8
HEAD_DIM = D_MODEL // (4 * N_HEADS)
VAL_DIM = 2 * HEAD_DIM
ATTN_QK_W = N_HEADS * 2 * HEAD_DIM
ATTN_V_W = N_HEADS * VAL_DIM
ROPE_THETA = 10000.0
ATTN_SCALE = HEAD_DIM ** -0.5
SUBLN_EPS = 1e-5
Q_OFF = FOURIER_W + 3 * CONV_W
K_OFF = Q_OFF + ATTN_QK_W
V_OFF = K_OFF + ATTN_QK_W
V_END = V_OFF + ATTN_V_W
N_IN = V_END + 3 * D_MODEL
D_FF = ((8 * D_MODEL // 3 + 255) // 256) * 256
EPS = 1e-6

LANES = 128
MXU_COLS = 256
BF16_SUBLANES = 16
VMEM_LIMIT = 56 * 1024 * 1024
MOD_ROWS = 8

PROJ_TN = 1024
PROJ_TM = 1024
P_GATE_OFF = Q_OFF
P_Q_OFF = P_GATE_OFF + 3 * D_MODEL
P_K_OFF = P_Q_OFF + ATTN_QK_W
P_V_OFF = P_K_OFF + ATTN_QK_W
PROJ_GATE_TILE = P_GATE_OFF // PROJ_TN
PROJ_Q_TILE = P_Q_OFF // PROJ_TN
PROJ_K_TILE = P_K_OFF // PROJ_TN
PROJ_TILES = N_IN // PROJ_TN
ATTN_TK = 512
ATTN_SCORES_AHEAD = 1
FOURIER_TR = 512
MERGE_TM = 256
FFN_TF = 512
FFN_UP_TM = 1024
FFN_DOWN_TM = 256


def _cparams(*sem):
    return pltpu.CompilerParams(dimension_semantics=sem, vmem_limit_bytes=VMEM_LIMIT)


def _mod_kernel(ct_ref, w_ref, b_ref, o_ref, *, n_rows):
    ct = ct_ref[...]
    st = ct * jax.nn.sigmoid(ct)
    w = w_ref[...]
    d, tn = w.shape
    rows = []
    for r in range(n_rows):
        t = (st[:, r:r + 1] * w).reshape(d // MOD_ROWS, MOD_ROWS, tn)
        rows.append(jnp.sum(jnp.sum(t, axis=0), axis=0, keepdims=True))
    rows.append(jnp.zeros((MOD_ROWS - n_rows, tn), F32))
    o_ref[...] = jnp.concatenate(rows, axis=0) + b_ref[...]


def _modulation(cvec_t, n_rows, w_mod, b_mod):
    depth, d, n = w_mod.shape
    tn = 1024
    return pl.pallas_call(
        functools.partial(_mod_kernel, n_rows=n_rows),
        grid=(depth, n // tn),
        in_specs=[
            pl.BlockSpec((d, MOD_ROWS), lambda l, j: (0, 0)),
            pl.BlockSpec((None, d, tn), lambda l, j: (l, 0, j)),
            pl.BlockSpec((None, 1, tn), lambda l, j: (l, 0, j)),
        ],
        out_specs=pl.BlockSpec((None, MOD_ROWS, tn), lambda l, j: (l, 0, j)),
        out_shape=jax.ShapeDtypeStruct((depth, MOD_ROWS, n), F32),
        compiler_params=_cparams("arbitrary", "arbitrary"),
        name="modulation",
    )(cvec_t, w_mod, b_mod.reshape(depth, 1, n))


def _mod_spec(which, mod_row, tm):
    return pl.BlockSpec((None, None, 1, D_MODEL), lambda i, *_: (mod_row(i * tm), which, 0, 0))


def _rms(x, g):
    return x * lax.rsqrt(jnp.mean(x * x, axis=-1, keepdims=True) + EPS) * g


def _norm_mod_kernel(h_ref, g_ref, sh_ref, sc_ref, o_ref):
    y = _rms(h_ref[...], g_ref[...])
    o_ref[...] = (y * (1.0 + sc_ref[...]) + sh_ref[...]).astype(o_ref.dtype)


def _norm_modulate(h, g, mods, mod_row, tm):
    m, d = h.shape
    return pl.pallas_call(
        _norm_mod_kernel,
        grid=(m // tm,),
        in_specs=[
            pl.BlockSpec((tm, d), lambda i: (i, 0)),
            pl.BlockSpec((1, d), lambda i: (0, 0)),
            _mod_spec(0, mod_row, tm),
            _mod_spec(1, mod_row, tm),
        ],
        out_specs=pl.BlockSpec((tm, d), lambda i: (i, 0)),
        out_shape=jax.ShapeDtypeStruct((m, d), BF16),
        compiler_params=_cparams("arbitrary"),
        name="norm_modulate",
    )(h, g.reshape(1, d), mods, mods)


def _proj_kernel(*refs, col0, rope):
    if rope:
        x_ref, w_ref, cos_ref, sin_up_ref, sin_dn_ref, o_ref, wb_ref = refs
    else:
        x_ref, w_ref, o_ref, wb_ref = refs
    j = pl.program_id(0) + col0

    @pl.when(pl.program_id(1) == 0)
    def _():
        wb_ref[...] = w_ref[...].astype(BF16)

    is_q = j == PROJ_Q_TILE
    is_k = j == PROJ_K_TILE
    x = x_ref[...]

    def column_chunks(epilogue):
        for c in range(PROJ_TN // MXU_COLS):
            cols = slice(c * MXU_COLS, (c + 1) * MXU_COLS)
            acc = jnp.dot(x, wb_ref[:, cols], preferred_element_type=F32)
            o_ref[:, cols] = epilogue(acc).astype(o_ref.dtype)

    def rotary(acc):
        qscale = jnp.where(is_q, ATTN_SCALE * math.log2(math.e), 1.0)
        half = HEAD_DIM // 4
        out = []
        for hd in range(MXU_COLS // LANES):
            a = acc[:, hd * LANES:(hd + 1) * LANES]
            if rope:
                a = (a * cos_ref[...] + pltpu.roll(a, LANES - half, 1) * sin_up_ref[...]
                     + pltpu.roll(a, half, 1) * sin_dn_ref[...])
            out.append(a * qscale)
        return jnp.concatenate(out, axis=1)

    pl.when(is_q | is_k)(lambda: column_chunks(rotary))
    pl.when(jnp.logical_not(is_q | is_k))(lambda: column_chunks(lambda acc: acc))


def _in_proj(xm, w_in, layer, rope_tabs, tm, col0, ncols, seq_len):
    m, d = xm.shape
    tn = PROJ_TN
    n_plain = PROJ_GATE_TILE
    n_gate = PROJ_Q_TILE - PROJ_GATE_TILE
    n_qkv = PROJ_TILES - PROJ_Q_TILE

    def w_tile(j):
        j = j + col0
        return jnp.where(j < n_plain, j, jnp.where(j < n_plain + n_gate, j + n_qkv, j - n_gate))

    in_specs = [
        pl.BlockSpec((tm, d), lambda j, i: (i, 0)),
        pl.BlockSpec((None, d, tn), lambda j, i: (layer, 0, w_tile(j))),
    ]
    args = [xm, w_in]
    rope = rope_tabs is not None
    if rope:
        tps = seq_len // tm
        in_specs += [pl.BlockSpec((tm, LANES), lambda j, i: (i % tps, 0))] * 3
        args += list(rope_tabs)
    return pl.pallas_call(
        functools.partial(_proj_kernel, col0=col0, rope=rope),
        grid=(ncols, m // tm),
        in_specs=in_specs,
        out_specs=pl.BlockSpec((tm, tn), lambda j, i: (i, j)),
        out_shape=jax.ShapeDtypeStruct((m, ncols * tn), BF16),
        scratch_shapes=[pltpu.VMEM((d, tn), BF16)],
        compiler_params=_cparams("arbitrary", "arbitrary"),
        name="in_proj",
    )(*args)


def _fourier_kernel(f_ref, cs_ref, c0_ref, s0_ref, rc_ref, rs_ref, o_ref, uc_ref, us_ref, *, chunk, scale):
    i = pl.program_id(1)
    seq = f_ref.shape[0]
    gw = FOURIER_GROUP_W

    @pl.when(i == 0)
    def _():
        for r in range(seq // chunk):
            rows = slice(r * chunk, (r + 1) * chunk)
            for g in range(FOURIER_GROUPS):
                cols = slice(g * gw, (g + 1) * gw)
                y = jnp.dot(f_ref[rows, cols], cs_ref[...], preferred_element_type=F32)
                uc_ref[rows, cols] = y[:, :gw].astype(BF16)
                us_ref[rows, cols] = y[:, gw:].astype(BF16)

    c0, s0 = c0_ref[...].astype(F32), s0_ref[...].astype(F32)
    rc, rs = rc_ref[pl.ds(i, 1), :], rs_ref[pl.ds(i, 1), :]
    cos_t = (c0 * rc - s0 * rs).astype(BF16)
    sin_t = (s0 * rc + c0 * rs).astype(BF16)
    y = jnp.dot(cos_t, uc_ref[...], preferred_element_type=F32)
    y -= jnp.dot(sin_t, us_ref[...], preferred_element_type=F32)
    o_ref[...] = (y * scale).astype(o_ref.dtype)


def _fourier(p3, cs, dft, tr):
    b, seq, _ = p3.shape
    chunk = min(seq, 1024)
    scale = 1.0 / math.sqrt(seq * FOURIER_GROUP_W)
    c0, s0, rc, rs = dft
    full = lambda a: pl.BlockSpec(a.shape, lambda bi, i: (0, 0), pipeline_mode=pl.Buffered(1))
    return pl.pallas_call(
        functools.partial(_fourier_kernel, chunk=chunk, scale=scale),
        grid=(b, seq // tr),
        in_specs=[
            pl.BlockSpec((None, seq, FOURIER_W), lambda bi, i: (bi, 0, 0)),
            pl.BlockSpec((FOURIER_GROUP_W, 2 * FOURIER_GROUP_W), lambda bi, i: (0, 0)),
            full(c0), full(s0), full(rc), full(rs),
        ],
        out_specs=pl.BlockSpec((None, tr, FOURIER_W), lambda bi, i: (bi, i, 0)),
        out_shape=jax.ShapeDtypeStruct((b, seq, FOURIER_W), BF16),
        scratch_shapes=[pltpu.VMEM((seq, FOURIER_W), BF16), pltpu.VMEM((seq, FOURIER_W), BF16)],
        compiler_params=_cparams("arbitrary", "arbitrary"),
        name="fourier",
    )(p3, cs, c0, s0, rc, rs)


def _attn_kernel(*refs, tq, tk, n_lat, lam_init):
    if n_lat:
        lam_ref, q_ref, kc_ref, vc_ref, kl_ref, vl_ref, g_ref, o_ref = refs
    else:
        lam_ref, q_ref, kc_ref, vc_ref, g_ref, o_ref = refs
        kl_ref = vl_ref = None
    n_ctx = kc_ref.shape[0]
    q = q_ref[...]
    lane = lax.broadcasted_iota(jnp.int32, (1, LANES), 1)
    comp0 = lane < HEAD_DIM
    zero = jnp.zeros_like(q)
    qq = jnp.concatenate([jnp.where(comp0, q, zero), jnp.where(comp0, zero, q)], axis=0)
    chunks = [(kc_ref, vc_ref, 0, n_ctx)] + [(kl_ref, vl_ref, c * tk, tk) for c in range(n_lat)]

    def lane_fold(x, op):
        part = x[:, 0:LANES]
        for c in range(1, x.shape[1] // LANES):
            part = op(part, x[:, c * LANES:(c + 1) * LANES])
        return part

    def scores(c):
        kr, _, r0, sz = chunks[c]
        return lax.dot_general(qq, kr[r0:r0 + sz, :], (((1,), (1,)), ((), ())), preferred_element_type=F32)

    m = l = acc = None
    ahead = ATTN_SCORES_AHEAD
    pending = [scores(c) for c in range(min(ahead, len(chunks)))]
    for c, (_, vr, r0, sz) in enumerate(chunks):
        s = pending.pop(0)
        if c + ahead < len(chunks):
            pending.append(scores(c + ahead))
        mc = jnp.max(lane_fold(s, jnp.maximum), axis=-1, keepdims=True)
        m_new = mc if m is None else jnp.maximum(m, mc)
        p = jnp.exp2(s - m_new)
        lc = lane_fold(p, jnp.add)
        y = jnp.dot(p.astype(BF16), vr[r0:r0 + sz, :], preferred_element_type=F32)
        if m is None:
            l, acc = lc, y
        else:
            alpha = jnp.exp2(m - m_new)
            l = alpha * l + lc
            acc = alpha * acc + y
        m = m_new
    o = acc / jnp.sum(l, axis=-1, keepdims=True)
    lf = lam_ref[...]
    lam = (jnp.exp(jnp.sum(lf[0:1] * lf[1:2], axis=-1, keepdims=True))
           - jnp.exp(jnp.sum(lf[2:3] * lf[3:4], axis=-1, keepdims=True)) + lam_init)
    o = o[:tq] - lam * o[tq:]
    o = o * lax.rsqrt(jnp.mean(o * o, axis=-1, keepdims=True) + SUBLN_EPS)
    o_ref[...] = (o * g_ref[...] * (1.0 - lam_init)).astype(o_ref.dtype)


def _attention(q3, qblk, c3, kcblk, vcblk, l3, lam, subln_g, lam_init, tq):
    b, lq, _ = q3.shape
    n_ctx = c3.shape[1]
    tk = ATTN_TK
    in_specs = [
        pl.BlockSpec((4, HEAD_DIM), lambda bi, h, i: (0, 0)),
        pl.BlockSpec((None, tq, LANES), lambda bi, h, i: (bi, i, qblk + h)),
        pl.BlockSpec((None, n_ctx, LANES), lambda bi, h, i: (bi, 0, kcblk + h)),
        pl.BlockSpec((None, n_ctx, LANES), lambda bi, h, i: (bi, 0, vcblk + h)),
    ]
    args = [lam, q3, c3, c3]
    n_lat = 0
    if l3 is not None:
        seq = l3.shape[1]
        n_lat = seq // min(tk, seq)
        tk = seq // n_lat
        in_specs += [
            pl.BlockSpec((None, seq, LANES), lambda bi, h, i: (bi, 0, P_K_OFF // LANES + h)),
            pl.BlockSpec((None, seq, LANES), lambda bi, h, i: (bi, 0, P_V_OFF // LANES + h)),
        ]
        args += [l3, l3]
    in_specs.append(pl.BlockSpec((1, VAL_DIM), lambda bi, h, i: (0, 0)))
    args.append(subln_g.reshape(1, VAL_DIM))
    return pl.pallas_call(
        functools.partial(_attn_kernel, tq=tq, tk=tk, n_lat=n_lat, lam_init=lam_init),
        grid=(b, N_HEADS, lq // tq),
        in_specs=in_specs,
        out_specs=pl.BlockSpec((None, tq, LANES), lambda bi, h, i: (bi, i, h)),
        out_shape=jax.ShapeDtypeStruct((b, lq, ATTN_V_W), BF16),
        compiler_params=_cparams("arbitrary", "arbitrary", "arbitrary"),
        name="diff_attention",
    )(*args)


def _conv3_rows(buf_ref, w, tm, base=BF16_SUBLANES):
    return (buf_ref[base - 1:base - 1 + tm, :] * w[0:1] + buf_ref[base:base + tm, :] * w[1:2]
            + buf_ref[base + 1:base + 1 + tm, :] * w[2:3])


def _halo_specs(tm, width, colblk, m, row_axis=0):
    per = tm // BF16_SUBLANES
    last = m // BF16_SUBLANES - 1
    prev = pl.BlockSpec((BF16_SUBLANES, width),
                        lambda *ids: (jnp.maximum(ids[row_axis] * per - 1, 0), colblk))
    nxt = pl.BlockSpec((BF16_SUBLANES, width),
                       lambda *ids: (jnp.minimum((ids[row_axis] + 1) * per, last), colblk))
    return prev, nxt


def _merge_kernel(f_ref, cb_ref, cc_ref, cx_ref, ccp_ref, cxp_ref, ccn_ref, cxn_ref, a_ref,
                  gf_ref, gc_ref, ga_ref, cw_ref, wbf_ref, wbc_ref, wba_ref, wo_ref,
                  h_ref, gt_ref, g2_ref, sh_ref, sc_ref, ho_ref, xo_ref, z_ref, mg_ref, *, tps, nchunk):
    i = pl.program_id(0)
    tm = cc_ref.shape[0]
    h = BF16_SUBLANES
    first = (i % tps) == 0
    last = (i % tps) == tps - 1
    zp = ccp_ref[...].astype(F32) * cxp_ref[...].astype(F32)
    zn = ccn_ref[...].astype(F32) * cxn_ref[...].astype(F32)
    z_ref[0:h, :] = jnp.where(first, 0.0, zp)
    z_ref[h:h + tm, :] = cc_ref[...].astype(F32) * cx_ref[...].astype(F32)
    z_ref[h + tm:2 * h + tm, :] = jnp.where(last, 0.0, zn)
    cv = (cb_ref[...].astype(F32) * _conv3_rows(z_ref, cw_ref[...], tm)).astype(BF16)
    f = f_ref[...]
    a = a_ref[...]
    cn = D_MODEL // nchunk
    for n in range(nchunk):
        cols = slice(n * cn, (n + 1) * cn)
        gate = lambda g_ref: jax.nn.sigmoid(g_ref[:, cols].astype(F32))
        y = gate(gf_ref) * jnp.dot(f, wbf_ref[:, cols], preferred_element_type=F32)
        y += gate(gc_ref) * jnp.dot(cv, wbc_ref[:, cols], preferred_element_type=F32)
        y += gate(ga_ref) * jnp.dot(a, wba_ref[:, cols], preferred_element_type=F32)
        mg_ref[:, cols] = y.astype(BF16)
    y = jnp.dot(mg_ref[...], wo_ref[...], preferred_element_type=F32)
    h_new = h_ref[...] + gt_ref[...] * y
    ho_ref[...] = h_new
    xo_ref[...] = (_rms(h_new, g2_ref[...]) * (1.0 + sc_ref[...]) + sh_ref[...]).astype(xo_ref.dtype)


def _merge_out(p, fmix, attn, conv_w, w_bf, w_bc, w_ba, w_out, layer, h, g2, mods, mod_row, tm, seq_len):
    m = p.shape[0]
    d = D_MODEL
    cw = CONV_W
    tps = seq_len // tm
    blk = lambda width, c: pl.BlockSpec((tm, width), lambda i: (i, c))
    ccp, ccn = _halo_specs(tm, cw, 2, m)
    cxp, cxn = _halo_specs(tm, cw, 3, m)
    gate0 = P_GATE_OFF // d
    full = lambda r, c: pl.BlockSpec((None, r, c), lambda i: (layer, 0, 0), pipeline_mode=pl.Buffered(1))
    return pl.pallas_call(
        functools.partial(_merge_kernel, tps=tps, nchunk=4),
        grid=(m // tm,),
        in_specs=[
            blk(FOURIER_W, 0),
            blk(cw, 1), blk(cw, 2), blk(cw, 3),
            ccp, cxp, ccn, cxn,
            blk(ATTN_V_W, 0),
            pl.BlockSpec((tm, d), lambda i: (i, gate0)),
            pl.BlockSpec((tm, d), lambda i: (i, gate0 + 1)),
            pl.BlockSpec((tm, d), lambda i: (i, gate0 + 2)),
            full(3, cw), full(FOURIER_W, d), full(cw, d), full(ATTN_V_W, d), full(d, d),
            pl.BlockSpec((tm, d), lambda i: (i, 0)),
            _mod_spec(2, mod_row, tm),
            pl.BlockSpec((1, d), lambda i: (0, 0)),
            _mod_spec(3, mod_row, tm),
            _mod_spec(4, mod_row, tm),
        ],
        out_specs=[pl.BlockSpec((tm, d), lambda i: (i, 0)), pl.BlockSpec((tm, d), lambda i: (i, 0))],
        out_shape=[jax.ShapeDtypeStruct((m, d), F32), jax.ShapeDtypeStruct((m, d), BF16)],
        scratch_shapes=[pltpu.VMEM((tm + 2 * BF16_SUBLANES, cw), F32), pltpu.VMEM((tm, d), BF16)],
        compiler_params=_cparams("arbitrary"),
        name="merge_out",
    )(fmix, p, p, p, p, p, p, p, attn, p, p, p, conv_w, w_bf, w_bc, w_ba, w_out,
      h, mods, g2.reshape(1, d), mods, mods)


def _ffn_up_kernel(*refs, tps, nseq, cast_down):
    if cast_down:
        (x_ref, xp_ref, xn_ref, wa_ref, wv_ref, ca_ref, cv_ref, wd_ref, o_ref, wdb_ref,
         wab_ref, wvb_ref, xh_ref, ua_ref, uv_ref) = refs
    else:
        (x_ref, xp_ref, xn_ref, wa_ref, wv_ref, ca_ref, cv_ref, o_ref,
         wab_ref, wvb_ref, xh_ref, ua_ref, uv_ref) = refs
    i = pl.program_id(1)
    tm = x_ref.shape[0]
    hl = BF16_SUBLANES
    seq = tm // nseq

    @pl.when(i == 0)
    def _():
        wab_ref[...] = wa_ref[...].astype(BF16)
        wvb_ref[...] = wv_ref[...].astype(BF16)
        if cast_down:
            wdb_ref[...] = wd_ref[...].astype(BF16)

    base = [hl + s * (seq + hl) for s in range(nseq)]
    zeros = jnp.zeros((hl, x_ref.shape[1]), BF16)
    if nseq == 1:
        first = (i % tps) == 0
        last = (i % tps) == tps - 1
        xh_ref[0:hl, :] = jnp.where(first, zeros, xp_ref[...])
        xh_ref[hl + tm:2 * hl + tm, :] = jnp.where(last, zeros, xn_ref[...])
    else:
        for s in range(nseq + 1):
            xh_ref[s * (seq + hl):s * (seq + hl) + hl, :] = zeros
    for s in range(nseq):
        xh_ref[base[s]:base[s] + seq, :] = x_ref[s * seq:(s + 1) * seq, :]
    xh = xh_ref[...]
    ua_ref[...] = jnp.dot(xh, wab_ref[...], preferred_element_type=F32)
    uv_ref[...] = jnp.dot(xh, wvb_ref[...], preferred_element_type=F32)
    for s in range(nseq):
        a = _conv3_rows(ua_ref, ca_ref[...], seq, base[s])
        v = _conv3_rows(uv_ref, cv_ref[...], seq, base[s])
        o_ref[s * seq:(s + 1) * seq, :] = (a * jax.nn.sigmoid(a) * v).astype(o_ref.dtype)


def _ffn_down_kernel(a_ref, w_ref, h_ref, gt_ref, g_ref, sh_ref, sc_ref, ho_ref, xo_ref):
    y = jnp.dot(a_ref[...], w_ref[...], preferred_element_type=F32)
    h_new = h_ref[...] + gt_ref[...] * y
    ho_ref[...] = h_new
    xo_ref[...] = (_rms(h_new, g_ref[...]) * (1.0 + sc_ref[...]) + sh_ref[...]).astype(xo_ref.dtype)


def _ffn_down_final_kernel(a_ref, w_ref, h_ref, gt_ref, g_ref, o_ref):
    y = jnp.dot(a_ref[...], w_ref[...], preferred_element_type=F32)
    o_ref[...] = _rms(h_ref[...] + gt_ref[...] * y, g_ref[...])


def _ffn(xm, w_up, conv_w, w_down, w_down_b, layer, h, mods, mod_row, seq_len, nxt):
    m, d = xm.shape
    tf = FFN_TF
    nj = D_FF // tf
    tm = FFN_UP_TM if seq_len >= FFN_UP_TM else min(m, FFN_UP_TM)
    tps = max(seq_len // tm, 1)
    nseq = max(tm // seq_len, 1)
    stage_rows = nseq * (tm // nseq + BF16_SUBLANES) + BF16_SUBLANES
    xp, xn = _halo_specs(tm, d, 0, m, row_axis=1)
    cast_down = w_down_b is None
    in_specs = [
        pl.BlockSpec((tm, d), lambda j, i: (i, 0)),
        xp, xn,
        pl.BlockSpec((None, d, tf), lambda j, i: (layer, 0, j)),
        pl.BlockSpec((None, d, tf), lambda j, i: (layer, 0, nj + j)),
        pl.BlockSpec((None, 3, tf), lambda j, i: (layer, 0, j)),
        pl.BlockSpec((None, 3, tf), lambda j, i: (layer, 0, nj + j)),
    ]
    args = [xm, xm, xm, w_up, w_up, conv_w, conv_w]
    out_specs = [pl.BlockSpec((tm, tf), lambda j, i: (i, j))]
    out_shape = [jax.ShapeDtypeStruct((m, D_FF), BF16)]
    if cast_down:
        in_specs.append(pl.BlockSpec((None, tf, d), lambda j, i: (layer, j, 0)))
        args.append(w_down)
        out_specs.append(pl.BlockSpec((tf, d), lambda j, i: (j, 0)))
        out_shape.append(jax.ShapeDtypeStruct((D_FF, d), BF16))
    outs = pl.pallas_call(
        functools.partial(_ffn_up_kernel, tps=tps, nseq=nseq, cast_down=cast_down),
        grid=(nj, m // tm),
        in_specs=in_specs,
        out_specs=out_specs,
        out_shape=out_shape,
        scratch_shapes=[
            pltpu.VMEM((d, tf), BF16),
            pltpu.VMEM((d, tf), BF16),
            pltpu.VMEM((stage_rows, d), BF16),
            pltpu.VMEM((stage_rows, tf), F32),
            pltpu.VMEM((stage_rows, tf), F32),
        ],
        compiler_params=_cparams("arbitrary", "arbitrary"),
        name="ffn_up",
    )(*args)
    act = outs[0]
    if cast_down:
        w_down_b = outs[1]

    tmd = min(m, FFN_DOWN_TM)
    row = lambda i: (i, 0)
    in_specs = [
        pl.BlockSpec((tmd, D_FF), row),
        pl.BlockSpec((D_FF, d), lambda i: (0, 0), pipeline_mode=pl.Buffered(1)),
        pl.BlockSpec((tmd, d), row),
        _mod_spec(5, mod_row, tmd),
        pl.BlockSpec((1, d), lambda i: (0, 0)),
    ]
    g_next, mods_next = nxt
    if mods_next is None:
        return pl.pallas_call(
            _ffn_down_final_kernel,
            grid=(m // tmd,),
            in_specs=in_specs,
            out_specs=pl.BlockSpec((tmd, d), row),
            out_shape=jax.ShapeDtypeStruct((m, d), F32),
            compiler_params=_cparams("arbitrary"),
            name="ffn_down_final",
        )(act, w_down_b, h, mods, g_next.reshape(1, d)), None, w_down_b
    h_new, x_next = pl.pallas_call(
        _ffn_down_kernel,
        grid=(m // tmd,),
        in_specs=in_specs + [_mod_spec(0, mod_row, tmd), _mod_spec(1, mod_row, tmd)],
        out_specs=[pl.BlockSpec((tmd, d), row), pl.BlockSpec((tmd, d), row)],
        out_shape=[jax.ShapeDtypeStruct((m, d), F32), jax.ShapeDtypeStruct((m, d), BF16)],
        compiler_params=_cparams("arbitrary"),
        name="ffn_down",
    )(act, w_down_b, h, mods, g_next.reshape(1, d), mods_next, mods_next)
    return h_new, x_next, w_down_b


def _rope_tables(length):
    n_freq = HEAD_DIM // 4
    pos = jnp.arange(length)
    row = (pos // GRID_W).astype(F32)
    col = (pos % GRID_W).astype(F32)
    inv = ROPE_THETA ** (-(2.0 * jnp.arange(n_freq, dtype=F32)) / (HEAD_DIM // 2))
    ang_r, ang_c = row[:, None] * inv, col[:, None] * inv
    ang = jnp.concatenate([ang_r, ang_r, ang_c, ang_c], axis=1)
    cos = jnp.tile(jnp.cos(ang), (1, 2))
    sin = jnp.tile(jnp.sin(ang), (1, 2))
    is_x1 = (jnp.arange(LANES) % (2 * n_freq)) < n_freq
    return cos, jnp.where(is_x1, -sin, 0.0), jnp.where(is_x1, 0.0, sin)


def _dft_tables(n, tr):
    m = jnp.arange(n, dtype=jnp.int32)

    def cos_sin(rows):
        ang = ((rows[:, None] * m[None, :]) % n).astype(F32) * (2.0 * math.pi / n)
        return jnp.cos(ang), jnp.sin(ang)

    rc, rs = cos_sin(jnp.arange(n // tr, dtype=jnp.int32) * tr)
    g = 1 << ((tr.bit_length() - 1 + 1) // 2)
    assert tr % g == 0
    ca, sa = cos_sin(jnp.arange(tr // g, dtype=jnp.int32) * g)
    cb, sb = cos_sin(jnp.arange(g, dtype=jnp.int32))
    c0 = (ca[:, None, :] * cb[None, :, :] - sa[:, None, :] * sb[None, :, :]).reshape(tr, n)
    s0 = (sa[:, None, :] * cb[None, :, :] + ca[:, None, :] * sb[None, :, :]).reshape(tr, n)
    return c0.astype(BF16), s0.astype(BF16), rc, rs


def _channel_dft():
    c = np.arange(FOURIER_GROUP_W)
    ang = 2.0 * np.pi * ((c[:, None] * c[None, :]) % FOURIER_GROUP_W) / FOURIER_GROUP_W
    return jnp.asarray(np.concatenate([np.cos(ang), np.sin(ang)], axis=1), dtype=BF16)


def _project(xm, seq_len, rope_tabs, w, proj_cols=(0, PROJ_TILES)):
    return _in_proj(xm, w["w_in"], w["layer"], rope_tabs, min(xm.shape[0], PROJ_TM), proj_cols[0],
                    proj_cols[1], seq_len)


def _mixer(h, p, seq_len, dft, ctx_kv, w, mods, mod_row, lam_init, tm, tq, nxt):
    m = h.shape[0]
    b = m // seq_len
    p3 = p.reshape(b, seq_len, N_IN)
    fmix = _fourier(p3, w["cs"], dft, min(seq_len, FOURIER_TR)).reshape(m, FOURIER_W)
    if ctx_kv is None:
        attn = _attention(p3, P_Q_OFF // LANES, p3, P_K_OFF // LANES, P_V_OFF // LANES, None,
                          w["lambdas"], w["subln_g"], lam_init, tq)
    else:
        c3, kcblk, vcblk = ctx_kv
        attn = _attention(p3, P_Q_OFF // LANES, c3, kcblk, vcblk, p3,
                          w["lambdas"], w["subln_g"], lam_init, tq)
    h_mid, xm2 = _merge_out(p, fmix, attn.reshape(m, ATTN_V_W), w["conv_mix_w"], w["w_br_fourier"],
                            w["w_br_conv"], w["w_br_attn"], w["w_out"], w["layer"], h, w["g_norm2"],
                            mods, mod_row, min(tm, MERGE_TM), seq_len)
    h_new, x_next, w["w_ffn_down_b"] = _ffn(xm2, w["w_ffn_up"], w["ffn_conv_w"], w["w_ffn_down"],
                                            w["w_ffn_down_b"], w["layer"], h_mid, mods, mod_row, seq_len, nxt)
    return h_new, x_next


def kernel(x, c, ctx, c_ctx, w_mod, b_mod, g_norm1, g_norm2, w_in, conv_mix_w, lambdas, subln_g,
           w_br_fourier, w_br_conv, w_br_attn, w_out, w_ffn_up, ffn_conv_w, w_ffn_down, g_final):
    b, seq, d = x.shape
    n_ctx = ctx.shape[1]
    depth = w_mod.shape[0]
    assert d == D_MODEL and b + 1 <= MOD_ROWS

    wb = lambda a: a.astype(BF16)
    w_bf, w_bc, w_ba, w_o = wb(w_br_fourier), wb(w_br_conv), wb(w_br_attn), wb(w_out)

    rope_tabs = _rope_tables(seq)
    dft_l = _dft_tables(seq, min(seq, FOURIER_TR))
    dft_c = _dft_tables(n_ctx, min(n_ctx, FOURIER_TR))
    cs = _channel_dft()

    cvec = jnp.zeros((MOD_ROWS, d), F32).at[:b].set(c).at[b].set(c_ctx)
    mods_all = _modulation(cvec.T, b + 1, w_mod, b_mod).reshape(depth, MOD_ROWS, 6, 1, d)

    tm_l = min(512, seq)
    tm_c = n_ctx
    tq = min(512, seq)
    lat_row = lambda r: r // seq
    ctx_row = lambda r: b

    h = x.reshape(b * seq, d)
    hc = ctx.reshape(b * n_ctx, d)
    xc = _norm_modulate(hc, g_norm1[0], mods_all[0], ctx_row, tm_c)
    xl = _norm_modulate(h, g_norm1[0], mods_all[0], lat_row, tm_l)
    for i in range(depth):
        last = i == depth - 1
        lam_init = 0.8 - 0.6 * math.exp(-0.3 * i)
        mods = mods_all[i]
        w = dict(w_in=w_in, layer=i, cs=cs, lambdas=lambdas[i], subln_g=subln_g[i], conv_mix_w=conv_mix_w,
                 w_br_fourier=w_bf, w_br_conv=w_bc, w_br_attn=w_ba, w_out=w_o,
                 g_norm2=g_norm2[i], w_ffn_up=w_ffn_up, ffn_conv_w=ffn_conv_w, w_ffn_down=w_ffn_down,
                 w_ffn_down_b=None)
        nxt = (g_final, None) if last else (g_norm1[i + 1], mods_all[i + 1])

        if last:
            pc = _project(xc, n_ctx, None, w, proj_cols=(PROJ_K_TILE, 2))
            ctx_kv = (pc.reshape(b, n_ctx, 2 * PROJ_TN), 0, PROJ_TN // LANES)
        else:
            pc = _project(xc, n_ctx, None, w)
            ctx_kv = (pc.reshape(b, n_ctx, N_IN), P_K_OFF // LANES, P_V_OFF // LANES)
        h, xl = _mixer(h, _project(xl, seq, rope_tabs, w), seq, dft_l, ctx_kv, w, mods, lat_row, lam_init,
                       tm_l, tq, nxt)
        if not last:
            hc, xc = _mixer(hc, pc, n_ctx, dft_c, None, w, mods, ctx_row, lam_init, tm_c, n_ctx, nxt)
    return h.reshape(b, seq, d)
```

```python
import functools
import math

import numpy as np
import jax
import jax.numpy as jnp
from jax import lax
from jax.experimental import pallas as pl
from jax.experimental.pallas import tpu as pltpu

F32 = jnp.float32
BF16 = jnp.bfloat16

D_MODEL = 2048
GRID_W = 64
FOURIER_GROUPS = 4
FOURIER_GROUP_W = D_MODEL // 16
FOURIER_W = FOURIER_GROUPS * FOURIER_GROUP_W
CONV_W = D_MODEL // 4
N_HEADS = 8
HEAD_DIM = D_MODEL // (4 * N_HEADS)
VAL_DIM = 2 * HEAD_DIM
ATTN_QK_W = N_HEADS * 2 * HEAD_DIM
ATTN_V_W = N_HEADS * VAL_DIM
ROPE_THETA = 10000.0
ATTN_SCALE = HEAD_DIM ** -0.5
SUBLN_EPS = 1e-5
Q_OFF = FOURIER_W + 3 * CONV_W
K_OFF = Q_OFF + ATTN_QK_W
V_OFF = K_OFF + ATTN_QK_W
V_END = V_OFF + ATTN_V_W
N_IN = V_END + 3 * D_MODEL
D_FF = ((8 * D_MODEL // 3 + 255) // 256) * 256
EPS = 1e-6

LANES = 128
MXU_COLS = 256
BF16_SUBLANES = 16
VMEM_LIMIT = 56 * 1024 * 1024
MOD_ROWS = 8

PROJ_TN = 1024
PROJ_TM = 1024
P_GATE_OFF = Q_OFF
P_Q_OFF = P_GATE_OFF + 3 * D_MODEL
P_K_OFF = P_Q_OFF + ATTN_QK_W
P_V_OFF = P_K_OFF + ATTN_QK_W
PROJ_GATE_TILE = P_GATE_OFF // PROJ_TN
PROJ_Q_TILE = P_Q_OFF // PROJ_TN
PROJ_K_TILE = P_K_OFF // PROJ_TN
PROJ_TILES = N_IN // PROJ_TN
ATTN_TK = 512
ATTN_SCORES_AHEAD = 1
FOURIER_TR = 512
MERGE_TM = 256
FFN_TF = 512
FFN_UP_TM = 1024
FFN_DOWN_TM = 256


def _cparams(*sem):
    return pltpu.CompilerParams(dimension_semantics=sem, vmem_limit_bytes=VMEM_LIMIT)


def _mod_kernel(ct_ref, w_ref, b_ref, o_ref, *, n_rows):
    ct = ct_ref[...]
    st = ct * jax.nn.sigmoid(ct)
    w = w_ref[...]
    d, tn = w.shape
    rows = []
    for r in range(n_rows):
        t = (st[:, r:r + 1] * w).reshape(d // MOD_ROWS, MOD_ROWS, tn)
        rows.append(jnp.sum(jnp.sum(t, axis=0), axis=0, keepdims=True))
    rows.append(jnp.zeros((MOD_ROWS - n_rows, tn), F32))
    o_ref[...] = jnp.concatenate(rows, axis=0) + b_ref[...]


def _modulation(cvec_t, n_rows, w_mod, b_mod):
    depth, d, n = w_mod.shape
    tn = 1024
    return pl.pallas_call(
        functools.partial(_mod_kernel, n_rows=n_rows),
        grid=(depth, n // tn),
        in_specs=[
            pl.BlockSpec((d, MOD_ROWS), lambda l, j: (0, 0)),
            pl.BlockSpec((None, d, tn), lambda l, j: (l, 0, j)),
            pl.BlockSpec((None, 1, tn), lambda l, j: (l, 0, j)),
        ],
        out_specs=pl.BlockSpec((None, MOD_ROWS, tn), lambda l, j: (l, 0, j)),
        out_shape=jax.ShapeDtypeStruct((depth, MOD_ROWS, n), F32),
        compiler_params=_cparams("arbitrary", "arbitrary"),
        name="modulation",
    )(cvec_t, w_mod, b_mod.reshape(depth, 1, n))


def _mod_spec(which, mod_row, tm):
    return pl.BlockSpec((None, None, 1, D_MODEL), lambda i, *_: (mod_row(i * tm), which, 0, 0))


def _rms(x, g):
    return x * lax.rsqrt(jnp.mean(x * x, axis=-1, keepdims=True) + EPS) * g


def _norm_mod_kernel(h_ref, g_ref, sh_ref, sc_ref, o_ref):
    y = _rms(h_ref[...], g_ref[...])
    o_ref[...] = (y * (1.0 + sc_ref[...]) + sh_ref[...]).astype(o_ref.dtype)


def _norm_modulate(h, g, mods, mod_row, tm):
    m, d = h.shape
    return pl.pallas_call(
        _norm_mod_kernel,
        grid=(m // tm,),
        in_specs=[
            pl.BlockSpec((tm, d), lambda i: (i, 0)),
            pl.BlockSpec((1, d), lambda i: (0, 0)),
            _mod_spec(0, mod_row, tm),
            _mod_spec(1, mod_row, tm),
        ],
        out_specs=pl.BlockSpec((tm, d), lambda i: (i, 0)),
        out_shape=jax.ShapeDtypeStruct((m, d), BF16),
        compiler_params=_cparams("arbitrary"),
        name="norm_modulate",
    )(h, g.reshape(1, d), mods, mods)


def _proj_kernel(*refs, col0, rope):
    if rope:
        x_ref, w_ref, cos_ref, sin_up_ref, sin_dn_ref, o_ref, wb_ref = refs
    else:
        x_ref, w_ref, o_ref, wb_ref = refs
    j = pl.program_id(0) + col0

    @pl.when(pl.program_id(1) == 0)
    def _():
        wb_ref[...] = w_ref[...].astype(BF16)

    is_q = j == PROJ_Q_TILE
    is_k = j == PROJ_K_TILE
    x = x_ref[...]

    def column_chunks(epilogue):
        for c in range(PROJ_TN // MXU_COLS):
            cols = slice(c * MXU_COLS, (c + 1) * MXU_COLS)
            acc = jnp.dot(x, wb_ref[:, cols], preferred_element_type=F32)
            o_ref[:, cols] = epilogue(acc).astype(o_ref.dtype)

    def rotary(acc):
        qscale = jnp.where(is_q, ATTN_SCALE * math.log2(math.e), 1.0)
        half = HEAD_DIM // 4
        out = []
        for hd in range(MXU_COLS // LANES):
            a = acc[:, hd * LANES:(hd + 1) * LANES]
            if rope:
                a = (a * cos_ref[...] + pltpu.roll(a, LANES - half, 1) * sin_up_ref[...]
                     + pltpu.roll(a, half, 1) * sin_dn_ref[...])
            out.append(a * qscale)
        return jnp.concatenate(out, axis=1)

    pl.when(is_q | is_k)(lambda: column_chunks(rotary))
    pl.when(jnp.logical_not(is_q | is_k))(lambda: column_chunks(lambda acc: acc))


def _in_proj(xm, w_in, layer, rope_tabs, tm, col0, ncols, seq_len):
    m, d = xm.shape
    tn = PROJ_TN
    n_plain = PROJ_GATE_TILE
    n_gate = PROJ_Q_TILE - PROJ_GATE_TILE
    n_qkv = PROJ_TILES - PROJ_Q_TILE

    def w_tile(j):
        j = j + col0
        return jnp.where(j < n_plain, j, jnp.where(j < n_plain + n_gate, j + n_qkv, j - n_gate))

    n_i = m // tm
    row_tile = _back_and_forth(n_i)

    in_specs = [
        pl.BlockSpec((tm, d), lambda j, i: (row_tile(j, i), 0)),
        pl.BlockSpec((None, d, tn), lambda j, i: (layer, 0, w_tile(j))),
    ]
    args = [xm, w_in]
    rope = rope_tabs is not None
    if rope:
        tps = seq_len // tm
        in_specs += [pl.BlockSpec((tm, LANES), lambda j, i: (row_tile(j, i) % tps, 0))] * 3
        args += list(rope_tabs)
    return pl.pallas_call(
        functools.partial(_proj_kernel, col0=col0, rope=rope),
        grid=(ncols, n_i),
        in_specs=in_specs,
        out_specs=pl.BlockSpec((tm, tn), lambda j, i: (row_tile(j, i), j)),
        out_shape=jax.ShapeDtypeStruct((m, ncols * tn), BF16),
        scratch_shapes=[pltpu.VMEM((d, tn), BF16)],
        compiler_params=_cparams("arbitrary", "arbitrary"),
        name="in_proj",
    )(*args)


def _fourier_kernel(f_ref, cs_ref, c0_ref, s0_ref, rc_ref, rs_ref, o_ref, uc_ref, us_ref, *, chunk, scale):
    i = pl.program_id(1)
    seq = f_ref.shape[0]
    gw = FOURIER_GROUP_W

    @pl.when(i == 0)
    def _():
        for r in range(seq // chunk):
            rows = slice(r * chunk, (r + 1) * chunk)
            for g in range(FOURIER_GROUPS):
                cols = slice(g * gw, (g + 1) * gw)
                y = jnp.dot(f_ref[rows, cols], cs_ref[...], preferred_element_type=F32)
                uc_ref[rows, cols] = y[:, :gw].astype(BF16)
                us_ref[rows, cols] = y[:, gw:].astype(BF16)

    c0, s0 = c0_ref[...].astype(F32), s0_ref[...].astype(F32)
    rc, rs = rc_ref[pl.ds(i, 1), :], rs_ref[pl.ds(i, 1), :]
    cos_t = (c0 * rc - s0 * rs).astype(BF16)
    sin_t = (s0 * rc + c0 * rs).astype(BF16)
    y = jnp.dot(cos_t, uc_ref[...], preferred_element_type=F32)
    y -= jnp.dot(sin_t, us_ref[...], preferred_element_type=F32)
    o_ref[...] = (y * scale).astype(o_ref.dtype)


def _fourier(p3, cs, dft, tr):
    b, seq, _ = p3.shape
    chunk = min(seq, 1024)
    scale = 1.0 / math.sqrt(seq * FOURIER_GROUP_W)
    c0, s0, rc, rs = dft
    full = lambda a: pl.BlockSpec(a.shape, lambda bi, i: (0, 0), pipeline_mode=pl.Buffered(1))
    return pl.pallas_call(
        functools.partial(_fourier_kernel, chunk=chunk, scale=scale),
        grid=(b, seq // tr),
        in_specs=[
            pl.BlockSpec((None, seq, FOURIER_W), lambda bi, i: (bi, 0, 0)),
            pl.BlockSpec((FOURIER_GROUP_W, 2 * FOURIER_GROUP_W), lambda bi, i: (0, 0)),
            full(c0), full(s0), full(rc), full(rs),
        ],
        out_specs=pl.BlockSpec((None, tr, FOURIER_W), lambda bi, i: (bi, i, 0)),
        out_shape=jax.ShapeDtypeStruct((b, seq, FOURIER_W), BF16),
        scratch_shapes=[pltpu.VMEM((seq, FOURIER_W), BF16), pltpu.VMEM((seq, FOURIER_W), BF16)],
        compiler_params=_cparams("arbitrary", "arbitrary"),
        name="fourier",
    )(p3, cs, c0, s0, rc, rs)


def _attn_kernel(*refs, tq, tk, n_lat, lam_init):
    if n_lat:
        lam_ref, q_ref, kc_ref, vc_ref, kl_ref, vl_ref, g_ref, o_ref = refs
    else:
        lam_ref, q_ref, kc_ref, vc_ref, g_ref, o_ref = refs
        kl_ref = vl_ref = None
    n_ctx = kc_ref.shape[0]
    q = q_ref[...]
    lane = lax.broadcasted_iota(jnp.int32, (1, LANES), 1)
    comp0 = lane < HEAD_DIM
    zero = jnp.zeros_like(q)
    qq = jnp.concatenate([jnp.where(comp0, q, zero), jnp.where(comp0, zero, q)], axis=0)
    chunks = [(kc_ref, vc_ref, 0, n_ctx)] + [(kl_ref, vl_ref, c * tk, tk) for c in range(n_lat)]

    def lane_fold(x, op):
        part = x[:, 0:LANES]
        for c in range(1, x.shape[1] // LANES):
            part = op(part, x[:, c * LANES:(c + 1) * LANES])
        return part

    def scores(c):
        kr, _, r0, sz = chunks[c]
        return lax.dot_general(qq, kr[r0:r0 + sz, :], (((1,), (1,)), ((), ())), preferred_element_type=F32)

    m = l = acc = None
    ahead = ATTN_SCORES_AHEAD
    pending = [scores(c) for c in range(min(ahead, len(chunks)))]
    for c, (_, vr, r0, sz) in enumerate(chunks):
        s = pending.pop(0)
        if c + ahead < len(chunks):
            pending.append(scores(c + ahead))
        mc = jnp.max(lane_fold(s, jnp.maximum), axis=-1, keepdims=True)
        m_new = mc if m is None else jnp.maximum(m, mc)
        p = jnp.exp2(s - m_new)
        lc = lane_fold(p, jnp.add)
        y = jnp.dot(p.astype(BF16), vr[r0:r0 + sz, :], preferred_element_type=F32)
        if m is None:
            l, acc = lc, y
        else:
            alpha = jnp.exp2(m - m_new)
            l = alpha * l + lc
            acc = alpha * acc + y
        m = m_new
    o = acc / jnp.sum(l, axis=-1, keepdims=True)
    lf = lam_ref[...]
    lam = (jnp.exp(jnp.sum(lf[0:1] * lf[1:2], axis=-1, keepdims=True))
           - jnp.exp(jnp.sum(lf[2:3] * lf[3:4], axis=-1, keepdims=True)) + lam_init)
    o = o[:tq] - lam * o[tq:]
    o = o * lax.rsqrt(jnp.mean(o * o, axis=-1, keepdims=True) + SUBLN_EPS)
    o_ref[...] = (o * g_ref[...] * (1.0 - lam_init)).astype(o_ref.dtype)


def _attention(q3, qblk, c3, kcblk, vcblk, l3, lam, subln_g, lam_init, tq):
    b, lq, _ = q3.shape
    n_ctx = c3.shape[1]
    tk = ATTN_TK
    in_specs = [
        pl.BlockSpec((4, HEAD_DIM), lambda bi, h, i: (0, 0)),
        pl.BlockSpec((None, tq, LANES), lambda bi, h, i: (bi, i, qblk + h)),
        pl.BlockSpec((None, n_ctx, LANES), lambda bi, h, i: (bi, 0, kcblk + h)),
        pl.BlockSpec((None, n_ctx, LANES), lambda bi, h, i: (bi, 0, vcblk + h)),
    ]
    args = [lam, q3, c3, c3]
    n_lat = 0
    if l3 is not None:
        seq = l3.shape[1]
        n_lat = seq // min(tk, seq)
        tk = seq // n_lat
        in_specs += [
            pl.BlockSpec((None, seq, LANES), lambda bi, h, i: (bi, 0, P_K_OFF // LANES + h)),
            pl.BlockSpec((None, seq, LANES), lambda bi, h, i: (bi, 0, P_V_OFF // LANES + h)),
        ]
        args += [l3, l3]
    in_specs.append(pl.BlockSpec((1, VAL_DIM), lambda bi, h, i: (0, 0)))
    args.append(subln_g.reshape(1, VAL_DIM))
    return pl.pallas_call(
        functools.partial(_attn_kernel, tq=tq, tk=tk, n_lat=n_lat, lam_init=lam_init),
        grid=(b, N_HEADS, lq // tq),
        in_specs=in_specs,
        out_specs=pl.BlockSpec((None, tq, LANES), lambda bi, h, i: (bi, i, h)),
        out_shape=jax.ShapeDtypeStruct((b, lq, ATTN_V_W), BF16),
        compiler_params=_cparams("arbitrary", "arbitrary", "arbitrary"),
        name="diff_attention",
    )(*args)


def _conv3_rows(buf_ref, w, tm, base=BF16_SUBLANES):
    return (buf_ref[base - 1:base - 1 + tm, :] * w[0:1] + buf_ref[base:base + tm, :] * w[1:2]
            + buf_ref[base + 1:base + 1 + tm, :] * w[2:3])


def _halo_specs(tm, width, colblk, m, row_tile=lambda i: i):
    per = tm // BF16_SUBLANES
    last = m // BF16_SUBLANES - 1
    prev = pl.BlockSpec((BF16_SUBLANES, width),
                        lambda *ids: (jnp.maximum(row_tile(*ids) * per - 1, 0), colblk))
    nxt = pl.BlockSpec((BF16_SUBLANES, width),
                       lambda *ids: (jnp.minimum((row_tile(*ids) + 1) * per, last), colblk))
    return prev, nxt


def _back_and_forth(n_i):
    return lambda j, i: jnp.where(j % 2 == 0, i, n_i - 1 - i)


def _merge_kernel(f_ref, cb_ref, cc_ref, cx_ref, ccp_ref, cxp_ref, ccn_ref, cxn_ref, a_ref,
                  gf_ref, gc_ref, ga_ref, cw_ref, wbf_ref, wbc_ref, wba_ref, wo_ref,
                  h_ref, gt_ref, g2_ref, sh_ref, sc_ref, ho_ref, xo_ref, z_ref, mg_ref, *, tps, nchunk):
    i = pl.program_id(0)
    tm = cc_ref.shape[0]
    h = BF16_SUBLANES
    first = (i % tps) == 0
    last = (i % tps) == tps - 1
    zp = ccp_ref[...].astype(F32) * cxp_ref[...].astype(F32)
    zn = ccn_ref[...].astype(F32) * cxn_ref[...].astype(F32)
    z_ref[0:h, :] = jnp.where(first, 0.0, zp)
    z_ref[h:h + tm, :] = cc_ref[...].astype(F32) * cx_ref[...].astype(F32)
    z_ref[h + tm:2 * h + tm, :] = jnp.where(last, 0.0, zn)
    cv = (cb_ref[...].astype(F32) * _conv3_rows(z_ref, cw_ref[...], tm)).astype(BF16)
    f = f_ref[...]
    a = a_ref[...]
    cn = D_MODEL // nchunk
    for n in range(nchunk):
        cols = slice(n * cn, (n + 1) * cn)
        gate = lambda g_ref: jax.nn.sigmoid(g_ref[:, cols].astype(F32))
        y = gate(gf_ref) * jnp.dot(f, wbf_ref[:, cols], preferred_element_type=F32)
        y += gate(gc_ref) * jnp.dot(cv, wbc_ref[:, cols], preferred_element_type=F32)
        y += gate(ga_ref) * jnp.dot(a, wba_ref[:, cols], preferred_element_type=F32)
        mg_ref[:, cols] = y.astype(BF16)
    y = jnp.dot(mg_ref[...], wo_ref[...], preferred_element_type=F32)
    h_new = h_ref[...] + gt_ref[...] * y
    ho_ref[...] = h_new
    xo_ref[...] = (_rms(h_new, g2_ref[...]) * (1.0 + sc_ref[...]) + sh_ref[...]).astype(xo_ref.dtype)


def _merge_out(p, fmix, attn, conv_w, w_bf, w_bc, w_ba, w_out, layer, h, g2, mods, mod_row, tm, seq_len):
    m = p.shape[0]
    d = D_MODEL
    cw = CONV_W
    tps = seq_len // tm
    blk = lambda width, c: pl.BlockSpec((tm, width), lambda i: (i, c))
    ccp, ccn = _halo_specs(tm, cw, 2, m)
    cxp, cxn = _halo_specs(tm, cw, 3, m)
    gate0 = P_GATE_OFF // d
    full = lambda r, c: pl.BlockSpec((None, r, c), lambda i: (layer, 0, 0), pipeline_mode=pl.Buffered(1))
    return pl.pallas_call(
        functools.partial(_merge_kernel, tps=tps, nchunk=4),
        grid=(m // tm,),
        in_specs=[
            blk(FOURIER_W, 0),
            blk(cw, 1), blk(cw, 2), blk(cw, 3),
            ccp, cxp, ccn, cxn,
            blk(ATTN_V_W, 0),
            pl.BlockSpec((tm, d), lambda i: (i, gate0)),
            pl.BlockSpec((tm, d), lambda i: (i, gate0 + 1)),
            pl.BlockSpec((tm, d), lambda i: (i, gate0 + 2)),
            full(3, cw), full(FOURIER_W, d), full(cw, d), full(ATTN_V_W, d), full(d, d),
            pl.BlockSpec((tm, d), lambda i: (i, 0)),
            _mod_spec(2, mod_row, tm),
            pl.BlockSpec((1, d), lambda i: (0, 0)),
            _mod_spec(3, mod_row, tm),
            _mod_spec(4, mod_row, tm),
        ],
        out_specs=[pl.BlockSpec((tm, d), lambda i: (i, 0)), pl.BlockSpec((tm, d), lambda i: (i, 0))],
        out_shape=[jax.ShapeDtypeStruct((m, d), F32), jax.ShapeDtypeStruct((m, d), BF16)],
        scratch_shapes=[pltpu.VMEM((tm + 2 * BF16_SUBLANES, cw), F32), pltpu.VMEM((tm, d), BF16)],
        compiler_params=_cparams("arbitrary"),
        name="merge_out",
    )(fmix, p, p, p, p, p, p, p, attn, p, p, p, conv_w, w_bf, w_bc, w_ba, w_out,
      h, mods, g2.reshape(1, d), mods, mods)


def _ffn_up_kernel(*refs, tps, nseq, cast_down, row_tile):
    if cast_down:
        (x_ref, xp_ref, xn_ref, wa_ref, wv_ref, ca_ref, cv_ref, wd_ref, o_ref, wdb_ref,
         wab_ref, wvb_ref, xh_ref, ua_ref, uv_ref) = refs
    else:
        (x_ref, xp_ref, xn_ref, wa_ref, wv_ref, ca_ref, cv_ref, o_ref,
         wab_ref, wvb_ref, xh_ref, ua_ref, uv_ref) = refs
    i = pl.program_id(1)
    tm = x_ref.shape[0]
    hl = BF16_SUBLANES
    seq = tm // nseq

    @pl.when(i == 0)
    def _():
        wab_ref[...] = wa_ref[...].astype(BF16)
        wvb_ref[...] = wv_ref[...].astype(BF16)
        if cast_down:
            wdb_ref[...] = wd_ref[...].astype(BF16)

    base = [hl + s * (seq + hl) for s in range(nseq)]
    zeros = jnp.zeros((hl, x_ref.shape[1]), BF16)
    if nseq == 1:
        it = row_tile(pl.program_id(0), i)
        first = (it % tps) == 0
        last = (it % tps) == tps - 1
        xh_ref[0:hl, :] = jnp.where(first, zeros, xp_ref[...])
        xh_ref[hl + tm:2 * hl + tm, :] = jnp.where(last, zeros, xn_ref[...])
    else:
        for s in range(nseq + 1):
            xh_ref[s * (seq + hl):s * (seq + hl) + hl, :] = zeros
    for s in range(nseq):
        xh_ref[base[s]:base[s] + seq, :] = x_ref[s * seq:(s + 1) * seq, :]
    xh = xh_ref[...]
    ua_ref[...] = jnp.dot(xh, wab_ref[...], preferred_element_type=F32)
    uv_ref[...] = jnp.dot(xh, wvb_ref[...], preferred_element_type=F32)
    for s in range(nseq):
        a = _conv3_rows(ua_ref, ca_ref[...], seq, base[s])
        v = _conv3_rows(uv_ref, cv_ref[...], seq, base[s])
        o_ref[s * seq:(s + 1) * seq, :] = (a * jax.nn.sigmoid(a) * v).astype(o_ref.dtype)


def _ffn_down_kernel(a_ref, w_ref, h_ref, gt_ref, g_ref, sh_ref, sc_ref, ho_ref, xo_ref):
    y = jnp.dot(a_ref[...], w_ref[...], preferred_element_type=F32)
    h_new = h_ref[...] + gt_ref[...] * y
    ho_ref[...] = h_new
    xo_ref[...] = (_rms(h_new, g_ref[...]) * (1.0 + sc_ref[...]) + sh_ref[...]).astype(xo_ref.dtype)


def _ffn_down_final_kernel(a_ref, w_ref, h_ref, gt_ref, g_ref, o_ref):
    y = jnp.dot(a_ref[...], w_ref[...], preferred_element_type=F32)
    o_ref[...] = _rms(h_ref[...] + gt_ref[...] * y, g_ref[...])


def _ffn(xm, w_up, conv_w, w_down, w_down_b, layer, h, mods, mod_row, seq_len, nxt):
    m, d = xm.shape
    tf = FFN_TF
    nj = D_FF // tf
    tm = FFN_UP_TM if seq_len >= FFN_UP_TM else min(m, FFN_UP_TM)
    tps = max(seq_len // tm, 1)
    nseq = max(tm // seq_len, 1)
    stage_rows = nseq * (tm // nseq + BF16_SUBLANES) + BF16_SUBLANES
    row_tile = _back_and_forth(m // tm)
    xp, xn = _halo_specs(tm, d, 0, m, row_tile)
    cast_down = w_down_b is None
    in_specs = [
        pl.BlockSpec((tm, d), lambda j, i: (row_tile(j, i), 0)),
        xp, xn,
        pl.BlockSpec((None, d, tf), lambda j, i: (layer, 0, j)),
        pl.BlockSpec((None, d, tf), lambda j, i: (layer, 0, nj + j)),
        pl.BlockSpec((None, 3, tf), lambda j, i: (layer, 0, j)),
        pl.BlockSpec((None, 3, tf), lambda j, i: (layer, 0, nj + j)),
    ]
    args = [xm, xm, xm, w_up, w_up, conv_w, conv_w]
    out_specs = [pl.BlockSpec((tm, tf), lambda j, i: (row_tile(j, i), j))]
    out_shape = [jax.ShapeDtypeStruct((m, D_FF), BF16)]
    if cast_down:
        in_specs.append(pl.BlockSpec((None, tf, d), lambda j, i: (layer, j, 0)))
        args.append(w_down)
        out_specs.append(pl.BlockSpec((tf, d), lambda j, i: (j, 0)))
        out_shape.append(jax.ShapeDtypeStruct((D_FF, d), BF16))
    outs = pl.pallas_call(
        functools.partial(_ffn_up_kernel, tps=tps, nseq=nseq, cast_down=cast_down, row_tile=row_tile),
        grid=(nj, m // tm),
        in_specs=in_specs,
        out_specs=out_specs,
        out_shape=out_shape,
        scratch_shapes=[
            pltpu.VMEM((d, tf), BF16),
            pltpu.VMEM((d, tf), BF16),
            pltpu.VMEM((stage_rows, d), BF16),
            pltpu.VMEM((stage_rows, tf), F32),
            pltpu.VMEM((stage_rows, tf), F32),
        ],
        compiler_params=_cparams("arbitrary", "arbitrary"),
        name="ffn_up",
    )(*args)
    act = outs[0]
    if cast_down:
        w_down_b = outs[1]

    tmd = min(m, FFN_DOWN_TM)
    row = lambda i: (i, 0)
    in_specs = [
        pl.BlockSpec((tmd, D_FF), row),
        pl.BlockSpec((D_FF, d), lambda i: (0, 0), pipeline_mode=pl.Buffered(1)),
        pl.BlockSpec((tmd, d), row),
        _mod_spec(5, mod_row, tmd),
        pl.BlockSpec((1, d), lambda i: (0, 0)),
    ]
    g_next, mods_next = nxt
    if mods_next is None:
        return pl.pallas_call(
            _ffn_down_final_kernel,
            grid=(m // tmd,),
            in_specs=in_specs,
            out_specs=pl.BlockSpec((tmd, d), row),
            out_shape=jax.ShapeDtypeStruct((m, d), F32),
            compiler_params=_cparams("arbitrary"),
            name="ffn_down_final",
        )(act, w_down_b, h, mods, g_next.reshape(1, d)), None, w_down_b
    h_new, x_next = pl.pallas_call(
        _ffn_down_kernel,
        grid=(m // tmd,),
        in_specs=in_specs + [_mod_spec(0, mod_row, tmd), _mod_spec(1, mod_row, tmd)],
        out_specs=[pl.BlockSpec((tmd, d), row), pl.BlockSpec((tmd, d), row)],
        out_shape=[jax.ShapeDtypeStruct((m, d), F32), jax.ShapeDtypeStruct((m, d), BF16)],
        compiler_params=_cparams("arbitrary"),
        name="ffn_down",
    )(act, w_down_b, h, mods, g_next.reshape(1, d), mods_next, mods_next)
    return h_new, x_next, w_down_b


def _rope_tables(length):
    n_freq = HEAD_DIM // 4
    pos = jnp.arange(length)
    row = (pos // GRID_W).astype(F32)
    col = (pos % GRID_W).astype(F32)
    inv = ROPE_THETA ** (-(2.0 * jnp.arange(n_freq, dtype=F32)) / (HEAD_DIM // 2))
    ang_r, ang_c = row[:, None] * inv, col[:, None] * inv
    ang = jnp.concatenate([ang_r, ang_r, ang_c, ang_c], axis=1)
    cos = jnp.tile(jnp.cos(ang), (1, 2))
    sin = jnp.tile(jnp.sin(ang), (1, 2))
    is_x1 = (jnp.arange(LANES) % (2 * n_freq)) < n_freq
    return cos, jnp.where(is_x1, -sin, 0.0), jnp.where(is_x1, 0.0, sin)


def _dft_tables(n, tr):
    m = jnp.arange(n, dtype=jnp.int32)

    def cos_sin(rows):
        ang = ((rows[:, None] * m[None, :]) % n).astype(F32) * (2.0 * math.pi / n)
        return jnp.cos(ang), jnp.sin(ang)

    rc, rs = cos_sin(jnp.arange(n // tr, dtype=jnp.int32) * tr)
    g = 1 << ((tr.bit_length() - 1 + 1) // 2)
    assert tr % g == 0
    ca, sa = cos_sin(jnp.arange(tr // g, dtype=jnp.int32) * g)
    cb, sb = cos_sin(jnp.arange(g, dtype=jnp.int32))
    c0 = (ca[:, None, :] * cb[None, :, :] - sa[:, None, :] * sb[None, :, :]).reshape(tr, n)
    s0 = (sa[:, None, :] * cb[None, :, :] + ca[:, None, :] * sb[None, :, :]).reshape(tr, n)
    return c0.astype(BF16), s0.astype(BF16), rc, rs


def _channel_dft():
    c = np.arange(FOURIER_GROUP_W)
    ang = 2.0 * np.pi * ((c[:, None] * c[None, :]) % FOURIER_GROUP_W) / FOURIER_GROUP_W
    return jnp.asarray(np.concatenate([np.cos(ang), np.sin(ang)], axis=1), dtype=BF16)


def _project(xm, seq_len, rope_tabs, w, proj_cols=(0, PROJ_TILES)):
    return _in_proj(xm, w["w_in"], w["layer"], rope_tabs, min(xm.shape[0], PROJ_TM), proj_cols[0],
                    proj_cols[1], seq_len)


def _mixer(h, p, seq_len, dft, ctx_kv, w, mods, mod_row, lam_init, tm, tq, nxt):
    m = h.shape[0]
    b = m // seq_len
    p3 = p.reshape(b, seq_len, N_IN)
    fmix = _fourier(p3, w["cs"], dft, min(seq_len, FOURIER_TR)).reshape(m, FOURIER_W)
    if ctx_kv is None:
        attn = _attention(p3, P_Q_OFF // LANES, p3, P_K_OFF // LANES, P_V_OFF // LANES, None,
                          w["lambdas"], w["subln_g"], lam_init, tq)
    else:
        c3, kcblk, vcblk = ctx_kv
        attn = _attention(p3, P_Q_OFF // LANES, c3, kcblk, vcblk, p3,
                          w["lambdas"], w["subln_g"], lam_init, tq)
    h_mid, xm2 = _merge_out(p, fmix, attn.reshape(m, ATTN_V_W), w["conv_mix_w"], w["w_br_fourier"],
                            w["w_br_conv"], w["w_br_attn"], w["w_out"], w["layer"], h, w["g_norm2"],
                            mods, mod_row, min(tm, MERGE_TM), seq_len)
    h_new, x_next, w["w_ffn_down_b"] = _ffn(xm2, w["w_ffn_up"], w["ffn_conv_w"], w["w_ffn_down"],
                                            w["w_ffn_down_b"], w["layer"], h_mid, mods, mod_row, seq_len, nxt)
    return h_new, x_next


def kernel(x, c, ctx, c_ctx, w_mod, b_mod, g_norm1, g_norm2, w_in, conv_mix_w, lambdas, subln_g,
           w_br_fourier, w_br_conv, w_br_attn, w_out, w_ffn_up, ffn_conv_w, w_ffn_down, g_final):
    b, seq, d = x.shape
    n_ctx = ctx.shape[1]
    depth = w_mod.shape[0]
    assert d == D_MODEL and b + 1 <= MOD_ROWS

    wb = lambda a: a.astype(BF16)
    w_bf, w_bc, w_ba, w_o = wb(w_br_fourier), wb(w_br_conv), wb(w_br_attn), wb(w_out)

    rope_tabs = _rope_tables(seq)
    dft_l = _dft_tables(seq, min(seq, FOURIER_TR))
    dft_c = _dft_tables(n_ctx, min(n_ctx, FOURIER_TR))
    cs = _channel_dft()

    cvec = jnp.zeros((MOD_ROWS, d), F32).at[:b].set(c).at[b].set(c_ctx)
    mods_all = _modulation(cvec.T, b + 1, w_mod, b_mod).reshape(depth, MOD_ROWS, 6, 1, d)

    tm_l = min(512, seq)
    tm_c = n_ctx
    tq = min(512, seq)
    lat_row = lambda r: r // seq
    ctx_row = lambda r: b

    h = x.reshape(b * seq, d)
    hc = ctx.reshape(b * n_ctx, d)
    xc = _norm_modulate(hc, g_norm1[0], mods_all[0], ctx_row, tm_c)
    xl = _norm_modulate(h, g_norm1[0], mods_all[0], lat_row, tm_l)
    for i in range(depth):
        last = i == depth - 1
        lam_init = 0.8 - 0.6 * math.exp(-0.3 * i)
        mods = mods_all[i]
        w = dict(w_in=w_in, layer=i, cs=cs, lambdas=lambdas[i], subln_g=subln_g[i], conv_mix_w=conv_mix_w,
                 w_br_fourier=w_bf, w_br_conv=w_bc, w_br_attn=w_ba, w_out=w_o,
                 g_norm2=g_norm2[i], w_ffn_up=w_ffn_up, ffn_conv_w=ffn_conv_w, w_ffn_down=w_ffn_down,
                 w_ffn_down_b=None)
        nxt = (g_final, None) if last else (g_norm1[i + 1], mods_all[i + 1])

        if last:
            pc = _project(xc, n_ctx, None, w, proj_cols=(PROJ_K_TILE, 2))
            ctx_kv = (pc.reshape(b, n_ctx, 2 * PROJ_TN), 0, PROJ_TN // LANES)
        else:
            pc = _project(xc, n_ctx, None, w)
            ctx_kv = (pc.reshape(b, n_ctx, N_IN), P_K_OFF // LANES, P_V_OFF // LANES)
        h, xl = _mixer(h, _project(xl, seq, rope_tabs, w), seq, dft_l, ctx_kv, w, mods, lat_row, lam_init,
                       tm_l, tq, nxt)
        if not last:
            hc, xc = _mixer(hc, pc, n_ctx, dft_c, None, w, mods, ctx_row, lam_init, tm_c, n_ctx, nxt)
    return h.reshape(b, seq, d)
```

```python
import functools
import math

import numpy as np
import jax
import jax.numpy as jnp
from jax import lax
from jax.experimental import pallas as pl
from jax.experimental.pallas import tpu as pltpu

F32 = jnp.float32
BF16 = jnp.bfloat16

D_MODEL = 2048
GRID_W = 64
FOURIER_GROUPS = 4
FOURIER_GROUP_W = D_MODEL // 16
FOURIER_W = FOURIER_GROUPS * FOURIER_GROUP_W
CONV_W = D_MODEL // 4
N_HEADS = 8
HEAD_DIM = D_MODEL // (4 * N_HEADS)
VAL_DIM = 2 * HEAD_DIM
ATTN_QK_W = N_HEADS * 2 * HEAD_DIM
ATTN_V_W = N_HEADS * VAL_DIM
ROPE_THETA = 10000.0
ATTN_SCALE = HEAD_DIM ** -0.5
SUBLN_EPS = 1e-5
Q_OFF = FOURIER_W + 3 * CONV_W
K_OFF = Q_OFF + ATTN_QK_W
V_OFF = K_OFF + ATTN_QK_W
V_END = V_OFF + ATTN_V_W
N_IN = V_END + 3 * D_MODEL
D_FF = ((8 * D_MODEL // 3 + 255) // 256) * 256
EPS = 1e-6

LANES = 128
SUBLANES = 8
MXU_COLS = 256
BF16_SUBLANES = 16
VMEM_LIMIT = 56 * 1024 * 1024
MOD_ROWS = 8

PROJ_TN = 1024
PROJ_TM = 1024
P_GATE_OFF = Q_OFF
P_Q_OFF = P_GATE_OFF + 3 * D_MODEL
P_K_OFF = P_Q_OFF + ATTN_QK_W
P_V_OFF = P_K_OFF + ATTN_QK_W
PROJ_GATE_TILE = P_GATE_OFF // PROJ_TN
PROJ_Q_TILE = P_Q_OFF // PROJ_TN
PROJ_K_TILE = P_K_OFF // PROJ_TN
PROJ_TILES = N_IN // PROJ_TN
ATTN_TK = 512
ATTN_SCORES_AHEAD = 1
FOURIER_TR = 512
MERGE_TM = 256
FFN_TF = 512
FFN_UP_TM = 1024
FFN_DOWN_TM = 256
FFN2_TN = 1024


def _cparams(*sem):
    return pltpu.CompilerParams(dimension_semantics=sem, vmem_limit_bytes=VMEM_LIMIT)


def _mod_kernel(ct_ref, w_ref, b_ref, o_ref, *, n_rows):
    ct = ct_ref[...]
    st = ct * jax.nn.sigmoid(ct)
    w = w_ref[...]
    d, tn = w.shape
    rows = []
    for r in range(n_rows):
        t = (st[:, r:r + 1] * w).reshape(d // MOD_ROWS, MOD_ROWS, tn)
        rows.append(jnp.sum(jnp.sum(t, axis=0), axis=0, keepdims=True))
    rows.append(jnp.zeros((MOD_ROWS - n_rows, tn), F32))
    o_ref[...] = jnp.concatenate(rows, axis=0) + b_ref[...]


def _modulation(cvec_t, n_rows, w_mod, b_mod):
    depth, d, n = w_mod.shape
    tn = 1024
    return pl.pallas_call(
        functools.partial(_mod_kernel, n_rows=n_rows),
        grid=(depth, n // tn),
        in_specs=[
            pl.BlockSpec((d, MOD_ROWS), lambda l, j: (0, 0)),
            pl.BlockSpec((None, d, tn), lambda l, j: (l, 0, j)),
            pl.BlockSpec((None, 1, tn), lambda l, j: (l, 0, j)),
        ],
        out_specs=pl.BlockSpec((None, MOD_ROWS, tn), lambda l, j: (l, 0, j)),
        out_shape=jax.ShapeDtypeStruct((depth, MOD_ROWS, n), F32),
        compiler_params=_cparams("arbitrary", "arbitrary"),
        name="modulation",
    )(cvec_t, w_mod, b_mod.reshape(depth, 1, n))


def _mod_spec(which, mod_row, tm):
    return pl.BlockSpec((None, None, 1, D_MODEL), lambda i, *_: (mod_row(i * tm), which, 0, 0))


def _rms(x, g):
    return x * lax.rsqrt(jnp.mean(x * x, axis=-1, keepdims=True) + EPS) * g


def _norm_mod_kernel(h_ref, g_ref, sh_ref, sc_ref, o_ref):
    y = _rms(h_ref[...], g_ref[...])
    o_ref[...] = (y * (1.0 + sc_ref[...]) + sh_ref[...]).astype(o_ref.dtype)


def _norm_modulate(h, g, mods, mod_row, tm):
    m, d = h.shape
    return pl.pallas_call(
        _norm_mod_kernel,
        grid=(m // tm,),
        in_specs=[
            pl.BlockSpec((tm, d), lambda i: (i, 0)),
            pl.BlockSpec((1, d), lambda i: (0, 0)),
            _mod_spec(0, mod_row, tm),
            _mod_spec(1, mod_row, tm),
        ],
        out_specs=pl.BlockSpec((tm, d), lambda i: (i, 0)),
        out_shape=jax.ShapeDtypeStruct((m, d), BF16),
        compiler_params=_cparams("arbitrary"),
        name="norm_modulate",
    )(h, g.reshape(1, d), mods, mods)


def _proj_kernel(*refs, col0, rope):
    if rope:
        x_ref, w_ref, cos_ref, sin_up_ref, sin_dn_ref, o_ref, wb_ref = refs
    else:
        x_ref, w_ref, o_ref, wb_ref = refs
    j = pl.program_id(0) + col0

    @pl.when(pl.program_id(1) == 0)
    def _():
        wb_ref[...] = w_ref[...].astype(BF16)

    is_q = j == PROJ_Q_TILE
    is_k = j == PROJ_K_TILE
    x = x_ref[...]

    def column_chunks(epilogue):
        for c in range(PROJ_TN // MXU_COLS):
            cols = slice(c * MXU_COLS, (c + 1) * MXU_COLS)
            acc = jnp.dot(x, wb_ref[:, cols], preferred_element_type=F32)
            o_ref[:, cols] = epilogue(acc).astype(o_ref.dtype)

    def rotary(acc):
        qscale = jnp.where(is_q, ATTN_SCALE * math.log2(math.e), 1.0)
        half = HEAD_DIM // 4
        out = []
        for hd in range(MXU_COLS // LANES):
            a = acc[:, hd * LANES:(hd + 1) * LANES]
            if rope:
                a = (a * cos_ref[...] + pltpu.roll(a, LANES - half, 1) * sin_up_ref[...]
                     + pltpu.roll(a, half, 1) * sin_dn_ref[...])
            out.append(a * qscale)
        return jnp.concatenate(out, axis=1)

    pl.when(is_q | is_k)(lambda: column_chunks(rotary))
    pl.when(jnp.logical_not(is_q | is_k))(lambda: column_chunks(lambda acc: acc))


def _in_proj(xm, w_in, layer, rope_tabs, tm, col0, ncols, seq_len):
    m, d = xm.shape
    tn = PROJ_TN
    n_plain = PROJ_GATE_TILE
    n_gate = PROJ_Q_TILE - PROJ_GATE_TILE
    n_qkv = PROJ_TILES - PROJ_Q_TILE

    def w_tile(j):
        j = j + col0
        return jnp.where(j < n_plain, j, jnp.where(j < n_plain + n_gate, j + n_qkv, j - n_gate))

    n_i = m // tm
    row_tile = _back_and_forth(n_i)

    in_specs = [
        pl.BlockSpec((tm, d), lambda j, i: (row_tile(j, i), 0)),
        pl.BlockSpec((None, d, tn), lambda j, i: (layer, 0, w_tile(j))),
    ]
    args = [xm, w_in]
    rope = rope_tabs is not None
    if rope:
        tps = seq_len // tm
        in_specs += [pl.BlockSpec((tm, LANES), lambda j, i: (row_tile(j, i) % tps, 0))] * 3
        args += list(rope_tabs)
    return pl.pallas_call(
        functools.partial(_proj_kernel, col0=col0, rope=rope),
        grid=(ncols, n_i),
        in_specs=in_specs,
        out_specs=pl.BlockSpec((tm, tn), lambda j, i: (row_tile(j, i), j)),
        out_shape=jax.ShapeDtypeStruct((m, ncols * tn), BF16),
        scratch_shapes=[pltpu.VMEM((d, tn), BF16)],
        compiler_params=_cparams("arbitrary", "arbitrary"),
        name="in_proj",
    )(*args)


def _fourier_kernel(f_ref, cs_ref, c0_ref, s0_ref, rc_ref, rs_ref, o_ref, uc_ref, us_ref, *, chunk, scale):
    i = pl.program_id(1)
    seq = f_ref.shape[0]
    gw = FOURIER_GROUP_W

    @pl.when(i == 0)
    def _():
        for r in range(seq // chunk):
            rows = slice(r * chunk, (r + 1) * chunk)
            for g in range(FOURIER_GROUPS):
                cols = slice(g * gw, (g + 1) * gw)
                y = jnp.dot(f_ref[rows, cols], cs_ref[...], preferred_element_type=F32)
                uc_ref[rows, cols] = y[:, :gw].astype(BF16)
                us_ref[rows, cols] = y[:, gw:].astype(BF16)

    c0, s0 = c0_ref[...].astype(F32), s0_ref[...].astype(F32)
    rc, rs = rc_ref[pl.ds(i, 1), :], rs_ref[pl.ds(i, 1), :]
    cos_t = (c0 * rc - s0 * rs).astype(BF16)
    sin_t = (s0 * rc + c0 * rs).astype(BF16)
    y = jnp.dot(cos_t, uc_ref[...], preferred_element_type=F32)
    y -= jnp.dot(sin_t, us_ref[...], preferred_element_type=F32)
    o_ref[...] = (y * scale).astype(o_ref.dtype)


def _fourier(p3, cs, dft, tr):
    b, seq, _ = p3.shape
    chunk = min(seq, 1024)
    scale = 1.0 / math.sqrt(seq * FOURIER_GROUP_W)
    c0, s0, rc, rs = dft
    full = lambda a: pl.BlockSpec(a.shape, lambda bi, i: (0, 0), pipeline_mode=pl.Buffered(1))
    return pl.pallas_call(
        functools.partial(_fourier_kernel, chunk=chunk, scale=scale),
        grid=(b, seq // tr),
        in_specs=[
            pl.BlockSpec((None, seq, FOURIER_W), lambda bi, i: (bi, 0, 0)),
            pl.BlockSpec((FOURIER_GROUP_W, 2 * FOURIER_GROUP_W), lambda bi, i: (0, 0)),
            full(c0), full(s0), full(rc), full(rs),
        ],
        out_specs=pl.BlockSpec((None, tr, FOURIER_W), lambda bi, i: (bi, i, 0)),
        out_shape=jax.ShapeDtypeStruct((b, seq, FOURIER_W), BF16),
        scratch_shapes=[pltpu.VMEM((seq, FOURIER_W), BF16), pltpu.VMEM((seq, FOURIER_W), BF16)],
        compiler_params=_cparams("arbitrary", "arbitrary"),
        name="fourier",
    )(p3, cs, c0, s0, rc, rs)


def _attn_kernel(*refs, tq, tk, n_lat, lam_init):
    if n_lat:
        lam_ref, q_ref, kc_ref, vc_ref, kl_ref, vl_ref, g_ref, o_ref = refs
    else:
        lam_ref, q_ref, kc_ref, vc_ref, g_ref, o_ref = refs
        kl_ref = vl_ref = None
    n_ctx = kc_ref.shape[0]
    q = q_ref[...]
    lane = lax.broadcasted_iota(jnp.int32, (1, LANES), 1)
    comp0 = lane < HEAD_DIM
    zero = jnp.zeros_like(q)
    qq = jnp.concatenate([jnp.where(comp0, q, zero), jnp.where(comp0, zero, q)], axis=0)
    chunks = [(kc_ref, vc_ref, 0, n_ctx)] + [(kl_ref, vl_ref, c * tk, tk) for c in range(n_lat)]

    def lane_fold(x, op):
        part = x[:, 0:LANES]
        for c in range(1, x.shape[1] // LANES):
            part = op(part, x[:, c * LANES:(c + 1) * LANES])
        return part

    def scores(c):
        kr, _, r0, sz = chunks[c]
        return lax.dot_general(qq, kr[r0:r0 + sz, :], (((1,), (1,)), ((), ())), preferred_element_type=F32)

    m = l = acc = None
    ahead = ATTN_SCORES_AHEAD
    pending = [scores(c) for c in range(min(ahead, len(chunks)))]
    for c, (_, vr, r0, sz) in enumerate(chunks):
        s = pending.pop(0)
        if c + ahead < len(chunks):
            pending.append(scores(c + ahead))
        mc = jnp.max(lane_fold(s, jnp.maximum), axis=-1, keepdims=True)
        m_new = mc if m is None else jnp.maximum(m, mc)
        p = jnp.exp2(s - m_new)
        lc = lane_fold(p, jnp.add)
        y = jnp.dot(p.astype(BF16), vr[r0:r0 + sz, :], preferred_element_type=F32)
        if m is None:
            l, acc = lc, y
        else:
            alpha = jnp.exp2(m - m_new)
            l = alpha * l + lc
            acc = alpha * acc + y
        m = m_new
    o = acc / jnp.sum(l, axis=-1, keepdims=True)
    lf = lam_ref[...]
    lam = (jnp.exp(jnp.sum(lf[0:1] * lf[1:2], axis=-1, keepdims=True))
           - jnp.exp(jnp.sum(lf[2:3] * lf[3:4], axis=-1, keepdims=True)) + lam_init)
    o = o[:tq] - lam * o[tq:]
    o = o * lax.rsqrt(jnp.mean(o * o, axis=-1, keepdims=True) + SUBLN_EPS)
    o_ref[...] = (o * g_ref[...] * (1.0 - lam_init)).astype(o_ref.dtype)


def _attention(q3, qblk, c3, kcblk, vcblk, l3, lam, subln_g, lam_init, tq):
    b, lq, _ = q3.shape
    n_ctx = c3.shape[1]
    tk = ATTN_TK
    in_specs = [
        pl.BlockSpec((4, HEAD_DIM), lambda bi, h, i: (0, 0)),
        pl.BlockSpec((None, tq, LANES), lambda bi, h, i: (bi, i, qblk + h)),
        pl.BlockSpec((None, n_ctx, LANES), lambda bi, h, i: (bi, 0, kcblk + h)),
        pl.BlockSpec((None, n_ctx, LANES), lambda bi, h, i: (bi, 0, vcblk + h)),
    ]
    args = [lam, q3, c3, c3]
    n_lat = 0
    if l3 is not None:
        seq = l3.shape[1]
        n_lat = seq // min(tk, seq)
        tk = seq // n_lat
        in_specs += [
            pl.BlockSpec((None, seq, LANES), lambda bi, h, i: (bi, 0, P_K_OFF // LANES + h)),
            pl.BlockSpec((None, seq, LANES), lambda bi, h, i: (bi, 0, P_V_OFF // LANES + h)),
        ]
        args += [l3, l3]
    in_specs.append(pl.BlockSpec((1, VAL_DIM), lambda bi, h, i: (0, 0)))
    args.append(subln_g.reshape(1, VAL_DIM))
    return pl.pallas_call(
        functools.partial(_attn_kernel, tq=tq, tk=tk, n_lat=n_lat, lam_init=lam_init),
        grid=(b, N_HEADS, lq // tq),
        in_specs=in_specs,
        out_specs=pl.BlockSpec((None, tq, LANES), lambda bi, h, i: (bi, i, h)),
        out_shape=jax.ShapeDtypeStruct((b, lq, ATTN_V_W), BF16),
        compiler_params=_cparams("arbitrary", "arbitrary", "arbitrary"),
        name="diff_attention",
    )(*args)


def _conv3_rows(buf_ref, w, tm, base=BF16_SUBLANES):
    return (buf_ref[base - 1:base - 1 + tm, :] * w[0:1] + buf_ref[base:base + tm, :] * w[1:2]
            + buf_ref[base + 1:base + 1 + tm, :] * w[2:3])


def _halo_specs(tm, width, colblk, m, row_tile=lambda i: i):
    per = tm // BF16_SUBLANES
    last = m // BF16_SUBLANES - 1
    prev = pl.BlockSpec((BF16_SUBLANES, width),
                        lambda *ids: (jnp.maximum(row_tile(*ids) * per - 1, 0), colblk))
    nxt = pl.BlockSpec((BF16_SUBLANES, width),
                       lambda *ids: (jnp.minimum((row_tile(*ids) + 1) * per, last), colblk))
    return prev, nxt


def _back_and_forth(n_i):
    return lambda j, i: jnp.where(j % 2 == 0, i, n_i - 1 - i)


def _merge_kernel(f_ref, cb_ref, cc_ref, cx_ref, ccp_ref, cxp_ref, ccn_ref, cxn_ref, a_ref,
                  gf_ref, gc_ref, ga_ref, cw_ref, wbf_ref, wbc_ref, wba_ref, wo_ref,
                  h_ref, gt_ref, g2_ref, sh_ref, sc_ref, ho_ref, xo_ref, z_ref, mg_ref, *, tps, nchunk):
    i = pl.program_id(0)
    tm = cc_ref.shape[0]
    h = BF16_SUBLANES
    first = (i % tps) == 0
    last = (i % tps) == tps - 1
    zp = ccp_ref[...].astype(F32) * cxp_ref[...].astype(F32)
    zn = ccn_ref[...].astype(F32) * cxn_ref[...].astype(F32)
    z_ref[0:h, :] = jnp.where(first, 0.0, zp)
    z_ref[h:h + tm, :] = cc_ref[...].astype(F32) * cx_ref[...].astype(F32)
    z_ref[h + tm:2 * h + tm, :] = jnp.where(last, 0.0, zn)
    cv = (cb_ref[...].astype(F32) * _conv3_rows(z_ref, cw_ref[...], tm)).astype(BF16)
    f = f_ref[...]
    a = a_ref[...]
    cn = D_MODEL // nchunk
    for n in range(nchunk):
        cols = slice(n * cn, (n + 1) * cn)
        gate = lambda g_ref: jax.nn.sigmoid(g_ref[:, cols].astype(F32))
        y = gate(gf_ref) * jnp.dot(f, wbf_ref[:, cols], preferred_element_type=F32)
        y += gate(gc_ref) * jnp.dot(cv, wbc_ref[:, cols], preferred_element_type=F32)
        y += gate(ga_ref) * jnp.dot(a, wba_ref[:, cols], preferred_element_type=F32)
        mg_ref[:, cols] = y.astype(BF16)
    y = jnp.dot(mg_ref[...], wo_ref[...], preferred_element_type=F32)
    h_new = h_ref[...] + gt_ref[...] * y
    ho_ref[...] = h_new
    xo_ref[...] = (_rms(h_new, g2_ref[...]) * (1.0 + sc_ref[...]) + sh_ref[...]).astype(xo_ref.dtype)


def _merge_out(p, fmix, attn, conv_w, w_bf, w_bc, w_ba, w_out, layer, h, g2, mods, mod_row, tm, seq_len):
    m = p.shape[0]
    d = D_MODEL
    cw = CONV_W
    tps = seq_len // tm
    blk = lambda width, c: pl.BlockSpec((tm, width), lambda i: (i, c))
    ccp, ccn = _halo_specs(tm, cw, 2, m)
    cxp, cxn = _halo_specs(tm, cw, 3, m)
    gate0 = P_GATE_OFF // d
    full = lambda r, c: pl.BlockSpec((None, r, c), lambda i: (layer, 0, 0), pipeline_mode=pl.Buffered(1))
    return pl.pallas_call(
        functools.partial(_merge_kernel, tps=tps, nchunk=4),
        grid=(m // tm,),
        in_specs=[
            blk(FOURIER_W, 0),
            blk(cw, 1), blk(cw, 2), blk(cw, 3),
            ccp, cxp, ccn, cxn,
            blk(ATTN_V_W, 0),
            pl.BlockSpec((tm, d), lambda i: (i, gate0)),
            pl.BlockSpec((tm, d), lambda i: (i, gate0 + 1)),
            pl.BlockSpec((tm, d), lambda i: (i, gate0 + 2)),
            full(3, cw), full(FOURIER_W, d), full(cw, d), full(ATTN_V_W, d), full(d, d),
            pl.BlockSpec((tm, d), lambda i: (i, 0)),
            _mod_spec(2, mod_row, tm),
            pl.BlockSpec((1, d), lambda i: (0, 0)),
            _mod_spec(3, mod_row, tm),
            _mod_spec(4, mod_row, tm),
        ],
        out_specs=[pl.BlockSpec((tm, d), lambda i: (i, 0)), pl.BlockSpec((tm, d), lambda i: (i, 0))],
        out_shape=[jax.ShapeDtypeStruct((m, d), F32), jax.ShapeDtypeStruct((m, d), BF16)],
        scratch_shapes=[pltpu.VMEM((tm + 2 * BF16_SUBLANES, cw), F32), pltpu.VMEM((tm, d), BF16)],
        compiler_params=_cparams("arbitrary"),
        name="merge_out",
    )(fmix, p, p, p, p, p, p, p, attn, p, p, p, conv_w, w_bf, w_bc, w_ba, w_out,
      h, mods, g2.reshape(1, d), mods, mods)


def _ffn_up_kernel(*refs, tps, nseq, cast_down, row_tile):
    if cast_down:
        (x_ref, xp_ref, xn_ref, wa_ref, wv_ref, ca_ref, cv_ref, wd_ref, o_ref, wdb_ref,
         wab_ref, wvb_ref, xh_ref, ua_ref, uv_ref) = refs
    else:
        (x_ref, xp_ref, xn_ref, wa_ref, wv_ref, ca_ref, cv_ref, o_ref,
         wab_ref, wvb_ref, xh_ref, ua_ref, uv_ref) = refs
    i = pl.program_id(1)
    tm = x_ref.shape[0]
    hl = BF16_SUBLANES
    seq = tm // nseq

    @pl.when(i == 0)
    def _():
        wab_ref[...] = wa_ref[...].astype(BF16)
        wvb_ref[...] = wv_ref[...].astype(BF16)
        if cast_down:
            wdb_ref[...] = wd_ref[...].astype(BF16)

    base = [hl + s * (seq + hl) for s in range(nseq)]
    zeros = jnp.zeros((hl, x_ref.shape[1]), BF16)
    if nseq == 1:
        it = row_tile(pl.program_id(0), i)
        first = (it % tps) == 0
        last = (it % tps) == tps - 1
        xh_ref[0:hl, :] = jnp.where(first, zeros, xp_ref[...])
        xh_ref[hl + tm:2 * hl + tm, :] = jnp.where(last, zeros, xn_ref[...])
    else:
        for s in range(nseq + 1):
            xh_ref[s * (seq + hl):s * (seq + hl) + hl, :] = zeros
    for s in range(nseq):
        xh_ref[base[s]:base[s] + seq, :] = x_ref[s * seq:(s + 1) * seq, :]
    xh = xh_ref[...]
    ua_ref[...] = jnp.dot(xh, wab_ref[...], preferred_element_type=F32)
    uv_ref[...] = jnp.dot(xh, wvb_ref[...], preferred_element_type=F32)
    for s in range(nseq):
        a = _conv3_rows(ua_ref, ca_ref[...], seq, base[s])
        v = _conv3_rows(uv_ref, cv_ref[...], seq, base[s])
        o_ref[s * seq:(s + 1) * seq, :] = (a * jax.nn.sigmoid(a) * v).astype(o_ref.dtype)


def _ffn_down_kernel(a_ref, w_ref, h_ref, gt_ref, g_ref, sh_ref, sc_ref, ho_ref, xo_ref):
    y = jnp.dot(a_ref[...], w_ref[...], preferred_element_type=F32)
    h_new = h_ref[...] + gt_ref[...] * y
    ho_ref[...] = h_new
    xo_ref[...] = (_rms(h_new, g_ref[...]) * (1.0 + sc_ref[...]) + sh_ref[...]).astype(xo_ref.dtype)


def _ffn_down_final_kernel(a_ref, w_ref, h_ref, gt_ref, g_ref, o_ref):
    y = jnp.dot(a_ref[...], w_ref[...], preferred_element_type=F32)
    o_ref[...] = _rms(h_ref[...] + gt_ref[...] * y, g_ref[...])


def _ffn(xm, w_up, conv_w, w_down, w_down_b, layer, h, mods, mod_row, seq_len, nxt):
    m, d = xm.shape
    tf = FFN_TF
    nj = D_FF // tf
    tm = FFN_UP_TM if seq_len >= FFN_UP_TM else min(m, FFN_UP_TM)
    tps = max(seq_len // tm, 1)
    nseq = max(tm // seq_len, 1)
    stage_rows = nseq * (tm // nseq + BF16_SUBLANES) + BF16_SUBLANES
    row_tile = _back_and_forth(m // tm)
    xp, xn = _halo_specs(tm, d, 0, m, row_tile)
    cast_down = w_down_b is None
    in_specs = [
        pl.BlockSpec((tm, d), lambda j, i: (row_tile(j, i), 0)),
        xp, xn,
        pl.BlockSpec((None, d, tf), lambda j, i: (layer, 0, j)),
        pl.BlockSpec((None, d, tf), lambda j, i: (layer, 0, nj + j)),
        pl.BlockSpec((None, 3, tf), lambda j, i: (layer, 0, j)),
        pl.BlockSpec((None, 3, tf), lambda j, i: (layer, 0, nj + j)),
    ]
    args = [xm, xm, xm, w_up, w_up, conv_w, conv_w]
    out_specs = [pl.BlockSpec((tm, tf), lambda j, i: (row_tile(j, i), j))]
    out_shape = [jax.ShapeDtypeStruct((m, D_FF), BF16)]
    if cast_down:
        in_specs.append(pl.BlockSpec((None, tf, d), lambda j, i: (layer, j, 0)))
        args.append(w_down)
        out_specs.append(pl.BlockSpec((tf, d), lambda j, i: (j, 0)))
        out_shape.append(jax.ShapeDtypeStruct((D_FF, d), BF16))
    outs = pl.pallas_call(
        functools.partial(_ffn_up_kernel, tps=tps, nseq=nseq, cast_down=cast_down, row_tile=row_tile),
        grid=(nj, m // tm),
        in_specs=in_specs,
        out_specs=out_specs,
        out_shape=out_shape,
        scratch_shapes=[
            pltpu.VMEM((d, tf), BF16),
            pltpu.VMEM((d, tf), BF16),
            pltpu.VMEM((stage_rows, d), BF16),
            pltpu.VMEM((stage_rows, tf), F32),
            pltpu.VMEM((stage_rows, tf), F32),
        ],
        compiler_params=_cparams("arbitrary", "arbitrary"),
        name="ffn_up",
    )(*args)
    act = outs[0]
    if cast_down:
        w_down_b = outs[1]

    tmd = min(m, FFN_DOWN_TM)
    row = lambda i: (i, 0)
    in_specs = [
        pl.BlockSpec((tmd, D_FF), row),
        pl.BlockSpec((D_FF, d), lambda i: (0, 0), pipeline_mode=pl.Buffered(1)),
        pl.BlockSpec((tmd, d), row),
        _mod_spec(5, mod_row, tmd),
        pl.BlockSpec((1, d), lambda i: (0, 0)),
    ]
    g_next, mods_next = nxt
    if mods_next is None:
        return pl.pallas_call(
            _ffn_down_final_kernel,
            grid=(m // tmd,),
            in_specs=in_specs,
            out_specs=pl.BlockSpec((tmd, d), row),
            out_shape=jax.ShapeDtypeStruct((m, d), F32),
            compiler_params=_cparams("arbitrary"),
            name="ffn_down_final",
        )(act, w_down_b, h, mods, g_next.reshape(1, d)), None, w_down_b
    h_new, x_next = pl.pallas_call(
        _ffn_down_kernel,
        grid=(m // tmd,),
        in_specs=in_specs + [_mod_spec(0, mod_row, tmd), _mod_spec(1, mod_row, tmd)],
        out_specs=[pl.BlockSpec((tmd, d), row), pl.BlockSpec((tmd, d), row)],
        out_shape=[jax.ShapeDtypeStruct((m, d), F32), jax.ShapeDtypeStruct((m, d), BF16)],
        compiler_params=_cparams("arbitrary"),
        name="ffn_down",
    )(act, w_down_b, h, mods, g_next.reshape(1, d), mods_next, mods_next)
    return h_new, x_next, w_down_b


def _up_kernel(*refs, cast_down):
    if cast_down:
        x_ref, w_ref, wd_ref, o_ref, wdb_ref, wb_ref = refs
    else:
        x_ref, w_ref, o_ref, wb_ref = refs

    @pl.when(pl.program_id(1) == 0)
    def _():
        wb_ref[...] = w_ref[...].astype(BF16)
        if cast_down:
            wdb_ref[...] = wd_ref[...].astype(BF16)

    o_ref[...] = jnp.dot(x_ref[...], wb_ref[...], preferred_element_type=F32).astype(o_ref.dtype)


def _gate_down_kernel(*refs, tps, final):
    if final:
        (ua_ref, uv_ref, uap_ref, uvp_ref, uan_ref, uvn_ref, ca_ref, cv_ref, w_ref, h_ref, gt_ref, g_ref,
         o_ref, za_ref, zv_ref, af_ref, ab_ref) = refs
    else:
        (ua_ref, uv_ref, uap_ref, uvp_ref, uan_ref, uvn_ref, ca_ref, cv_ref, w_ref, h_ref, gt_ref, g_ref,
         sh_ref, sc_ref, ho_ref, xo_ref, za_ref, zv_ref, af_ref, ab_ref) = refs
    i = pl.program_id(0)
    tm = ua_ref.shape[0]
    hl = BF16_SUBLANES
    first = (i % tps) == 0
    last = (i % tps) == tps - 1
    sub = SUBLANES
    n = tm // sub
    for c in range(ua_ref.shape[1] // LANES):
        cols = slice(c * LANES, (c + 1) * LANES)
        slot = c % za_ref.shape[0]
        for z_ref, u_ref, up_ref, un_ref in ((za_ref, ua_ref, uap_ref, uan_ref), (zv_ref, uv_ref, uvp_ref, uvn_ref)):
            z_ref[slot, 0:hl, :] = jnp.where(first, 0.0, up_ref[:, cols].astype(F32))
            z_ref[slot, hl:hl + tm, :] = u_ref[:, cols].astype(F32)
            z_ref[slot, hl + tm:2 * hl + tm, :] = jnp.where(last, 0.0, un_ref[:, cols].astype(F32))
        wa, wv = ca_ref[:, cols], cv_ref[:, cols]
        rows_a = [za_ref[slot, pl.ds(hl - 1 + s, n, stride=sub), :] for s in range(sub + 2)]
        rows_v = [zv_ref[slot, pl.ds(hl - 1 + s, n, stride=sub), :] for s in range(sub + 2)]
        for s in range(sub):
            a = rows_a[s] * wa[0:1] + rows_a[s + 1] * wa[1:2] + rows_a[s + 2] * wa[2:3]
            v = rows_v[s] * wv[0:1] + rows_v[s + 1] * wv[1:2] + rows_v[s + 2] * wv[2:3]
            af_ref[slot, pl.ds(s, n, stride=sub), :] = a * jax.nn.sigmoid(a) * v
        ab_ref[:, cols] = af_ref[slot].astype(BF16)
        if (c + 1) % (FFN_TF // LANES) == 0:
            kc = slice((c + 1) * LANES - FFN_TF, (c + 1) * LANES)
            part = jnp.dot(ab_ref[:, kc], w_ref[kc, :], preferred_element_type=F32)
            y = part if c + 1 == FFN_TF // LANES else y + part
    h_new = h_ref[...] + gt_ref[...] * y
    if final:
        o_ref[...] = _rms(h_new, g_ref[...])
    else:
        ho_ref[...] = h_new
        xo_ref[...] = (_rms(h_new, g_ref[...]) * (1.0 + sc_ref[...]) + sh_ref[...]).astype(xo_ref.dtype)


def _ffn2(xm, w_up, conv_w, w_down, w_down_b, layer, h, mods, mod_row, seq_len, nxt):
    m, d = xm.shape
    tn = FFN2_TN
    tm = min(m, FFN_UP_TM)
    n_i = m // tm
    row_tile = _back_and_forth(n_i)
    cast_down = w_down_b is None
    in_specs = [
        pl.BlockSpec((tm, d), lambda j, i: (row_tile(j, i), 0)),
        pl.BlockSpec((None, d, tn), lambda j, i: (layer, 0, j)),
    ]
    args = [xm, w_up]
    out_specs = [pl.BlockSpec((tm, tn), lambda j, i: (row_tile(j, i), j))]
    out_shape = [jax.ShapeDtypeStruct((m, 2 * D_FF), BF16)]
    n_j = 2 * D_FF // tn
    if cast_down:
        rows = D_FF // n_j
        in_specs.append(pl.BlockSpec((None, rows, d), lambda j, i: (layer, j, 0)))
        args.append(w_down)
        out_specs.append(pl.BlockSpec((rows, d), lambda j, i: (j, 0)))
        out_shape.append(jax.ShapeDtypeStruct((D_FF, d), BF16))
    outs = pl.pallas_call(
        functools.partial(_up_kernel, cast_down=cast_down),
        grid=(n_j, n_i),
        in_specs=in_specs,
        out_specs=out_specs,
        out_shape=out_shape,
        scratch_shapes=[pltpu.VMEM((d, tn), BF16)],
        compiler_params=_cparams("arbitrary", "arbitrary"),
        name="ffn_up",
    )(*args)
    u = outs[0]
    if cast_down:
        w_down_b = outs[1]

    tmd = min(seq_len, FFN_DOWN_TM)
    tps = seq_len // tmd
    row = lambda i: (i, 0)
    half = lambda c: pl.BlockSpec((tmd, D_FF), lambda i: (i, c))
    uap, uan = _halo_specs(tmd, D_FF, 0, m)
    uvp, uvn = _halo_specs(tmd, D_FF, 1, m)
    in_specs = [
        half(0), half(1), uap, uvp, uan, uvn,
        pl.BlockSpec((None, 3, D_FF), lambda i: (layer, 0, 0)),
        pl.BlockSpec((None, 3, D_FF), lambda i: (layer, 0, 1)),
        pl.BlockSpec((D_FF, d), lambda i: (0, 0), pipeline_mode=pl.Buffered(1)),
        pl.BlockSpec((tmd, d), row),
        _mod_spec(5, mod_row, tmd),
        pl.BlockSpec((1, d), lambda i: (0, 0)),
    ]
    g_next, mods_next = nxt
    args = [u, u, u, u, u, u, conv_w, conv_w, w_down_b, h, mods, g_next.reshape(1, d)]
    slots = 4
    scratch = [pltpu.VMEM((slots, tmd + 2 * BF16_SUBLANES, LANES), F32)] * 2
    scratch += [pltpu.VMEM((slots, tmd, LANES), F32), pltpu.VMEM((tmd, D_FF), BF16)]
    final = mods_next is None
    if final:
        out = pl.pallas_call(
            functools.partial(_gate_down_kernel, tps=tps, final=True),
            grid=(m // tmd,),
            in_specs=in_specs,
            out_specs=pl.BlockSpec((tmd, d), row),
            out_shape=jax.ShapeDtypeStruct((m, d), F32),
            scratch_shapes=scratch,
            compiler_params=_cparams("arbitrary"),
            name="ffn_down_final",
        )(*args)
        return out, None, w_down_b
    h_new, x_next = pl.pallas_call(
        functools.partial(_gate_down_kernel, tps=tps, final=False),
        grid=(m // tmd,),
        in_specs=in_specs + [_mod_spec(0, mod_row, tmd), _mod_spec(1, mod_row, tmd)],
        out_specs=[pl.BlockSpec((tmd, d), row), pl.BlockSpec((tmd, d), row)],
        out_shape=[jax.ShapeDtypeStruct((m, d), F32), jax.ShapeDtypeStruct((m, d), BF16)],
        scratch_shapes=scratch,
        compiler_params=_cparams("arbitrary"),
        name="ffn_down",
    )(*(args + [mods_next, mods_next]))
    return h_new, x_next, w_down_b


def _rope_tables(length):
    n_freq = HEAD_DIM // 4
    pos = jnp.arange(length)
    row = (pos // GRID_W).astype(F32)
    col = (pos % GRID_W).astype(F32)
    inv = ROPE_THETA ** (-(2.0 * jnp.arange(n_freq, dtype=F32)) / (HEAD_DIM // 2))
    ang_r, ang_c = row[:, None] * inv, col[:, None] * inv
    ang = jnp.concatenate([ang_r, ang_r, ang_c, ang_c], axis=1)
    cos = jnp.tile(jnp.cos(ang), (1, 2))
    sin = jnp.tile(jnp.sin(ang), (1, 2))
    is_x1 = (jnp.arange(LANES) % (2 * n_freq)) < n_freq
    return cos, jnp.where(is_x1, -sin, 0.0), jnp.where(is_x1, 0.0, sin)


def _dft_tables(n, tr):
    m = jnp.arange(n, dtype=jnp.int32)

    def cos_sin(rows):
        ang = ((rows[:, None] * m[None, :]) % n).astype(F32) * (2.0 * math.pi / n)
        return jnp.cos(ang), jnp.sin(ang)

    rc, rs = cos_sin(jnp.arange(n // tr, dtype=jnp.int32) * tr)
    g = 1 << ((tr.bit_length() - 1 + 1) // 2)
    assert tr % g == 0
    ca, sa = cos_sin(jnp.arange(tr // g, dtype=jnp.int32) * g)
    cb, sb = cos_sin(jnp.arange(g, dtype=jnp.int32))
    c0 = (ca[:, None, :] * cb[None, :, :] - sa[:, None, :] * sb[None, :, :]).reshape(tr, n)
    s0 = (sa[:, None, :] * cb[None, :, :] + ca[:, None, :] * sb[None, :, :]).reshape(tr, n)
    return c0.astype(BF16), s0.astype(BF16), rc, rs


def _channel_dft():
    c = np.arange(FOURIER_GROUP_W)
    ang = 2.0 * np.pi * ((c[:, None] * c[None, :]) % FOURIER_GROUP_W) / FOURIER_GROUP_W
    return jnp.asarray(np.concatenate([np.cos(ang), np.sin(ang)], axis=1), dtype=BF16)


def _project(xm, seq_len, rope_tabs, w, proj_cols=(0, PROJ_TILES)):
    return _in_proj(xm, w["w_in"], w["layer"], rope_tabs, min(xm.shape[0], PROJ_TM), proj_cols[0],
                    proj_cols[1], seq_len)


def _mixer(h, p, seq_len, dft, ctx_kv, w, mods, mod_row, lam_init, tm, tq, nxt):
    m = h.shape[0]
    b = m // seq_len
    p3 = p.reshape(b, seq_len, N_IN)
    fmix = _fourier(p3, w["cs"], dft, min(seq_len, FOURIER_TR)).reshape(m, FOURIER_W)
    if ctx_kv is None:
        attn = _attention(p3, P_Q_OFF // LANES, p3, P_K_OFF // LANES, P_V_OFF // LANES, None,
                          w["lambdas"], w["subln_g"], lam_init, tq)
    else:
        c3, kcblk, vcblk = ctx_kv
        attn = _attention(p3, P_Q_OFF // LANES, c3, kcblk, vcblk, p3,
                          w["lambdas"], w["subln_g"], lam_init, tq)
    h_mid, xm2 = _merge_out(p, fmix, attn.reshape(m, ATTN_V_W), w["conv_mix_w"], w["w_br_fourier"],
                            w["w_br_conv"], w["w_br_attn"], w["w_out"], w["layer"], h, w["g_norm2"],
                            mods, mod_row, min(tm, MERGE_TM), seq_len)
    h_new, x_next, w["w_ffn_down_b"] = _ffn2(xm2, w["w_ffn_up"], w["ffn_conv_w"], w["w_ffn_down"],
                                            w["w_ffn_down_b"], w["layer"], h_mid, mods, mod_row, seq_len, nxt)
    return h_new, x_next


def kernel(x, c, ctx, c_ctx, w_mod, b_mod, g_norm1, g_norm2, w_in, conv_mix_w, lambdas, subln_g,
           w_br_fourier, w_br_conv, w_br_attn, w_out, w_ffn_up, ffn_conv_w, w_ffn_down, g_final):
    b, seq, d = x.shape
    n_ctx = ctx.shape[1]
    depth = w_mod.shape[0]
    assert d == D_MODEL and b + 1 <= MOD_ROWS

    wb = lambda a: a.astype(BF16)
    w_bf, w_bc, w_ba, w_o = wb(w_br_fourier), wb(w_br_conv), wb(w_br_attn), wb(w_out)

    rope_tabs = _rope_tables(seq)
    dft_l = _dft_tables(seq, min(seq, FOURIER_TR))
    dft_c = _dft_tables(n_ctx, min(n_ctx, FOURIER_TR))
    cs = _channel_dft()

    cvec = jnp.zeros((MOD_ROWS, d), F32).at[:b].set(c).at[b].set(c_ctx)
    mods_all = _modulation(cvec.T, b + 1, w_mod, b_mod).reshape(depth, MOD_ROWS, 6, 1, d)

    tm_l = min(512, seq)
    tm_c = n_ctx
    tq = min(512, seq)
    lat_row = lambda r: r // seq
    ctx_row = lambda r: b

    h = x.reshape(b * seq, d)
    hc = ctx.reshape(b * n_ctx, d)
    xc = _norm_modulate(hc, g_norm1[0], mods_all[0], ctx_row, tm_c)
    xl = _norm_modulate(h, g_norm1[0], mods_all[0], lat_row, tm_l)
    for i in range(depth):
        last = i == depth - 1
        lam_init = 0.8 - 0.6 * math.exp(-0.3 * i)
        mods = mods_all[i]
        w = dict(w_in=w_in, layer=i, cs=cs, lambdas=lambdas[i], subln_g=subln_g[i], conv_mix_w=conv_mix_w,
                 w_br_fourier=w_bf, w_br_conv=w_bc, w_br_attn=w_ba, w_out=w_o,
                 g_norm2=g_norm2[i], w_ffn_up=w_ffn_up, ffn_conv_w=ffn_conv_w, w_ffn_down=w_ffn_down,
                 w_ffn_down_b=None)
        nxt = (g_final, None) if last else (g_norm1[i + 1], mods_all[i + 1])

        if last:
            pc = _project(xc, n_ctx, None, w, proj_cols=(PROJ_K_TILE, 2))
            ctx_kv = (pc.reshape(b, n_ctx, 2 * PROJ_TN), 0, PROJ_TN // LANES)
        else:
            pc = _project(xc, n_ctx, None, w)
            ctx_kv = (pc.reshape(b, n_ctx, N_IN), P_K_OFF // LANES, P_V_OFF // LANES)
        h, xl = _mixer(h, _project(xl, seq, rope_tabs, w), seq, dft_l, ctx_kv, w, mods, lat_row, lam_init,
                       tm_l, tq, nxt)
        if not last:
            hc, xc = _mixer(hc, pc, n_ctx, dft_c, None, w, mods, ctx_row, lam_init, tm_c, n_ctx, nxt)
    return h.reshape(b, seq, d)
```

```python
import functools
import math

import numpy as np
import jax
import jax.numpy as jnp
from jax import lax
from jax.experimental import pallas as pl
from jax.experimental.pallas import tpu as pltpu

F32 = jnp.float32
BF16 = jnp.bfloat16

D_MODEL = 2048
GRID_W = 64
FOURIER_GROUPS = 4
FOURIER_GROUP_W = D_MODEL // 16
FOURIER_W = FOURIER_GROUPS * FOURIER_GROUP_W
CONV_W = D_MODEL // 4
N_HEADS = 8
HEAD_DIM = D_MODEL // (4 * N_HEADS)
VAL_DIM = 2 * HEAD_DIM
ATTN_QK_W = N_HEADS * 2 * HEAD_DIM
ATTN_V_W = N_HEADS * VAL_DIM
ROPE_THETA = 10000.0
ATTN_SCALE = HEAD_DIM ** -0.5
SUBLN_EPS = 1e-5
Q_OFF = FOURIER_W + 3 * CONV_W
K_OFF = Q_OFF + ATTN_QK_W
V_OFF = K_OFF + ATTN_QK_W
V_END = V_OFF + ATTN_V_W
N_IN = V_END + 3 * D_MODEL
D_FF = ((8 * D_MODEL // 3 + 255) // 256) * 256
EPS = 1e-6

LANES = 128
MXU_COLS = 256
BF16_SUBLANES = 16
VMEM_LIMIT = 56 * 1024 * 1024
MOD_ROWS = 8

PROJ_TN = 1024
PROJ_TM = 1024
P_GATE_OFF = Q_OFF
P_Q_OFF = P_GATE_OFF + 3 * D_MODEL
P_K_OFF = P_Q_OFF + ATTN_QK_W
P_V_OFF = P_K_OFF + ATTN_QK_W
PROJ_GATE_TILE = P_GATE_OFF // PROJ_TN
PROJ_Q_TILE = P_Q_OFF // PROJ_TN
PROJ_K_TILE = P_K_OFF // PROJ_TN
PROJ_TILES = N_IN // PROJ_TN
ATTN_TK = 512
ATTN_SCORES_AHEAD = 1
FOURIER_TR = 512
MERGE_TM = 256
FFN_TF = 512
FFN_UP_TM = 1024
FFN_DOWN_TM = 256


def _cparams(*sem):
    return pltpu.CompilerParams(dimension_semantics=sem, vmem_limit_bytes=VMEM_LIMIT)


def _mod_kernel(ct_ref, w_ref, b_ref, o_ref, *, n_rows):
    ct = ct_ref[...]
    st = ct * jax.nn.sigmoid(ct)
    w = w_ref[...]
    d, tn = w.shape
    rows = []
    for r in range(n_rows):
        t = (st[:, r:r + 1] * w).reshape(d // MOD_ROWS, MOD_ROWS, tn)
        rows.append(jnp.sum(jnp.sum(t, axis=0), axis=0, keepdims=True))
    rows.append(jnp.zeros((MOD_ROWS - n_rows, tn), F32))
    o_ref[...] = jnp.concatenate(rows, axis=0) + b_ref[...]


def _modulation(cvec_t, n_rows, w_mod, b_mod):
    depth, d, n = w_mod.shape
    tn = 1024
    return pl.pallas_call(
        functools.partial(_mod_kernel, n_rows=n_rows),
        grid=(depth, n // tn),
        in_specs=[
            pl.BlockSpec((d, MOD_ROWS), lambda l, j: (0, 0)),
            pl.BlockSpec((None, d, tn), lambda l, j: (l, 0, j)),
            pl.BlockSpec((None, 1, tn), lambda l, j: (l, 0, j)),
        ],
        out_specs=pl.BlockSpec((None, MOD_ROWS, tn), lambda l, j: (l, 0, j)),
        out_shape=jax.ShapeDtypeStruct((depth, MOD_ROWS, n), F32),
        compiler_params=_cparams("arbitrary", "arbitrary"),
        name="modulation",
    )(cvec_t, w_mod, b_mod.reshape(depth, 1, n))


def _mod_spec(which, mod_row, tm):
    return pl.BlockSpec((None, None, 1, D_MODEL), lambda i, *_: (mod_row(i * tm), which, 0, 0))


def _rms(x, g):
    return x * lax.rsqrt(jnp.mean(x * x, axis=-1, keepdims=True) + EPS) * g


def _norm_mod_kernel(h_ref, g_ref, sh_ref, sc_ref, o_ref):
    y = _rms(h_ref[...], g_ref[...])
    o_ref[...] = (y * (1.0 + sc_ref[...]) + sh_ref[...]).astype(o_ref.dtype)


def _norm_modulate(h, g, mods, mod_row, tm):
    m, d = h.shape
    return pl.pallas_call(
        _norm_mod_kernel,
        grid=(m // tm,),
        in_specs=[
            pl.BlockSpec((tm, d), lambda i: (i, 0)),
            pl.BlockSpec((1, d), lambda i: (0, 0)),
            _mod_spec(0, mod_row, tm),
            _mod_spec(1, mod_row, tm),
        ],
        out_specs=pl.BlockSpec((tm, d), lambda i: (i, 0)),
        out_shape=jax.ShapeDtypeStruct((m, d), BF16),
        compiler_params=_cparams("arbitrary"),
        name="norm_modulate",
    )(h, g.reshape(1, d), mods, mods)


def _proj_kernel(*refs, col0, rope):
    if rope:
        x_ref, w_ref, cos_ref, sin_up_ref, sin_dn_ref, o_ref, wb_ref = refs
    else:
        x_ref, w_ref, o_ref, wb_ref = refs
    j = pl.program_id(0) + col0

    @pl.when(pl.program_id(1) == 0)
    def _():
        wb_ref[...] = w_ref[...].astype(BF16)

    is_q = j == PROJ_Q_TILE
    is_k = j == PROJ_K_TILE
    x = x_ref[...]

    def column_chunks(epilogue):
        for c in range(PROJ_TN // MXU_COLS):
            cols = slice(c * MXU_COLS, (c + 1) * MXU_COLS)
            acc = jnp.dot(x, wb_ref[:, cols], preferred_element_type=F32)
            o_ref[:, cols] = epilogue(acc).astype(o_ref.dtype)

    def rotary(acc):
        qscale = jnp.where(is_q, ATTN_SCALE * math.log2(math.e), 1.0)
        half = HEAD_DIM // 4
        out = []
        for hd in range(MXU_COLS // LANES):
            a = acc[:, hd * LANES:(hd + 1) * LANES]
            if rope:
                a = (a * cos_ref[...] + pltpu.roll(a, LANES - half, 1) * sin_up_ref[...]
                     + pltpu.roll(a, half, 1) * sin_dn_ref[...])
            out.append(a * qscale)
        return jnp.concatenate(out, axis=1)

    pl.when(is_q | is_k)(lambda: column_chunks(rotary))
    pl.when(jnp.logical_not(is_q | is_k))(lambda: column_chunks(lambda acc: acc))


def _in_proj(xm, w_in, layer, rope_tabs, tm, col0, ncols, seq_len):
    m, d = xm.shape
    tn = PROJ_TN
    n_plain = PROJ_GATE_TILE
    n_gate = PROJ_Q_TILE - PROJ_GATE_TILE
    n_qkv = PROJ_TILES - PROJ_Q_TILE

    def w_tile(j):
        j = j + col0
        return jnp.where(j < n_plain, j, jnp.where(j < n_plain + n_gate, j + n_qkv, j - n_gate))

    n_i = m // tm
    row_tile = _back_and_forth(n_i)

    in_specs = [
        pl.BlockSpec((tm, d), lambda j, i: (row_tile(j, i), 0)),
        pl.BlockSpec((None, d, tn), lambda j, i: (layer, 0, w_tile(j))),
    ]
    args = [xm, w_in]
    rope = rope_tabs is not None
    if rope:
        tps = seq_len // tm
        in_specs += [pl.BlockSpec((tm, LANES), lambda j, i: (row_tile(j, i) % tps, 0))] * 3
        args += list(rope_tabs)
    return pl.pallas_call(
        functools.partial(_proj_kernel, col0=col0, rope=rope),
        grid=(ncols, n_i),
        in_specs=in_specs,
        out_specs=pl.BlockSpec((tm, tn), lambda j, i: (row_tile(j, i), j)),
        out_shape=jax.ShapeDtypeStruct((m, ncols * tn), BF16),
        scratch_shapes=[pltpu.VMEM((d, tn), BF16)],
        compiler_params=_cparams("arbitrary", "arbitrary"),
        name="in_proj",
    )(*args)


def _fourier_kernel(f_ref, cs_ref, c0_ref, s0_ref, rc_ref, rs_ref, o_ref, uc_ref, us_ref, *, chunk, scale):
    i = pl.program_id(1)
    seq = f_ref.shape[0]
    gw = FOURIER_GROUP_W

    @pl.when(i == 0)
    def _():
        for r in range(seq // chunk):
            rows = slice(r * chunk, (r + 1) * chunk)
            for g in range(FOURIER_GROUPS):
                cols = slice(g * gw, (g + 1) * gw)
                y = jnp.dot(f_ref[rows, cols], cs_ref[...], preferred_element_type=F32)
                uc_ref[rows, cols] = y[:, :gw].astype(BF16)
                us_ref[rows, cols] = y[:, gw:].astype(BF16)

    c0, s0 = c0_ref[...].astype(F32), s0_ref[...].astype(F32)
    rc, rs = rc_ref[pl.ds(i, 1), :], rs_ref[pl.ds(i, 1), :]
    cos_t = (c0 * rc - s0 * rs).astype(BF16)
    sin_t = (s0 * rc + c0 * rs).astype(BF16)
    y = jnp.dot(cos_t, uc_ref[...], preferred_element_type=F32)
    y -= jnp.dot(sin_t, us_ref[...], preferred_element_type=F32)
    o_ref[...] = (y * scale).astype(o_ref.dtype)


def _fourier(p3, cs, dft, tr):
    b, seq, _ = p3.shape
    chunk = min(seq, 1024)
    scale = 1.0 / math.sqrt(seq * FOURIER_GROUP_W)
    c0, s0, rc, rs = dft
    full = lambda a: pl.BlockSpec(a.shape, lambda bi, i: (0, 0), pipeline_mode=pl.Buffered(1))
    return pl.pallas_call(
        functools.partial(_fourier_kernel, chunk=chunk, scale=scale),
        grid=(b, seq // tr),
        in_specs=[
            pl.BlockSpec((None, seq, FOURIER_W), lambda bi, i: (bi, 0, 0)),
            pl.BlockSpec((FOURIER_GROUP_W, 2 * FOURIER_GROUP_W), lambda bi, i: (0, 0)),
            full(c0), full(s0), full(rc), full(rs),
        ],
        out_specs=pl.BlockSpec((None, tr, FOURIER_W), lambda bi, i: (bi, i, 0)),
        out_shape=jax.ShapeDtypeStruct((b, seq, FOURIER_W), BF16),
        scratch_shapes=[pltpu.VMEM((seq, FOURIER_W), BF16), pltpu.VMEM((seq, FOURIER_W), BF16)],
        compiler_params=_cparams("arbitrary", "arbitrary"),
        name="fourier",
    )(p3, cs, c0, s0, rc, rs)


def _attn_kernel(*refs, tq, tk, n_lat, lam_init):
    if n_lat:
        lam_ref, q_ref, kc_ref, vc_ref, kl_ref, vl_ref, g_ref, o_ref = refs
    else:
        lam_ref, q_ref, kc_ref, vc_ref, g_ref, o_ref = refs
        kl_ref = vl_ref = None
    n_ctx = kc_ref.shape[0]
    q = q_ref[...]
    lane = lax.broadcasted_iota(jnp.int32, (1, LANES), 1)
    comp0 = lane < HEAD_DIM
    zero = jnp.zeros_like(q)
    qq = jnp.concatenate([jnp.where(comp0, q, zero), jnp.where(comp0, zero, q)], axis=0)
    chunks = [(kc_ref, vc_ref, 0, n_ctx)] + [(kl_ref, vl_ref, c * tk, tk) for c in range(n_lat)]

    def lane_fold(x, op):
        part = x[:, 0:LANES]
        for c in range(1, x.shape[1] // LANES):
            part = op(part, x[:, c * LANES:(c + 1) * LANES])
        return part

    def scores(c):
        kr, _, r0, sz = chunks[c]
        return lax.dot_general(qq, kr[r0:r0 + sz, :], (((1,), (1,)), ((), ())), preferred_element_type=F32)

    m = l = acc = None
    ahead = ATTN_SCORES_AHEAD
    pending = [scores(c) for c in range(min(ahead, len(chunks)))]
    for c, (_, vr, r0, sz) in enumerate(chunks):
        s = pending.pop(0)
        if c + ahead < len(chunks):
            pending.append(scores(c + ahead))
        mc = jnp.max(lane_fold(s, jnp.maximum), axis=-1, keepdims=True)
        m_new = mc if m is None else jnp.maximum(m, mc)
        p = jnp.exp2(s - m_new)
        lc = lane_fold(p, jnp.add)
        y = jnp.dot(p.astype(BF16), vr[r0:r0 + sz, :], preferred_element_type=F32)
        if m is None:
            l, acc = lc, y
        else:
            alpha = jnp.exp2(m - m_new)
            l = alpha * l + lc
            acc = alpha * acc + y
        m = m_new
    o = acc / jnp.sum(l, axis=-1, keepdims=True)
    lf = lam_ref[...]
    lam = (jnp.exp(jnp.sum(lf[0:1] * lf[1:2], axis=-1, keepdims=True))
           - jnp.exp(jnp.sum(lf[2:3] * lf[3:4], axis=-1, keepdims=True)) + lam_init)
    o = o[:tq] - lam * o[tq:]
    o = o * lax.rsqrt(jnp.mean(o * o, axis=-1, keepdims=True) + SUBLN_EPS)
    o_ref[...] = (o * g_ref[...] * (1.0 - lam_init)).astype(o_ref.dtype)


def _attention(q3, qblk, c3, kcblk, vcblk, l3, lam, subln_g, lam_init, tq):
    b, lq, _ = q3.shape
    n_ctx = c3.shape[1]
    tk = ATTN_TK
    in_specs = [
        pl.BlockSpec((4, HEAD_DIM), lambda bi, h, i: (0, 0)),
        pl.BlockSpec((None, tq, LANES), lambda bi, h, i: (bi, i, qblk + h)),
        pl.BlockSpec((None, n_ctx, LANES), lambda bi, h, i: (bi, 0, kcblk + h)),
        pl.BlockSpec((None, n_ctx, LANES), lambda bi, h, i: (bi, 0, vcblk + h)),
    ]
    args = [lam, q3, c3, c3]
    n_lat = 0
    if l3 is not None:
        seq = l3.shape[1]
        n_lat = seq // min(tk, seq)
        tk = seq // n_lat
        in_specs += [
            pl.BlockSpec((None, seq, LANES), lambda bi, h, i: (bi, 0, P_K_OFF // LANES + h)),
            pl.BlockSpec((None, seq, LANES), lambda bi, h, i: (bi, 0, P_V_OFF // LANES + h)),
        ]
        args += [l3, l3]
    in_specs.append(pl.BlockSpec((1, VAL_DIM), lambda bi, h, i: (0, 0)))
    args.append(subln_g.reshape(1, VAL_DIM))
    return pl.pallas_call(
        functools.partial(_attn_kernel, tq=tq, tk=tk, n_lat=n_lat, lam_init=lam_init),
        grid=(b, N_HEADS, lq // tq),
        in_specs=in_specs,
        out_specs=pl.BlockSpec((None, tq, LANES), lambda bi, h, i: (bi, i, h)),
        out_shape=jax.ShapeDtypeStruct((b, lq, ATTN_V_W), BF16),
        compiler_params=_cparams("arbitrary", "arbitrary", "arbitrary"),
        name="diff_attention",
    )(*args)


def _conv3_rows(buf_ref, w, tm, base=BF16_SUBLANES):
    return (buf_ref[base - 1:base - 1 + tm, :] * w[0:1] + buf_ref[base:base + tm, :] * w[1:2]
            + buf_ref[base + 1:base + 1 + tm, :] * w[2:3])


def _halo_specs(tm, width, colblk, m, row_tile=lambda i: i):
    per = tm // BF16_SUBLANES
    last = m // BF16_SUBLANES - 1
    prev = pl.BlockSpec((BF16_SUBLANES, width),
                        lambda *ids: (jnp.maximum(row_tile(*ids) * per - 1, 0), colblk))
    nxt = pl.BlockSpec((BF16_SUBLANES, width),
                       lambda *ids: (jnp.minimum((row_tile(*ids) + 1) * per, last), colblk))
    return prev, nxt


def _back_and_forth(n_i):
    return lambda j, i: jnp.where(j % 2 == 0, i, n_i - 1 - i)


def _merge_kernel(f_ref, cb_ref, cc_ref, cx_ref, ccp_ref, cxp_ref, ccn_ref, cxn_ref, a_ref,
                  gf_ref, gc_ref, ga_ref, cw_ref, wbf_ref, wbc_ref, wba_ref, wo_ref,
                  h_ref, gt_ref, g2_ref, sh_ref, sc_ref, ho_ref, xo_ref, z_ref, mg_ref, *, tps, nchunk):
    i = pl.program_id(0)
    tm = cc_ref.shape[0]
    h = BF16_SUBLANES
    first = (i % tps) == 0
    last = (i % tps) == tps - 1
    zp = ccp_ref[...].astype(F32) * cxp_ref[...].astype(F32)
    zn = ccn_ref[...].astype(F32) * cxn_ref[...].astype(F32)
    z_ref[0:h, :] = jnp.where(first, 0.0, zp)
    z_ref[h:h + tm, :] = cc_ref[...].astype(F32) * cx_ref[...].astype(F32)
    z_ref[h + tm:2 * h + tm, :] = jnp.where(last, 0.0, zn)
    cv = (cb_ref[...].astype(F32) * _conv3_rows(z_ref, cw_ref[...], tm)).astype(BF16)
    f = f_ref[...]
    a = a_ref[...]
    cn = D_MODEL // nchunk
    for n in range(nchunk):
        cols = slice(n * cn, (n + 1) * cn)
        gate = lambda g_ref: jax.nn.sigmoid(g_ref[:, cols].astype(F32))
        y = gate(gf_ref) * jnp.dot(f, wbf_ref[:, cols], preferred_element_type=F32)
        y += gate(gc_ref) * jnp.dot(cv, wbc_ref[:, cols], preferred_element_type=F32)
        y += gate(ga_ref) * jnp.dot(a, wba_ref[:, cols], preferred_element_type=F32)
        mg_ref[:, cols] = y.astype(BF16)
    y = jnp.dot(mg_ref[...], wo_ref[...], preferred_element_type=F32)
    h_new = h_ref[...] + gt_ref[...] * y
    ho_ref[...] = h_new
    xo_ref[...] = (_rms(h_new, g2_ref[...]) * (1.0 + sc_ref[...]) + sh_ref[...]).astype(xo_ref.dtype)


def _merge_out(p, fmix, attn, conv_w, w_bf, w_bc, w_ba, w_out, layer, h, g2, mods, mod_row, tm, seq_len):
    m = p.shape[0]
    d = D_MODEL
    cw = CONV_W
    tps = seq_len // tm
    blk = lambda width, c: pl.BlockSpec((tm, width), lambda i: (i, c))
    ccp, ccn = _halo_specs(tm, cw, 2, m)
    cxp, cxn = _halo_specs(tm, cw, 3, m)
    gate0 = P_GATE_OFF // d
    full = lambda r, c: pl.BlockSpec((None, r, c), lambda i: (layer, 0, 0), pipeline_mode=pl.Buffered(1))
    return pl.pallas_call(
        functools.partial(_merge_kernel, tps=tps, nchunk=4),
        grid=(m // tm,),
        in_specs=[
            blk(FOURIER_W, 0),
            blk(cw, 1), blk(cw, 2), blk(cw, 3),
            ccp, cxp, ccn, cxn,
            blk(ATTN_V_W, 0),
            pl.BlockSpec((tm, d), lambda i: (i, gate0)),
            pl.BlockSpec((tm, d), lambda i: (i, gate0 + 1)),
            pl.BlockSpec((tm, d), lambda i: (i, gate0 + 2)),
            full(3, cw), full(FOURIER_W, d), full(cw, d), full(ATTN_V_W, d), full(d, d),
            pl.BlockSpec((tm, d), lambda i: (i, 0)),
            _mod_spec(2, mod_row, tm),
            pl.BlockSpec((1, d), lambda i: (0, 0)),
            _mod_spec(3, mod_row, tm),
            _mod_spec(4, mod_row, tm),
        ],
        out_specs=[pl.BlockSpec((tm, d), lambda i: (i, 0)), pl.BlockSpec((tm, d), lambda i: (i, 0))],
        out_shape=[jax.ShapeDtypeStruct((m, d), F32), jax.ShapeDtypeStruct((m, d), BF16)],
        scratch_shapes=[pltpu.VMEM((tm + 2 * BF16_SUBLANES, cw), F32), pltpu.VMEM((tm, d), BF16)],
        compiler_params=_cparams("arbitrary"),
        name="merge_out",
    )(fmix, p, p, p, p, p, p, p, attn, p, p, p, conv_w, w_bf, w_bc, w_ba, w_out,
      h, mods, g2.reshape(1, d), mods, mods)


def _ffn_up_kernel(*refs, tps, nseq, cast_down, row_tile):
    if cast_down:
        (x_ref, xp_ref, xn_ref, wa_ref, wv_ref, ca_ref, cv_ref, wd_ref, o_ref, wdb_ref,
         wab_ref, wvb_ref, ua_ref, uv_ref) = refs
    else:
        (x_ref, xp_ref, xn_ref, wa_ref, wv_ref, ca_ref, cv_ref, o_ref,
         wab_ref, wvb_ref, ua_ref, uv_ref) = refs
    i = pl.program_id(1)
    tm = x_ref.shape[0]
    hl = BF16_SUBLANES
    seq = tm // nseq

    @pl.when(i == 0)
    def _():
        wab_ref[...] = wa_ref[...].astype(BF16)
        wvb_ref[...] = wv_ref[...].astype(BF16)
        if cast_down:
            wdb_ref[...] = wd_ref[...].astype(BF16)

    base = [hl + s * (seq + hl) for s in range(nseq)]
    zeros = jnp.zeros((hl, x_ref.shape[1]), BF16)
    if nseq == 1:
        it = row_tile(pl.program_id(0), i)
        first = (it % tps) == 0
        last = (it % tps) == tps - 1
        pieces = [jnp.where(first, zeros, xp_ref[...]), x_ref[...], jnp.where(last, zeros, xn_ref[...])]
    else:
        pieces = [zeros]
        for s in range(nseq):
            pieces += [x_ref[s * seq:(s + 1) * seq, :], zeros]
    xh = jnp.concatenate(pieces, axis=0)
    ua_ref[...] = jnp.dot(xh, wab_ref[...], preferred_element_type=F32)
    uv_ref[...] = jnp.dot(xh, wvb_ref[...], preferred_element_type=F32)
    for s in range(nseq):
        a = _conv3_rows(ua_ref, ca_ref[...], seq, base[s])
        v = _conv3_rows(uv_ref, cv_ref[...], seq, base[s])
        o_ref[s * seq:(s + 1) * seq, :] = (a * jax.nn.sigmoid(a) * v).astype(o_ref.dtype)


def _ffn_down_kernel(a_ref, w_ref, h_ref, gt_ref, g_ref, sh_ref, sc_ref, ho_ref, xo_ref):
    y = jnp.dot(a_ref[...], w_ref[...], preferred_element_type=F32)
    h_new = h_ref[...] + gt_ref[...] * y
    ho_ref[...] = h_new
    xo_ref[...] = (_rms(h_new, g_ref[...]) * (1.0 + sc_ref[...]) + sh_ref[...]).astype(xo_ref.dtype)


def _ffn_down_final_kernel(a_ref, w_ref, h_ref, gt_ref, g_ref, o_ref):
    y = jnp.dot(a_ref[...], w_ref[...], preferred_element_type=F32)
    o_ref[...] = _rms(h_ref[...] + gt_ref[...] * y, g_ref[...])


def _ffn(xm, w_up, conv_w, w_down, w_down_b, layer, h, mods, mod_row, seq_len, nxt):
    m, d = xm.shape
    tf = FFN_TF
    nj = D_FF // tf
    tm = FFN_UP_TM if seq_len >= FFN_UP_TM else min(m, FFN_UP_TM)
    tps = max(seq_len // tm, 1)
    nseq = max(tm // seq_len, 1)
    stage_rows = nseq * (tm // nseq + BF16_SUBLANES) + BF16_SUBLANES
    row_tile = _back_and_forth(m // tm)
    xp, xn = _halo_specs(tm, d, 0, m, row_tile)
    cast_down = w_down_b is None
    in_specs = [
        pl.BlockSpec((tm, d), lambda j, i: (row_tile(j, i), 0)),
        xp, xn,
        pl.BlockSpec((None, d, tf), lambda j, i: (layer, 0, j)),
        pl.BlockSpec((None, d, tf), lambda j, i: (layer, 0, nj + j)),
        pl.BlockSpec((None, 3, tf), lambda j, i: (layer, 0, j)),
        pl.BlockSpec((None, 3, tf), lambda j, i: (layer, 0, nj + j)),
    ]
    args = [xm, xm, xm, w_up, w_up, conv_w, conv_w]
    out_specs = [pl.BlockSpec((tm, tf), lambda j, i: (row_tile(j, i), j))]
    out_shape = [jax.ShapeDtypeStruct((m, D_FF), BF16)]
    if cast_down:
        in_specs.append(pl.BlockSpec((None, tf, d), lambda j, i: (layer, j, 0)))
        args.append(w_down)
        out_specs.append(pl.BlockSpec((tf, d), lambda j, i: (j, 0)))
        out_shape.append(jax.ShapeDtypeStruct((D_FF, d), BF16))
    outs = pl.pallas_call(
        functools.partial(_ffn_up_kernel, tps=tps, nseq=nseq, cast_down=cast_down, row_tile=row_tile),
        grid=(nj, m // tm),
        in_specs=in_specs,
        out_specs=out_specs,
        out_shape=out_shape,
        scratch_shapes=[
            pltpu.VMEM((d, tf), BF16),
            pltpu.VMEM((d, tf), BF16),
            pltpu.VMEM((stage_rows, tf), F32),
            pltpu.VMEM((stage_rows, tf), F32),
        ],
        compiler_params=_cparams("arbitrary", "arbitrary"),
        name="ffn_up",
    )(*args)
    act = outs[0]
    if cast_down:
        w_down_b = outs[1]

    tmd = min(m, FFN_DOWN_TM)
    row = lambda i: (i, 0)
    in_specs = [
        pl.BlockSpec((tmd, D_FF), row),
        pl.BlockSpec((D_FF, d), lambda i: (0, 0), pipeline_mode=pl.Buffered(1)),
        pl.BlockSpec((tmd, d), row),
        _mod_spec(5, mod_row, tmd),
        pl.BlockSpec((1, d), lambda i: (0, 0)),
    ]
    g_next, mods_next = nxt
    if mods_next is None:
        return pl.pallas_call(
            _ffn_down_final_kernel,
            grid=(m // tmd,),
            in_specs=in_specs,
            out_specs=pl.BlockSpec((tmd, d), row),
            out_shape=jax.ShapeDtypeStruct((m, d), F32),
            compiler_params=_cparams("arbitrary"),
            name="ffn_down_final",
        )(act, w_down_b, h, mods, g_next.reshape(1, d)), None, w_down_b
    h_new, x_next = pl.pallas_call(
        _ffn_down_kernel,
        grid=(m // tmd,),
        in_specs=in_specs + [_mod_spec(0, mod_row, tmd), _mod_spec(1, mod_row, tmd)],
        out_specs=[pl.BlockSpec((tmd, d), row), pl.BlockSpec((tmd, d), row)],
        out_shape=[jax.ShapeDtypeStruct((m, d), F32), jax.ShapeDtypeStruct((m, d), BF16)],
        compiler_params=_cparams("arbitrary"),
        name="ffn_down",
    )(act, w_down_b, h, mods, g_next.reshape(1, d), mods_next, mods_next)
    return h_new, x_next, w_down_b


def _rope_tables(length):
    n_freq = HEAD_DIM // 4
    pos = jnp.arange(length)
    row = (pos // GRID_W).astype(F32)
    col = (pos % GRID_W).astype(F32)
    inv = ROPE_THETA ** (-(2.0 * jnp.arange(n_freq, dtype=F32)) / (HEAD_DIM // 2))
    ang_r, ang_c = row[:, None] * inv, col[:, None] * inv
    ang = jnp.concatenate([ang_r, ang_r, ang_c, ang_c], axis=1)
    cos = jnp.tile(jnp.cos(ang), (1, 2))
    sin = jnp.tile(jnp.sin(ang), (1, 2))
    is_x1 = (jnp.arange(LANES) % (2 * n_freq)) < n_freq
    return cos, jnp.where(is_x1, -sin, 0.0), jnp.where(is_x1, 0.0, sin)


def _dft_tables(n, tr):
    m = jnp.arange(n, dtype=jnp.int32)

    def cos_sin(rows):
        ang = ((rows[:, None] * m[None, :]) % n).astype(F32) * (2.0 * math.pi / n)
        return jnp.cos(ang), jnp.sin(ang)

    rc, rs = cos_sin(jnp.arange(n // tr, dtype=jnp.int32) * tr)
    g = 1 << ((tr.bit_length() - 1 + 1) // 2)
    assert tr % g == 0
    ca, sa = cos_sin(jnp.arange(tr // g, dtype=jnp.int32) * g)
    cb, sb = cos_sin(jnp.arange(g, dtype=jnp.int32))
    c0 = (ca[:, None, :] * cb[None, :, :] - sa[:, None, :] * sb[None, :, :]).reshape(tr, n)
    s0 = (sa[:, None, :] * cb[None, :, :] + ca[:, None, :] * sb[None, :, :]).reshape(tr, n)
    return c0.astype(BF16), s0.astype(BF16), rc, rs


def _channel_dft():
    c = np.arange(FOURIER_GROUP_W)
    ang = 2.0 * np.pi * ((c[:, None] * c[None, :]) % FOURIER_GROUP_W) / FOURIER_GROUP_W
    return jnp.asarray(np.concatenate([np.cos(ang), np.sin(ang)], axis=1), dtype=F32).astype(BF16)


def _project(xm, seq_len, rope_tabs, w, proj_cols=(0, PROJ_TILES)):
    return _in_proj(xm, w["w_in"], w["layer"], rope_tabs, min(xm.shape[0], PROJ_TM), proj_cols[0],
                    proj_cols[1], seq_len)


def _mixer(h, p, seq_len, dft, ctx_kv, w, mods, mod_row, lam_init, tm, tq, nxt):
    m = h.shape[0]
    b = m // seq_len
    p3 = p.reshape(b, seq_len, N_IN)
    fmix = _fourier(p3, w["cs"], dft, min(seq_len, FOURIER_TR)).reshape(m, FOURIER_W)
    if ctx_kv is None:
        attn = _attention(p3, P_Q_OFF // LANES, p3, P_K_OFF // LANES, P_V_OFF // LANES, None,
                          w["lambdas"], w["subln_g"], lam_init, tq)
    else:
        c3, kcblk, vcblk = ctx_kv
        attn = _attention(p3, P_Q_OFF // LANES, c3, kcblk, vcblk, p3,
                          w["lambdas"], w["subln_g"], lam_init, tq)
    h_mid, xm2 = _merge_out(p, fmix, attn.reshape(m, ATTN_V_W), w["conv_mix_w"], w["w_br_fourier"],
                            w["w_br_conv"], w["w_br_attn"], w["w_out"], w["layer"], h, w["g_norm2"],
                            mods, mod_row, min(tm, MERGE_TM), seq_len)
    h_new, x_next, w["w_ffn_down_b"] = _ffn(xm2, w["w_ffn_up"], w["ffn_conv_w"], w["w_ffn_down"],
                                            w["w_ffn_down_b"], w["layer"], h_mid, mods, mod_row, seq_len, nxt)
    return h_new, x_next


def kernel(x, c, ctx, c_ctx, w_mod, b_mod, g_norm1, g_norm2, w_in, conv_mix_w, lambdas, subln_g,
           w_br_fourier, w_br_conv, w_br_attn, w_out, w_ffn_up, ffn_conv_w, w_ffn_down, g_final):
    b, seq, d = x.shape
    n_ctx = ctx.shape[1]
    depth = w_mod.shape[0]
    assert d == D_MODEL and b + 1 <= MOD_ROWS

    wb = lambda a: a.astype(BF16)
    w_bf, w_bc, w_ba, w_o = wb(w_br_fourier), wb(w_br_conv), wb(w_br_attn), wb(w_out)

    rope_tabs = _rope_tables(seq)
    dft_l = _dft_tables(seq, min(seq, FOURIER_TR))
    dft_c = _dft_tables(n_ctx, min(n_ctx, FOURIER_TR))
    cs = _channel_dft()

    cvec = jnp.zeros((MOD_ROWS, d), F32).at[:b].set(c).at[b].set(c_ctx)
    mods_all = _modulation(cvec.T, b + 1, w_mod, b_mod).reshape(depth, MOD_ROWS, 6, 1, d)

    tm_l = min(512, seq)
    tm_c = n_ctx
    tq = min(512, seq)
    lat_row = lambda r: r // seq
    ctx_row = lambda r: b

    h = x.reshape(b * seq, d)
    hc = ctx.reshape(b * n_ctx, d)
    xc = _norm_modulate(hc, g_norm1[0], mods_all[0], ctx_row, tm_c)
    xl = _norm_modulate(h, g_norm1[0], mods_all[0], lat_row, tm_l)
    for i in range(depth):
        last = i == depth - 1
        lam_init = 0.8 - 0.6 * math.exp(-0.3 * i)
        mods = mods_all[i]
        w = dict(w_in=w_in, layer=i, cs=cs, lambdas=lambdas[i], subln_g=subln_g[i], conv_mix_w=conv_mix_w,
                 w_br_fourier=w_bf, w_br_conv=w_bc, w_br_attn=w_ba, w_out=w_o,
                 g_norm2=g_norm2[i], w_ffn_up=w_ffn_up, ffn_conv_w=ffn_conv_w, w_ffn_down=w_ffn_down,
                 w_ffn_down_b=None)
        nxt = (g_final, None) if last else (g_norm1[i + 1], mods_all[i + 1])

        if last:
            pc = _project(xc, n_ctx, None, w, proj_cols=(PROJ_K_TILE, 2))
            ctx_kv = (pc.reshape(b, n_ctx, 2 * PROJ_TN), 0, PROJ_TN // LANES)
        else:
            pc = _project(xc, n_ctx, None, w)
            ctx_kv = (pc.reshape(b, n_ctx, N_IN), P_K_OFF // LANES, P_V_OFF // LANES)
        h, xl = _mixer(h, _project(xl, seq, rope_tabs, w), seq, dft_l, ctx_kv, w, mods, lat_row, lam_init,
                       tm_l, tq, nxt)
        if not last:
            hc, xc = _mixer(hc, pc, n_ctx, dft_c, None, w, mods, ctx_row, lam_init, tm_c, n_ctx, nxt)
    return h.reshape(b, seq, d)
```
